```python
import jax
import jax.numpy as jnp
from jax import lax
import numpy as np

D_MODEL = 2048
BATCH = 8
SEQ = 2048
DEPTH = 1

HEAD_DIM = 128
MIX_WIDTH = D_MODEL
ATTN_WIDTH = MIX_WIDTH // 2
GMLP_WIDTH = MIX_WIDTH - ATTN_WIDTH
ATTN_HEADS = ATTN_WIDTH // HEAD_DIM
KV_HEADS = max(1, ATTN_HEADS // 4)
Q_PER_KV = ATTN_HEADS // KV_HEADS
KV_WIDTH = KV_HEADS * HEAD_DIM
GMLP_HEAD_DIM = 128
GMLP_HEADS = GMLP_WIDTH // GMLP_HEAD_DIM
IN_WIDTH = ATTN_WIDTH + 2 * KV_WIDTH + 2 * GMLP_WIDTH
WINDOW = 128
BLOCK = 128
CHUNK = 128
ROPE_THETA = 500000.0
ROPE_DIM = HEAD_DIM // 4
N_GROUPS = 4
EXPERTS_PER_GROUP = 4
TOP_K = 2
D_EXPERT = D_MODEL // 2
N_MOD = 6
EPS = 1e-6
NEG_INF = -1e30

kernel_name = 'hybrid_swa_gmlp_hmoe_block'


def rms_norm(x, g):
    xf = x.astype(jnp.float32)
    y = xf * lax.rsqrt(jnp.mean(xf * xf, axis=-1, keepdims=True) + EPS)
    return (y * g.astype(jnp.float32)).astype(x.dtype)


def partial_rope(x, positions):
    half = ROPE_DIM // 2
    inv_freq = ROPE_THETA ** (-jnp.arange(half, dtype=jnp.float32) * 2.0 / ROPE_DIM)
    ang = positions.astype(jnp.float32)[..., None] * inv_freq
    cos = jnp.cos(ang)[:, :, None, :]
    sin = jnp.sin(ang)[:, :, None, :]
    xf = x.astype(jnp.float32)
    x1 = xf[..., :half]
    x2 = xf[..., half:ROPE_DIM]
    out = jnp.concatenate([x1 * cos - x2 * sin, x2 * cos + x1 * sin, xf[..., ROPE_DIM:]], axis=-1)
    return out.astype(x.dtype)


def window_attention(q, k, v, sink):
    B, S = q.shape[0], q.shape[1]
    nb = S // BLOCK
    qb = q.reshape(B, nb, BLOCK, KV_HEADS, Q_PER_KV, HEAD_DIM)

    def band(t):
        tp = jnp.pad(t, ((0, 0), (BLOCK, BLOCK), (0, 0), (0, 0)))
        tb = tp.reshape(B, nb + 2, BLOCK, KV_HEADS, HEAD_DIM)
        return jnp.concatenate([tb[:, :-2], tb[:, 1:-1], tb[:, 2:]], axis=2)

    kb = band(k)
    vb = band(v)
    s = jnp.einsum('bnqhgd,bnkhd->bnhgqk', qb, kb).astype(jnp.float32) * (HEAD_DIM ** -0.5)
    q_off = jnp.arange(BLOCK)[:, None]
    k_off = jnp.arange(3 * BLOCK)[None, :] - BLOCK
    k_glob = jnp.arange(nb)[:, None, None] * BLOCK + k_off[None]
    valid = (jnp.abs(k_off - q_off) <= WINDOW)[None] & (k_glob >= 0) & (k_glob < S)
    s = jnp.where(valid[None, :, None, None], s, NEG_INF)
    sink_col = jnp.broadcast_to(sink.astype(jnp.float32).reshape(1, 1, KV_HEADS, Q_PER_KV, 1, 1),
                                s.shape[:-1] + (1,))
    p = jax.nn.softmax(jnp.concatenate([s, sink_col], axis=-1), axis=-1)[..., :-1]
    o = jnp.einsum('bnhgqk,bnkhd->bnqhgd', p.astype(v.dtype), vb)
    return o.reshape(B, S, ATTN_WIDTH)


def spatial_gating(u, v, g_norm, w_s, b_s):
    B, S = u.shape[0], u.shape[1]
    nc = S // CHUNK
    u = jax.nn.gelu(u, approximate=False).reshape(B, S, GMLP_HEADS, GMLP_HEAD_DIM)
    vf = jax.nn.gelu(v.astype(jnp.float32), approximate=False).reshape(B, S, GMLP_HEADS, GMLP_HEAD_DIM)
    mu = jnp.mean(vf, axis=-1, keepdims=True)
    var = jnp.mean(jnp.square(vf - mu), axis=-1, keepdims=True)
    vf = (vf - mu) * lax.rsqrt(var + EPS) * g_norm.astype(jnp.float32)
    vc = vf.astype(u.dtype).reshape(B, nc, CHUNK, GMLP_HEADS, GMLP_HEAD_DIM)
    mixed = jnp.einsum('hpq,bcqhd->bcphd', w_s, vc) + b_s.T[None, None, :, :, None]
    return (u * mixed.reshape(B, S, GMLP_HEADS, GMLP_HEAD_DIM)).reshape(B, S, GMLP_WIDTH)


def token_mixer(a, positions, w_in, sink, g_norm, w_s, b_s, g_group, w_out):
    B, S = a.shape[0], a.shape[1]
    z = a @ w_in
    o0 = ATTN_WIDTH
    o1 = o0 + KV_WIDTH
    o2 = o1 + KV_WIDTH
    o3 = o2 + GMLP_WIDTH
    q = partial_rope(z[..., :o0].reshape(B, S, ATTN_HEADS, HEAD_DIM), positions)
    k = partial_rope(z[..., o0:o1].reshape(B, S, KV_HEADS, HEAD_DIM), positions)
    v = z[..., o1:o2].reshape(B, S, KV_HEADS, HEAD_DIM)
    y_attn = window_attention(q, k, v, sink)
    y_gmlp = spatial_gating(z[..., o2:o3], z[..., o3:], g_norm, w_s, b_s)
    y = jnp.concatenate([rms_norm(y_attn, g_group[:ATTN_WIDTH]),
                         rms_norm(y_gmlp, g_group[ATTN_WIDTH:])], axis=-1)
    return y @ w_out


def hier_moe(f, w_rg, b_rg, w_re, b_re, w_gate, w_up, w_down):
    B, S, D = f.shape
    t = f.reshape(B * S, D)
    g_logits = (t @ w_rg).astype(jnp.float32) + b_rg.astype(jnp.float32)
    g_val, g_idx = lax.top_k(jax.nn.softmax(g_logits, axis=-1), 1)
    e_logits = jnp.einsum('td,dge->tge', t, w_re).astype(jnp.float32) + b_re.astype(jnp.float32)
    e_sel = jnp.take_along_axis(e_logits, g_idx[:, :, None], axis=1)[:, 0]
    e_val, e_idx = lax.top_k(jax.nn.softmax(e_sel, axis=-1), TOP_K)
    e_w = e_val / jnp.sum(e_val, axis=-1, keepdims=True) * g_val
    comb = jnp.sum(jax.nn.one_hot(e_idx, EXPERTS_PER_GROUP, dtype=jnp.float32) * e_w[..., None], axis=1)
    comb = jax.nn.one_hot(g_idx[:, 0], N_GROUPS, dtype=jnp.float32)[:, :, None] * comb[:, None, :]
    comb = comb.astype(t.dtype)
    y = jnp.zeros_like(t)
    for g in range(N_GROUPS):
        h_gate = jnp.einsum('td,edf->etf', t, w_gate[g])
        h_up = jnp.einsum('td,edf->etf', t, w_up[g])
        hid = jax.nn.silu(h_gate) * h_up * comb[:, g].T[:, :, None]
        y = y + jnp.einsum('etf,efd->td', hid, w_down[g])
    return y.reshape(B, S, D)


def setup_inputs(seed: int = 0) -> dict:
    key = jax.random.key(seed)
    ks = jax.random.split(key, 24)
    f32 = jnp.float32

    def nrm(k, shape, scale):
        return jax.random.normal(k, shape, f32) * scale

    def gain(k, shape):
        return 1.0 + 0.02 * jax.random.normal(k, shape, f32)

    G, E, D, F = N_GROUPS, EXPERTS_PER_GROUP, D_MODEL, D_EXPERT
    x = nrm(ks[0], (BATCH, SEQ, D), 1.0)
    c = nrm(ks[1], (BATCH, D), 1.0)
    offset = jax.random.randint(ks[2], (BATCH, 1), 0, 4096, dtype=jnp.int32)
    positions = offset + jnp.arange(SEQ, dtype=jnp.int32)[None, :]
    return {
        'x': x,
        'c': c,
        'positions': positions,
        'w_ada': nrm(ks[3], (DEPTH, D, N_MOD * D), 0.5 * D ** -0.5),
        'b_ada': nrm(ks[4], (DEPTH, N_MOD * D), 0.02),
        'g_mix_pre': gain(ks[5], (DEPTH, D)),
        'g_mix_post': gain(ks[6], (DEPTH, D)),
        'w_in': nrm(ks[7], (DEPTH, D, IN_WIDTH), D ** -0.5),
        'sink_logits': nrm(ks[8], (DEPTH, ATTN_HEADS), 1.0),
        'gmlp_norm_g': gain(ks[9], (DEPTH, GMLP_HEADS, GMLP_HEAD_DIM)),
        'w_spatial': nrm(ks[10], (DEPTH, GMLP_HEADS, CHUNK, CHUNK), CHUNK ** -0.5),
        'b_spatial': gain(ks[11], (DEPTH, GMLP_HEADS, CHUNK)),
        'g_group_out': gain(ks[12], (DEPTH, MIX_WIDTH)),
        'w_out': nrm(ks[13], (DEPTH, MIX_WIDTH, D), MIX_WIDTH ** -0.5),
        'g_ffn_pre': gain(ks[14], (DEPTH, D)),
        'g_ffn_post': gain(ks[15], (DEPTH, D)),
        'w_router_group': nrm(ks[16], (DEPTH, D, G), D ** -0.5),
        'b_router_group': nrm(ks[17], (DEPTH, G), 0.01),
        'w_router_expert': nrm(ks[18], (DEPTH, D, G, E), D ** -0.5),
        'b_router_expert': nrm(ks[19], (DEPTH, G, E), 0.01),
        'w_gate': nrm(ks[20], (DEPTH, G, E, D, F), D ** -0.5),
        'w_up': nrm(ks[21], (DEPTH, G, E, D, F), D ** -0.5),
        'w_down': nrm(ks[22], (DEPTH, G, E, F, D), F ** -0.5),
    }


def reference(x, c, positions, w_ada, b_ada, g_mix_pre, g_mix_post, w_in, sink_logits,
              gmlp_norm_g, w_spatial, b_spatial, g_group_out, w_out, g_ffn_pre, g_ffn_post,
              w_router_group, b_router_group, w_router_expert, b_router_expert,
              w_gate, w_up, w_down):
    h = x
    c_act = jax.nn.silu(c)
    for l in range(DEPTH):
        mod = c_act @ w_ada[l] + b_ada[l]
        sh1, sc1, gt1, sh2, sc2, gt2 = jnp.split(mod[:, None, :], N_MOD, axis=-1)
        a = rms_norm(h, g_mix_pre[l]) * (1 + sc1) + sh1
        mix = token_mixer(a, positions, w_in[l], sink_logits[l], gmlp_norm_g[l],
                          w_spatial[l], b_spatial[l], g_group_out[l], w_out[l])
        h = h + gt1 * rms_norm(mix, g_mix_post[l])
        f = rms_norm(h, g_ffn_pre[l]) * (1 + sc2) + sh2
        y = hier_moe(f, w_router_group[l], b_router_group[l], w_router_expert[l],
                     b_router_expert[l], w_gate[l], w_up[l], w_down[l])
        h = h + gt2 * rms_norm(y, g_ffn_post[l])
    return h
```

```python
import functools

import jax
import jax.numpy as jnp
from jax import lax
from jax.experimental import pallas as pl
from jax.experimental.pallas import tpu as pltpu

F32 = jnp.float32
BF16 = jnp.bfloat16
I32 = jnp.int32

HEAD_DIM = 128
ATTN_HEADS = 8
KV_HEADS = 2
Q_PER_KV = ATTN_HEADS // KV_HEADS
GMLP_HEADS = 8
WINDOW = 128
BLOCK = 128
ROPE_THETA = 500000.0
ROPE_DIM = HEAD_DIM // 4
ROPE_HALF = ROPE_DIM // 2
N_GROUPS = 4
EXPERTS_PER_GROUP = 4
N_EXPERTS = N_GROUPS * EXPERTS_PER_GROUP
TOP_K = 2
N_MOD = 6
EPS = 1e-6
NEG_INF = -1e30
LANES = 128
PAD_HI_LANE = 32
ROUTER_ROWS = 32

SLOT_TILE = 256
SLOT_TILE_LOG2 = 8
VMEM_LIMIT = 52 * 1024 * 1024


def _rms(x, g):
    ms = jnp.mean(x * x, axis=-1, keepdims=True)
    return x * lax.rsqrt(ms + EPS) * g


def _silu(x):
    return x / (1.0 + jnp.exp(-x))


def _gelu(x):
    return 0.5 * x * (1.0 + lax.erf(x * 0.7071067811865476))


def _ada_kernel(c_ref, w_ref, b_ref, o_ref):
    ca = _silu(c_ref[...]).astype(BF16)
    o_ref[...] = jnp.dot(ca, w_ref[...].astype(BF16), preferred_element_type=F32) + b_ref[...]


def _ada(c, w, b):
    bsz, d = c.shape
    n = w.shape[1]
    tn = 1024
    return pl.pallas_call(
        _ada_kernel,
        grid=(n // tn,),
        in_specs=[pl.BlockSpec((bsz, d), lambda j: (0, 0)),
                  pl.BlockSpec((d, tn), lambda j: (0, j)),
                  pl.BlockSpec((1, tn), lambda j: (0, j))],
        out_specs=pl.BlockSpec((bsz, tn), lambda j: (0, j)),
        out_shape=jax.ShapeDtypeStruct((bsz, n), F32),
        compiler_params=pltpu.CompilerParams(dimension_semantics=("arbitrary",),
                                             vmem_limit_bytes=VMEM_LIMIT),
        name="ada",
    )(c, w, b.reshape(1, n))


def _inproj_kernel(x_ref, mod_ref, g_ref, w_ref, o_ref, a_scr):
    @pl.when(pl.program_id(2) == 0)
    def _():
        mod = mod_ref[0]
        a = _rms(x_ref[0], g_ref[...]) * (1.0 + mod[1:2]) + mod[0:1]
        a_scr[...] = a.astype(BF16)

    o_ref[0] = jnp.dot(a_scr[...], w_ref[...].astype(BF16),
                       preferred_element_type=F32).astype(BF16)


def _inproj(x, mod, g, w):
    bsz, s, d = x.shape
    n = w.shape[1]
    tm = min(1024, s)
    tn = 512
    return pl.pallas_call(
        _inproj_kernel,
        grid=(bsz, s // tm, n // tn),
        in_specs=[pl.BlockSpec((1, tm, d), lambda b, i, j: (b, i, 0)),
                  pl.BlockSpec((1, N_MOD, d), lambda b, i, j: (b, 0, 0)),
                  pl.BlockSpec((1, d), lambda b, i, j: (0, 0)),
                  pl.BlockSpec((d, tn), lambda b, i, j: (0, j))],
        out_specs=pl.BlockSpec((1, tm, tn), lambda b, i, j: (b, i, j)),
        out_shape=jax.ShapeDtypeStruct((bsz, s, n), BF16),
        scratch_shapes=[pltpu.VMEM((tm, d), BF16)],
        compiler_params=pltpu.CompilerParams(
            dimension_semantics=("arbitrary", "arbitrary", "arbitrary"),
            vmem_limit_bytes=VMEM_LIMIT),
        name="inproj",
    )(x, mod, g.reshape(1, d), w)


def _rope_tables(pos, invf):
    ang = pos.astype(F32) * invf
    lane = lax.broadcasted_iota(I32, ang.shape, 1)
    cos = jnp.where(lane < ROPE_DIM, jnp.cos(ang), 1.0)
    sin = jnp.sin(ang)
    sin = jnp.where(lane < ROPE_HALF, -sin, jnp.where(lane < ROPE_DIM, sin, 0.0))
    return cos, sin


def _rope(x, cos, sin):
    lane = lax.broadcasted_iota(I32, x.shape, 1)
    partner = jnp.where(lane < ROPE_HALF,
                        pltpu.roll(x, HEAD_DIM - ROPE_HALF, 1),
                        pltpu.roll(x, ROPE_HALF, 1))
    return x * cos + partner * sin


def _mixer_kernel(sink_ref, q_ref, kvp_ref, kvc_ref, kvn_ref, u0_ref, u1_ref, v0_ref, v1_ref,
                  posp_ref, posc_ref, posn_ref, invf_ref, gn_ref, ws_ref, bst_ref, gg_ref,
                  o_ref, ya_scr, yg_scr, *, seq_len):
    tq = q_ref.shape[1]
    nsub = tq // BLOCK
    aw = ATTN_HEADS * HEAD_DIM
    i = pl.program_id(1)
    invf = invf_ref[...]

    cos_c, sin_c = _rope_tables(posc_ref[0], invf)
    tabs = (_rope_tables(posp_ref[0], invf), (cos_c, sin_c), _rope_tables(posn_ref[0], invf))
    kbands, vbands = [], []
    for h in range(KV_HEADS):
        kparts, vparts = [], []
        for ref, (cs, sn) in zip((kvp_ref, kvc_ref, kvn_ref), tabs):
            k = ref[0, :, h * HEAD_DIM:(h + 1) * HEAD_DIM].astype(F32)
            kparts.append(_rope(k, cs, sn).astype(BF16))
            vparts.append(ref[0, :, (KV_HEADS + h) * HEAD_DIM:(KV_HEADS + h + 1) * HEAD_DIM])
        kbands.append(jnp.concatenate(kparts, axis=0))
        vbands.append(jnp.concatenate(vparts, axis=0))

    scale = HEAD_DIM ** -0.5
    cos_q, sin_q = cos_c * scale, sin_c * scale
    qh = [_rope(q_ref[0, :, h * HEAD_DIM:(h + 1) * HEAD_DIM].astype(F32), cos_q, sin_q).astype(BF16)
          for h in range(ATTN_HEADS)]

    rows = Q_PER_KV * BLOCK
    band = 3 * BLOCK
    r_io = lax.broadcasted_iota(I32, (rows, band), 0)
    c_io = lax.broadcasted_iota(I32, (rows, band), 1)
    rel = c_io - BLOCK - (r_io & (BLOCK - 1))
    in_window = jnp.abs(rel) <= WINDOW
    for s in range(nsub):
        kglob = i * tq + (s - 1) * BLOCK + c_io
        valid = in_window & (kglob >= 0) & (kglob < seq_len)
        for h in range(KV_HEADS):
            q4 = jnp.concatenate([qh[h * Q_PER_KV + g][s * BLOCK:(s + 1) * BLOCK]
                                  for g in range(Q_PER_KV)], axis=0)
            kb = kbands[h][s * BLOCK:s * BLOCK + band]
            vb = vbands[h][s * BLOCK:s * BLOCK + band]
            sc = lax.dot_general(q4, kb, (((1,), (1,)), ((), ())), preferred_element_type=F32)
            sc = jnp.where(valid, sc, NEG_INF)
            sink = jnp.concatenate([jnp.full((BLOCK, 1), sink_ref[h * Q_PER_KV + g], F32)
                                    for g in range(Q_PER_KV)], axis=0)
            m = jnp.maximum(jnp.max(sc, axis=-1, keepdims=True), sink)
            p = jnp.exp(sc - m)
            den = jnp.sum(p, axis=-1, keepdims=True) + jnp.exp(sink - m)
            o = jnp.dot(p.astype(BF16), vb, preferred_element_type=F32) / den
            for g in range(Q_PER_KV):
                hq = h * Q_PER_KV + g
                ya_scr[s * BLOCK:(s + 1) * BLOCK, hq * HEAD_DIM:(hq + 1) * HEAD_DIM] = (
                    o[g * BLOCK:(g + 1) * BLOCK])

    half_heads = GMLP_HEADS // 2
    for h in range(GMLP_HEADS):
        u_ref = u0_ref if h < half_heads else u1_ref
        v_ref = v0_ref if h < half_heads else v1_ref
        hh = h % half_heads
        u = _gelu(u_ref[0, :, hh * HEAD_DIM:(hh + 1) * HEAD_DIM].astype(F32))
        v = _gelu(v_ref[0, :, hh * HEAD_DIM:(hh + 1) * HEAD_DIM].astype(F32))
        mu = jnp.mean(v, axis=-1, keepdims=True)
        dv = v - mu
        var = jnp.mean(dv * dv, axis=-1, keepdims=True)
        vn = (dv * lax.rsqrt(var + EPS) * gn_ref[h:h + 1, :]).astype(BF16)
        w = ws_ref[h].astype(BF16)
        bias = bst_ref[:, h:h + 1]
        for cidx in range(nsub):
            sl = slice(cidx * BLOCK, (cidx + 1) * BLOCK)
            mixed = jnp.dot(w, vn[sl], preferred_element_type=F32) + bias
            yg_scr[sl, h * HEAD_DIM:(h + 1) * HEAD_DIM] = u[sl] * mixed

    o_ref[0, :, :aw] = _rms(ya_scr[...], gg_ref[:, :aw]).astype(BF16)
    o_ref[0, :, aw:] = _rms(yg_scr[...], gg_ref[:, aw:]).astype(BF16)


def _mixer(z, positions, sink, invf, gn, ws, bs, gg):
    bsz, s, _ = z.shape
    tq = min(256, s)
    nsub = tq // BLOCK
    nblk = s // BLOCK
    aw = ATTN_HEADS * HEAD_DIM
    gw = GMLP_HEADS * HEAD_DIM
    cw = 512
    pos3 = positions.reshape(bsz, s, 1)

    def prev_blk(b, i):
        return (b, jnp.maximum(i * nsub - 1, 0), 2)

    def next_blk(b, i):
        return (b, jnp.minimum((i + 1) * nsub, nblk - 1), 2)

    kernel = functools.partial(_mixer_kernel, seq_len=s)
    return pl.pallas_call(
        kernel,
        grid=(bsz, s // tq),
        in_specs=[pl.BlockSpec(memory_space=pltpu.SMEM),
                  pl.BlockSpec((1, tq, aw), lambda b, i: (b, i, 0)),
                  pl.BlockSpec((1, BLOCK, cw), prev_blk),
                  pl.BlockSpec((1, tq, cw), lambda b, i: (b, i, 2)),
                  pl.BlockSpec((1, BLOCK, cw), next_blk),
                  pl.BlockSpec((1, tq, cw), lambda b, i: (b, i, 3)),
                  pl.BlockSpec((1, tq, cw), lambda b, i: (b, i, 4)),
                  pl.BlockSpec((1, tq, cw), lambda b, i: (b, i, 5)),
                  pl.BlockSpec((1, tq, cw), lambda b, i: (b, i, 6)),
                  pl.BlockSpec((1, BLOCK, 1), lambda b, i: prev_blk(b, i)[:2] + (0,)),
                  pl.BlockSpec((1, tq, 1), lambda b, i: (b, i, 0)),
                  pl.BlockSpec((1, BLOCK, 1), lambda b, i: next_blk(b, i)[:2] + (0,)),
                  pl.BlockSpec((1, LANES), lambda b, i: (0, 0)),
                  pl.BlockSpec((GMLP_HEADS, HEAD_DIM), lambda b, i: (0, 0)),
                  pl.BlockSpec((GMLP_HEADS, BLOCK, BLOCK), lambda b, i: (0, 0, 0)),
                  pl.BlockSpec((BLOCK, GMLP_HEADS), lambda b, i: (0, 0)),
                  pl.BlockSpec((1, aw + gw), lambda b, i: (0, 0))],
        out_specs=pl.BlockSpec((1, tq, aw + gw), lambda b, i: (b, i, 0)),
        out_shape=jax.ShapeDtypeStruct((bsz, s, aw + gw), BF16),
        scratch_shapes=[pltpu.VMEM((tq, aw), F32), pltpu.VMEM((tq, gw), F32)],
        compiler_params=pltpu.CompilerParams(dimension_semantics=("arbitrary", "arbitrary"),
                                             vmem_limit_bytes=VMEM_LIMIT),
        name="mixer",
    )(sink, z, z, z, z, z, z, z, z, pos3, pos3, pos3, invf, gn, ws, bs.T, gg.reshape(1, aw + gw))


def _split_bf16(x):
    hi = x.astype(BF16)
    lo = (x - hi.astype(F32)).astype(BF16)
    return hi, lo


def _outproj_kernel(y_ref, w_ref, x_ref, mod_ref, gpost_ref, gpre_ref, wr_ref, br_ref,
                    h_ref, f_ref, eid_ref, ew_ref):
    mod = mod_ref[0]
    mix = jnp.dot(y_ref[0], w_ref[...], preferred_element_type=F32)
    h1 = x_ref[0] + mod[2:3] * _rms(mix, gpost_ref[...])
    h_ref[0] = h1
    f = _rms(h1, gpre_ref[...]) * (1.0 + mod[4:5]) + mod[3:4]
    f_ref[...] = f

    f_hi, f_lo = _split_bf16(f)
    w_hi, w_lo = _split_bf16(wr_ref[...])
    nt = (((1,), (1,)), ((), ()))
    logits = (lax.dot_general(w_hi, f_hi, nt, preferred_element_type=F32)
              + lax.dot_general(w_lo, f_hi, nt, preferred_element_type=F32)
              + lax.dot_general(w_hi, f_lo, nt, preferred_element_type=F32)) + br_ref[...]

    gl = [logits[g:g + 1] for g in range(N_GROUPS)]
    gmax = functools.reduce(jnp.maximum, gl)
    gidx = jnp.full(gmax.shape, N_GROUPS - 1, I32)
    for g in range(N_GROUPS - 2, -1, -1):
        gidx = jnp.where(gl[g] == gmax, g, gidx)
    gval = 1.0 / functools.reduce(lambda a, b: a + b, [jnp.exp(v - gmax) for v in gl])

    es = []
    for e in range(EXPERTS_PER_GROUP):
        r = N_GROUPS + (N_GROUPS - 1) * EXPERTS_PER_GROUP + e
        v = logits[r:r + 1]
        for g in range(N_GROUPS - 2, -1, -1):
            r = N_GROUPS + g * EXPERTS_PER_GROUP + e
            v = jnp.where(gidx == g, logits[r:r + 1], v)
        es.append(v)
    m1 = functools.reduce(jnp.maximum, es)
    i1 = jnp.full(m1.shape, EXPERTS_PER_GROUP - 1, I32)
    for e in range(EXPERTS_PER_GROUP - 2, -1, -1):
        i1 = jnp.where(es[e] == m1, e, i1)
    rest = [jnp.where(i1 == e, -jnp.inf, es[e]) for e in range(EXPERTS_PER_GROUP)]
    m2 = functools.reduce(jnp.maximum, rest)
    i2 = jnp.full(m2.shape, EXPERTS_PER_GROUP - 1, I32)
    for e in range(EXPERTS_PER_GROUP - 2, -1, -1):
        i2 = jnp.where(rest[e] == m2, e, i2)
    p2 = jnp.exp(m2 - m1)
    w1 = gval / (1.0 + p2)
    w2 = gval * p2 / (1.0 + p2)
    eid_ref[0:1, :] = gidx * EXPERTS_PER_GROUP + i1
    eid_ref[1:2, :] = gidx * EXPERTS_PER_GROUP + i2
    ew_ref[0:1, :] = w1
    ew_ref[1:2, :] = w2


def _outproj(ycat, w_out_bf16, x, mod, gpost, gpre, wr, br):
    bsz, s, d = x.shape
    t = bsz * s
    tm = min(256, s)
    nti = s // tm
    return pl.pallas_call(
        _outproj_kernel,
        grid=(bsz, nti),
        in_specs=[pl.BlockSpec((1, tm, d), lambda b, i: (b, i, 0)),
                  pl.BlockSpec((d, d), lambda b, i: (0, 0)),
                  pl.BlockSpec((1, tm, d), lambda b, i: (b, i, 0)),
                  pl.BlockSpec((1, N_MOD, d), lambda b, i: (b, 0, 0)),
                  pl.BlockSpec((1, d), lambda b, i: (0, 0)),
                  pl.BlockSpec((1, d), lambda b, i: (0, 0)),
                  pl.BlockSpec((ROUTER_ROWS, d), lambda b, i: (0, 0)),
                  pl.BlockSpec((ROUTER_ROWS, 1), lambda b, i: (0, 0))],
        out_specs=[pl.BlockSpec((1, tm, d), lambda b, i: (b, i, 0)),
                   pl.BlockSpec((tm, d), lambda b, i: (b * nti + i, 0)),
                   pl.BlockSpec((TOP_K, tm), lambda b, i: (0, b * nti + i)),
                   pl.BlockSpec((TOP_K, tm), lambda b, i: (0, b * nti + i))],
        out_shape=[jax.ShapeDtypeStruct((bsz, s, d), F32),
                   jax.ShapeDtypeStruct((t, d), F32),
                   jax.ShapeDtypeStruct((TOP_K, t), I32),
                   jax.ShapeDtypeStruct((TOP_K, t), F32)],
        compiler_params=pltpu.CompilerParams(dimension_semantics=("arbitrary", "arbitrary"),
                                             vmem_limit_bytes=VMEM_LIMIT),
        name="outproj",
    )(ycat, w_out_bf16, x, mod, gpost.reshape(1, d), gpre.reshape(1, d), wr, br)


def _slots_kernel(eid_ref, pos_ref, te_ref, pad_ref, rank_scr, *, n_slots):
    t = eid_ref.shape[1]
    chunk = min(512, t)
    nchunk = t // chunk
    ntp = te_ref.shape[1]
    tri = jnp.where(lax.broadcasted_iota(I32, (chunk, chunk), 0)
                    <= lax.broadcasted_iota(I32, (chunk, chunk), 1), 1.0, 0.0).astype(BF16)
    e_io = lax.broadcasted_iota(I32, (N_EXPERTS, chunk), 0)

    cnt = jnp.zeros((N_EXPERTS, 1), F32)
    for k in range(TOP_K):
        def rank_body(c, carry, k=k):
            off = pl.multiple_of(c * chunk, chunk)
            onehot = e_io == eid_ref[pl.ds(k, 1), pl.ds(off, chunk)]
            ones = jnp.where(onehot, 1.0, 0.0)
            prefix = jnp.dot(ones.astype(BF16), tri, preferred_element_type=F32) + carry
            rank = jnp.sum(jnp.where(onehot, prefix, 0.0), axis=0, keepdims=True) - 1.0
            rank_scr[pl.ds(k, 1), pl.ds(off, chunk)] = rank
            return carry + jnp.sum(ones, axis=1, keepdims=True)
        cnt = lax.fori_loop(0, nchunk, rank_body, cnt)

    padded = jnp.floor((cnt + (SLOT_TILE - 1)) * (1.0 / SLOT_TILE)) * SLOT_TILE
    sub = lax.broadcasted_iota(I32, (N_EXPERTS, LANES), 0)
    lan = lax.broadcasted_iota(I32, (N_EXPERTS, LANES), 1)
    padded_row = jnp.sum(jnp.where(sub == lan, padded, 0.0), axis=0, keepdims=True)
    start = jnp.sum(jnp.where(lan < sub, padded_row, 0.0), axis=1, keepdims=True)
    end = start + padded
    pad_lo = start + cnt

    for k in range(TOP_K):
        def pos_body(c, carry, k=k):
            off = pl.multiple_of(c * chunk, chunk)
            onehot = e_io == eid_ref[pl.ds(k, 1), pl.ds(off, chunk)]
            base = jnp.sum(jnp.where(onehot, start, 0.0), axis=0, keepdims=True)
            pos_ref[pl.ds(k, 1), pl.ds(off, chunk)] = (
                base + rank_scr[pl.ds(k, 1), pl.ds(off, chunk)]).astype(I32)
            return carry
        lax.fori_loop(0, nchunk, pos_body, 0)

    tile_start = lax.broadcasted_iota(I32, (N_EXPERTS, ntp), 1).astype(F32) * SLOT_TILE
    te_ref[...] = jnp.sum(jnp.where(end <= tile_start, 1.0, 0.0), axis=0, keepdims=True).astype(I32)
    lo_row = jnp.sum(jnp.where(sub == lan, pad_lo, 0.0), axis=0, keepdims=True)
    hi_row = jnp.sum(jnp.where(sub + PAD_HI_LANE == lan, end, 0.0), axis=0, keepdims=True)
    lane_row = lax.broadcasted_iota(I32, (1, LANES), 1)
    total = jnp.sum(padded_row, axis=1, keepdims=True)
    tail = (jnp.where(lane_row == N_EXPERTS, total, 0.0)
            + jnp.where(lane_row == PAD_HI_LANE + N_EXPERTS, float(n_slots), 0.0))
    pad_ref[...] = (lo_row + hi_row + tail).astype(I32)


def _slots(eid, n_tiles):
    t = eid.shape[1]
    ntp = -(-n_tiles // LANES) * LANES
    return pl.pallas_call(
        functools.partial(_slots_kernel, n_slots=n_tiles * SLOT_TILE),
        out_shape=[jax.ShapeDtypeStruct((TOP_K, t), I32),
                   jax.ShapeDtypeStruct((1, ntp), I32),
                   jax.ShapeDtypeStruct((1, LANES), I32)],
        scratch_shapes=[pltpu.VMEM((TOP_K, t), F32)],
        compiler_params=pltpu.CompilerParams(vmem_limit_bytes=VMEM_LIMIT),
        name="slots",
    )(eid)


def _scatter_kernel(pos_ref, pad_ref, f_ref, xs_ref, zero_scr, sem, zsem, *, n_tokens):
    ts = f_ref.shape[0]
    i = pl.program_id(0)

    def zero_copy(r):
        return pltpu.make_async_copy(zero_scr.at[pl.ds(0, 1)], xs_ref.at[pl.ds(r, 1)], zsem)

    @pl.when(i == 0)
    def _():
        zero_scr[...] = jnp.zeros(zero_scr.shape, zero_scr.dtype)
        for e in range(N_EXPERTS + 1):
            lo, hi = pad_ref[e], pad_ref[PAD_HI_LANE + e]

            def start(r, c):
                zero_copy(r).start()
                return c

            def wait(r, c):
                zero_copy(r).wait()
                return c
            lax.fori_loop(lo, hi, start, 0)
            lax.fori_loop(lo, hi, wait, 0)

    def row_copy(tok, k):
        dst = pos_ref[k * n_tokens + i * ts + tok]
        return pltpu.make_async_copy(f_ref.at[pl.ds(tok, 1)], xs_ref.at[pl.ds(dst, 1)], sem)

    def start(tok, c):
        for k in range(TOP_K):
            row_copy(tok, k).start()
        return c

    def wait(tok, c):
        for k in range(TOP_K):
            row_copy(tok, k).wait()
        return c
    lax.fori_loop(0, ts, start, 0, unroll=8)
    lax.fori_loop(0, ts, wait, 0, unroll=8)


def _scatter(pos_flat, pad, f, n_slots):
    t, d = f.shape
    ts = min(512, t)
    kernel = functools.partial(_scatter_kernel, n_tokens=t)
    return pl.pallas_call(
        kernel,
        grid_spec=pltpu.PrefetchScalarGridSpec(
            num_scalar_prefetch=2,
            grid=(t // ts,),
            in_specs=[pl.BlockSpec((ts, d), lambda i, pos, pad: (i, 0))],
            out_specs=pl.BlockSpec(memory_space=pl.ANY),
            scratch_shapes=[pltpu.VMEM((8, d), F32), pltpu.SemaphoreType.DMA, pltpu.SemaphoreType.DMA]),
        out_shape=jax.ShapeDtypeStruct((n_slots, d), F32),
        compiler_params=pltpu.CompilerParams(dimension_semantics=("arbitrary",),
                                             vmem_limit_bytes=VMEM_LIMIT),
        name="scatter",
    )(pos_flat, pad, f)


def _experts_kernel(te_ref, x_ref, wg_ref, wu_ref, wd_ref, o_ref):
    active = te_ref[pl.program_id(0)] < N_EXPERTS

    @pl.when(active)
    def _():
        x = x_ref[...].astype(BF16)
        hg = jnp.dot(x, wg_ref[0], preferred_element_type=F32)
        hu = jnp.dot(x, wu_ref[0], preferred_element_type=F32)
        hid = (_silu(hg) * hu).astype(BF16)
        o_ref[...] = jnp.dot(hid, wd_ref[0], preferred_element_type=F32)

    @pl.when(jnp.logical_not(active))
    def _():
        o_ref[...] = jnp.zeros(o_ref.shape, o_ref.dtype)


def _experts(te, xs, wg, wu, wd, n_tiles):
    n_slots, d = xs.shape
    fdim = wg.shape[2]

    def w_idx(i, te):
        return (jnp.minimum(te[i], N_EXPERTS - 1), 0, 0)

    return pl.pallas_call(
        _experts_kernel,
        grid_spec=pltpu.PrefetchScalarGridSpec(
            num_scalar_prefetch=1,
            grid=(n_tiles,),
            in_specs=[pl.BlockSpec((SLOT_TILE, d), lambda i, te: (i, 0)),
                      pl.BlockSpec((1, d, fdim), w_idx),
                      pl.BlockSpec((1, d, fdim), w_idx),
                      pl.BlockSpec((1, fdim, d), w_idx)],
            out_specs=pl.BlockSpec((SLOT_TILE, d), lambda i, te: (i, 0))),
        out_shape=jax.ShapeDtypeStruct((n_slots, d), F32),
        compiler_params=pltpu.CompilerParams(dimension_semantics=("arbitrary",),
                                             vmem_limit_bytes=VMEM_LIMIT),
        name="experts",
    )(te, xs, wg, wu, wd)


def _final_kernel(pos_ref, ys_ref, h_ref, ew_ref, mod_ref, g_ref, o_ref, buf, sem, *, n_tokens):
    tg = h_ref.shape[1]
    base = (pl.program_id(0) * pl.num_programs(1) + pl.program_id(1)) * tg

    def row_copy(tok, k):
        src = pos_ref[k * n_tokens + base + tok]
        return pltpu.make_async_copy(ys_ref.at[pl.ds(src, 1)], buf.at[k, pl.ds(tok, 1)], sem)

    def start(tok, c):
        for k in range(TOP_K):
            row_copy(tok, k).start()
        return c

    def wait(tok, c):
        for k in range(TOP_K):
            row_copy(tok, k).wait()
        return c
    lax.fori_loop(0, tg, start, 0, unroll=8)
    lax.fori_loop(0, tg, wait, 0, unroll=8)

    ew = ew_ref[...]
    y = ew[:, 0:1] * buf[0] + ew[:, 1:2] * buf[1]
    o_ref[0] = h_ref[0] + mod_ref[0][5:6] * _rms(y, g_ref[...])


def _final(pos_flat, ys, h1, ew_t, mod, g):
    bsz, s, d = h1.shape
    t = bsz * s
    tg = min(256, s)
    nti = s // tg
    kernel = functools.partial(_final_kernel, n_tokens=t)
    return pl.pallas_call(
        kernel,
        grid_spec=pltpu.PrefetchScalarGridSpec(
            num_scalar_prefetch=1,
            grid=(bsz, nti),
            in_specs=[pl.BlockSpec(memory_space=pl.ANY),
                      pl.BlockSpec((1, tg, d), lambda b, i, pos: (b, i, 0)),
                      pl.BlockSpec((tg, TOP_K), lambda b, i, pos: (b * nti + i, 0)),
                      pl.BlockSpec((1, N_MOD, d), lambda b, i, pos: (b, 0, 0)),
                      pl.BlockSpec((1, d), lambda b, i, pos: (0, 0))],
            out_specs=pl.BlockSpec((1, tg, d), lambda b, i, pos: (b, i, 0)),
            scratch_shapes=[pltpu.VMEM((TOP_K, tg, d), F32), pltpu.SemaphoreType.DMA]),
        out_shape=jax.ShapeDtypeStruct((bsz, s, d), F32),
        compiler_params=pltpu.CompilerParams(dimension_semantics=("arbitrary", "arbitrary"),
                                             vmem_limit_bytes=VMEM_LIMIT),
        name="final",
    )(pos_flat, ys, h1, ew_t, mod, g.reshape(1, d))


def _inv_freq_lanes():
    inv = ROPE_THETA ** (-jnp.arange(ROPE_HALF, dtype=F32) * 2.0 / ROPE_DIM)
    return jnp.tile(inv, LANES // ROPE_HALF).reshape(1, LANES)


def kernel(x, c, positions, w_ada, b_ada, g_mix_pre, g_mix_post, w_in, sink_logits, gmlp_norm_g,
           w_spatial, b_spatial, g_group_out, w_out, g_ffn_pre, g_ffn_post, w_router_group,
           b_router_group, w_router_expert, b_router_expert, w_gate, w_up, w_down):
    bsz, s, d = x.shape
    t = bsz * s
    depth = w_ada.shape[0]
    fdim = w_gate.shape[-1]
    n_tiles = (TOP_K * t) // SLOT_TILE + N_EXPERTS
    n_slots = n_tiles * SLOT_TILE
    invf = _inv_freq_lanes()
    h = x
    for l in range(depth):
        mod = _ada(c, w_ada[l], b_ada[l]).reshape(bsz, N_MOD, d)
        z = _inproj(h, mod, g_mix_pre[l], w_in[l])
        ycat = _mixer(z, positions, sink_logits[l], invf, gmlp_norm_g[l], w_spatial[l],
                      b_spatial[l], g_group_out[l])
        wr = jnp.concatenate([w_router_group[l], w_router_expert[l].reshape(d, N_EXPERTS)], axis=1)
        wr = jnp.pad(wr.T, ((0, ROUTER_ROWS - N_GROUPS - N_EXPERTS), (0, 0)))
        br = jnp.concatenate([b_router_group[l], b_router_expert[l].reshape(N_EXPERTS)])
        br = jnp.pad(br, (0, ROUTER_ROWS - N_GROUPS - N_EXPERTS)).reshape(ROUTER_ROWS, 1)
        h1, f, eid, ew = _outproj(ycat, w_out[l].astype(BF16), h, mod, g_mix_post[l], g_ffn_pre[l],
                                  wr, br)
        pos, te, pad = _slots(eid, n_tiles)
        pos_flat = pos.reshape(TOP_K * t)
        xs = _scatter(pos_flat, pad.reshape(LANES), f, n_slots)
        ys = _experts(te.reshape(-1), xs,
                      w_gate[l].reshape(N_EXPERTS, d, fdim).astype(BF16),
                      w_up[l].reshape(N_EXPERTS, d, fdim).astype(BF16),
                      w_down[l].reshape(N_EXPERTS, fdim, d).astype(BF16), n_tiles)
        h = _final(pos_flat, ys, h1, ew.T, mod, g_ffn_post[l])
    return h
```

```python
import functools

import jax
import jax.numpy as jnp
from jax import lax
from jax.experimental import pallas as pl
from jax.experimental.pallas import tpu as pltpu

F32 = jnp.float32
BF16 = jnp.bfloat16
I32 = jnp.int32

HEAD_DIM = 128
ATTN_HEADS = 8
KV_HEADS = 2
Q_PER_KV = ATTN_HEADS // KV_HEADS
GMLP_HEADS = 8
WINDOW = 128
BLOCK = 128
ROPE_THETA = 500000.0
ROPE_DIM = HEAD_DIM // 4
ROPE_HALF = ROPE_DIM // 2
N_GROUPS = 4
EXPERTS_PER_GROUP = 4
N_EXPERTS = N_GROUPS * EXPERTS_PER_GROUP
TOP_K = 2
N_MOD = 6
EPS = 1e-6
NEG_INF = -1e30
LANES = 128
BF16_SUBLANES = 16
PAD_HI_LANE = 32
ROUTER_ROWS = 32

SLOT_TILE = 256
SLOT_TILE_LOG2 = 8
VMEM_LIMIT = 52 * 1024 * 1024


def _rms(x, g):
    ms = jnp.mean(x * x, axis=-1, keepdims=True)
    return x * lax.rsqrt(ms + EPS) * g


def _silu(x):
    return x / (1.0 + jnp.exp(-x))


def _gelu(x):
    return 0.5 * x * (1.0 + lax.erf(x * 0.7071067811865476))


def _ada_kernel(c_ref, w_ref, b_ref, o_ref):
    ca = _silu(c_ref[...]).astype(BF16)
    o_ref[...] = jnp.dot(ca, w_ref[...].astype(BF16), preferred_element_type=F32) + b_ref[...]


def _ada(c, w, b):
    bsz, d = c.shape
    n = w.shape[1]
    tn = 1024
    return pl.pallas_call(
        _ada_kernel,
        grid=(n // tn,),
        in_specs=[pl.BlockSpec((bsz, d), lambda j: (0, 0)),
                  pl.BlockSpec((d, tn), lambda j: (0, j)),
                  pl.BlockSpec((1, tn), lambda j: (0, j))],
        out_specs=pl.BlockSpec((bsz, tn), lambda j: (0, j)),
        out_shape=jax.ShapeDtypeStruct((bsz, n), F32),
        compiler_params=pltpu.CompilerParams(dimension_semantics=("arbitrary",),
                                             vmem_limit_bytes=VMEM_LIMIT),
        name="ada",
    )(c, w, b.reshape(1, n))


def _inproj_kernel(x_ref, mod_ref, g_ref, w_ref, o_ref, a_a, a_b, *, rows_per_step):
    g = pl.program_id(0)
    j = pl.program_id(1)
    tm = a_a.shape[0]

    @pl.when((g == 0) & (j == 0))
    def _():
        a_b[...] = jnp.zeros(a_b.shape, a_b.dtype)

    def step(a_w, a_r):
        o_ref[0] = jnp.dot(a_r[...], w_ref[...], preferred_element_type=F32).astype(BF16)
        start = pl.multiple_of(jnp.minimum(j * rows_per_step, tm - rows_per_step), BF16_SUBLANES)
        mod = mod_ref[0]
        a = _rms(x_ref[0, pl.ds(start, rows_per_step), :], g_ref[...]) * (1.0 + mod[1:2]) + mod[0:1]
        a_w[pl.ds(start, rows_per_step), :] = a.astype(BF16)

    @pl.when(g % 2 == 0)
    def _():
        step(a_a, a_b)

    @pl.when(g % 2 == 1)
    def _():
        step(a_b, a_a)


def _inproj(x, mod, g, w_bf16):
    bsz, s, d = x.shape
    n = w_bf16.shape[1]
    tm = min(1024, s)
    tn = 512
    nj = n // tn
    nti = s // tm
    ntile = bsz * nti
    rows_per_step = -(-tm // (nj * BF16_SUBLANES)) * BF16_SUBLANES

    def norm_tile(g):
        gg = jnp.minimum(g, ntile - 1)
        return gg // nti, gg % nti

    def mm_tile(g):
        gg = jnp.maximum(g - 1, 0)
        return gg // nti, gg % nti

    return pl.pallas_call(
        functools.partial(_inproj_kernel, rows_per_step=rows_per_step),
        grid=(ntile + 1, nj),
        in_specs=[pl.BlockSpec((1, tm, d), lambda g, j: norm_tile(g) + (0,)),
                  pl.BlockSpec((1, N_MOD, d), lambda g, j: (norm_tile(g)[0], 0, 0)),
                  pl.BlockSpec((1, d), lambda g, j: (0, 0)),
                  pl.BlockSpec((d, tn), lambda g, j: (0, j))],
        out_specs=pl.BlockSpec((1, tm, tn), lambda g, j: mm_tile(g) + (jnp.where(g == 0, 0, j),)),
        out_shape=jax.ShapeDtypeStruct((bsz, s, n), BF16),
        scratch_shapes=[pltpu.VMEM((tm, d), BF16), pltpu.VMEM((tm, d), BF16)],
        compiler_params=pltpu.CompilerParams(dimension_semantics=("arbitrary", "arbitrary"),
                                             vmem_limit_bytes=VMEM_LIMIT),
        name="inproj",
    )(x, mod, g.reshape(1, d), w_bf16)


def _angles_kernel(pos_ref, invf_ref, cos_ref, sin_ref):
    ang = pos_ref[...].astype(F32) * invf_ref[...]
    cos_ref[...] = jnp.cos(ang)
    sin_ref[...] = jnp.sin(ang)


def _rope_tables(positions):
    bsz, s = positions.shape
    per_row = LANES // ROPE_HALF
    rows = bsz * s // per_row
    inv = ROPE_THETA ** (-jnp.arange(ROPE_HALF, dtype=F32) * 2.0 / ROPE_DIM)
    invf = jnp.tile(inv, per_row).reshape(1, LANES)
    pos_rep = jnp.repeat(positions.reshape(rows, per_row), ROPE_HALF, axis=1)
    cos, sin = pl.pallas_call(
        _angles_kernel,
        out_shape=[jax.ShapeDtypeStruct((rows, LANES), F32)] * 2,
        name="angles",
    )(pos_rep, invf)
    cos = cos.reshape(bsz, s, ROPE_HALF)
    sin = sin.reshape(bsz, s, ROPE_HALF)
    rest = HEAD_DIM - ROPE_DIM
    cos = jnp.concatenate([cos, cos, jnp.ones((bsz, s, rest), F32)], axis=-1)
    sin = jnp.concatenate([sin, sin, jnp.zeros((bsz, s, rest), F32)], axis=-1)
    return cos, sin


def _rope(x, cos, sin):
    lane = lax.broadcasted_iota(I32, x.shape, 1)
    partner = jnp.where(lane < ROPE_HALF,
                        pltpu.roll(x, HEAD_DIM - ROPE_HALF, 1),
                        pltpu.roll(x, ROPE_HALF, 1))
    return x * cos + partner * sin


def _mixer_kernel(sink_ref, q_ref, kvp_ref, kvc_ref, kvn_ref, u0_ref, u1_ref, v0_ref, v1_ref,
                  cosp_ref, cosc_ref, cosn_ref, sinp_ref, sinc_ref, sinn_ref,
                  gn_ref, ws_ref, bst_ref, gg_ref, o_ref, ya_scr, yg_scr):
    tq = q_ref.shape[1]
    nsub = tq // BLOCK
    aw = ATTN_HEADS * HEAD_DIM
    i = pl.program_id(1)
    sign = jnp.where(lax.broadcasted_iota(I32, (1, HEAD_DIM), 1) < ROPE_HALF, -1.0, 1.0)

    cos_c, sin_c = cosc_ref[0], sinc_ref[0] * sign
    tabs = ((cosp_ref[0], sinp_ref[0] * sign), (cos_c, sin_c), (cosn_ref[0], sinn_ref[0] * sign))
    kbands, vbands = [], []
    for h in range(KV_HEADS):
        kparts, vparts = [], []
        for ref, (cs, sn) in zip((kvp_ref, kvc_ref, kvn_ref), tabs):
            k = ref[0, :, h * HEAD_DIM:(h + 1) * HEAD_DIM].astype(F32)
            kparts.append(_rope(k, cs, sn).astype(BF16))
            vparts.append(ref[0, :, (KV_HEADS + h) * HEAD_DIM:(KV_HEADS + h + 1) * HEAD_DIM])
        kbands.append(jnp.concatenate(kparts, axis=0))
        vbands.append(jnp.concatenate(vparts, axis=0))

    scale = HEAD_DIM ** -0.5
    cos_q, sin_q = cos_c * scale, sin_c * scale
    qh = [_rope(q_ref[0, :, h * HEAD_DIM:(h + 1) * HEAD_DIM].astype(F32), cos_q, sin_q).astype(BF16)
          for h in range(ATTN_HEADS)]

    rows = Q_PER_KV * BLOCK
    band = 3 * BLOCK
    q_off = lax.broadcasted_iota(I32, (rows, BLOCK), 0) & (BLOCK - 1)
    k_off = lax.broadcasted_iota(I32, (rows, BLOCK), 1)
    bias_prev = jnp.where(k_off >= q_off, 0.0, NEG_INF)
    bias_next = jnp.where(k_off <= q_off, 0.0, NEG_INF)
    first_tile = i == 0
    last_tile = i == pl.num_programs(1) - 1
    for s in range(nsub):
        bp = jnp.where(first_tile, NEG_INF, bias_prev) if s == 0 else bias_prev
        bn = jnp.where(last_tile, NEG_INF, bias_next) if s == nsub - 1 else bias_next
        for h in range(KV_HEADS):
            q4 = jnp.concatenate([qh[h * Q_PER_KV + g][s * BLOCK:(s + 1) * BLOCK]
                                  for g in range(Q_PER_KV)], axis=0)
            kb = kbands[h][s * BLOCK:s * BLOCK + band]
            vb = vbands[h][s * BLOCK:s * BLOCK + band]
            sc = lax.dot_general(q4, kb, (((1,), (1,)), ((), ())), preferred_element_type=F32)
            sc = jnp.concatenate([sc[:, :BLOCK] + bp, sc[:, BLOCK:2 * BLOCK], sc[:, 2 * BLOCK:] + bn],
                                 axis=1)
            sink = jnp.concatenate([jnp.full((BLOCK, 1), sink_ref[h * Q_PER_KV + g], F32)
                                    for g in range(Q_PER_KV)], axis=0)
            m = jnp.maximum(jnp.max(sc, axis=-1, keepdims=True), sink)
            p = jnp.exp(sc - m)
            den = jnp.sum(p, axis=-1, keepdims=True) + jnp.exp(sink - m)
            o = jnp.dot(p.astype(BF16), vb, preferred_element_type=F32) / den
            for g in range(Q_PER_KV):
                hq = h * Q_PER_KV + g
                ya_scr[s * BLOCK:(s + 1) * BLOCK, hq * HEAD_DIM:(hq + 1) * HEAD_DIM] = (
                    o[g * BLOCK:(g + 1) * BLOCK])

    half_heads = GMLP_HEADS // 2
    for h in range(GMLP_HEADS):
        u_ref = u0_ref if h < half_heads else u1_ref
        v_ref = v0_ref if h < half_heads else v1_ref
        hh = h % half_heads
        u = _gelu(u_ref[0, :, hh * HEAD_DIM:(hh + 1) * HEAD_DIM].astype(F32))
        v = _gelu(v_ref[0, :, hh * HEAD_DIM:(hh + 1) * HEAD_DIM].astype(F32))
        mu = jnp.mean(v, axis=-1, keepdims=True)
        dv = v - mu
        var = jnp.mean(dv * dv, axis=-1, keepdims=True)
        vn = (dv * lax.rsqrt(var + EPS) * gn_ref[h:h + 1, :]).astype(BF16)
        w = ws_ref[h].astype(BF16)
        bias = bst_ref[:, h:h + 1]
        for cidx in range(nsub):
            sl = slice(cidx * BLOCK, (cidx + 1) * BLOCK)
            mixed = jnp.dot(w, vn[sl], preferred_element_type=F32) + bias
            yg_scr[sl, h * HEAD_DIM:(h + 1) * HEAD_DIM] = u[sl] * mixed

    o_ref[0, :, :aw] = _rms(ya_scr[...], gg_ref[:, :aw]).astype(BF16)
    o_ref[0, :, aw:] = _rms(yg_scr[...], gg_ref[:, aw:]).astype(BF16)


def _mixer(z, cos, sin, sink, gn, ws, bs, gg):
    assert WINDOW == BLOCK
    bsz, s, _ = z.shape
    tq = min(256, s)
    nsub = tq // BLOCK
    nblk = s // BLOCK
    aw = ATTN_HEADS * HEAD_DIM
    gw = GMLP_HEADS * HEAD_DIM
    cw = 512

    def prev_blk(b, i):
        return (b, jnp.maximum(i * nsub - 1, 0), 2)

    def next_blk(b, i):
        return (b, jnp.minimum((i + 1) * nsub, nblk - 1), 2)

    tab_specs = [pl.BlockSpec((1, BLOCK, HEAD_DIM), lambda b, i: prev_blk(b, i)[:2] + (0,)),
                 pl.BlockSpec((1, tq, HEAD_DIM), lambda b, i: (b, i, 0)),
                 pl.BlockSpec((1, BLOCK, HEAD_DIM), lambda b, i: next_blk(b, i)[:2] + (0,))]
    return pl.pallas_call(
        _mixer_kernel,
        grid=(bsz, s // tq),
        in_specs=[pl.BlockSpec(memory_space=pltpu.SMEM),
                  pl.BlockSpec((1, tq, aw), lambda b, i: (b, i, 0)),
                  pl.BlockSpec((1, BLOCK, cw), prev_blk),
                  pl.BlockSpec((1, tq, cw), lambda b, i: (b, i, 2)),
                  pl.BlockSpec((1, BLOCK, cw), next_blk),
                  pl.BlockSpec((1, tq, cw), lambda b, i: (b, i, 3)),
                  pl.BlockSpec((1, tq, cw), lambda b, i: (b, i, 4)),
                  pl.BlockSpec((1, tq, cw), lambda b, i: (b, i, 5)),
                  pl.BlockSpec((1, tq, cw), lambda b, i: (b, i, 6)),
                  *tab_specs, *tab_specs,
                  pl.BlockSpec((GMLP_HEADS, HEAD_DIM), lambda b, i: (0, 0)),
                  pl.BlockSpec((GMLP_HEADS, BLOCK, BLOCK), lambda b, i: (0, 0, 0)),
                  pl.BlockSpec((BLOCK, GMLP_HEADS), lambda b, i: (0, 0)),
                  pl.BlockSpec((1, aw + gw), lambda b, i: (0, 0))],
        out_specs=pl.BlockSpec((1, tq, aw + gw), lambda b, i: (b, i, 0)),
        out_shape=jax.ShapeDtypeStruct((bsz, s, aw + gw), BF16),
        scratch_shapes=[pltpu.VMEM((tq, aw), F32), pltpu.VMEM((tq, gw), F32)],
        compiler_params=pltpu.CompilerParams(dimension_semantics=("arbitrary", "arbitrary"),
                                             vmem_limit_bytes=VMEM_LIMIT),
        name="mixer",
    )(sink, z, z, z, z, z, z, z, z, cos, cos, cos, sin, sin, sin, gn, ws, bs.T,
      gg.reshape(1, aw + gw))


def _split_bf16(x):
    hi = x.astype(BF16)
    lo = (x - hi.astype(F32)).astype(BF16)
    return hi, lo


def _outproj_kernel(y_ref, w_ref, x_ref, mod_ref, gpost_ref, gpre_ref, wr_ref, br_ref,
                    h_ref, f_ref, eid_ref, ew_ref, mix_a, mix_b):
    g = pl.program_id(0)

    @pl.when(g == 0)
    def _():
        mix_b[...] = jnp.zeros(mix_b.shape, mix_b.dtype)

    def step(mix_w, mix_r):
        mix_w[...] = jnp.dot(y_ref[0], w_ref[...], preferred_element_type=F32)
        _outproj_finish(mix_r[...], x_ref, mod_ref, gpost_ref, gpre_ref, wr_ref, br_ref,
                        h_ref, f_ref, eid_ref, ew_ref)

    @pl.when(g % 2 == 0)
    def _():
        step(mix_a, mix_b)

    @pl.when(g % 2 == 1)
    def _():
        step(mix_b, mix_a)


def _outproj_finish(mix, x_ref, mod_ref, gpost_ref, gpre_ref, wr_ref, br_ref,
                    h_ref, f_ref, eid_ref, ew_ref):
    mod = mod_ref[0]
    h1 = x_ref[0] + mod[2:3] * _rms(mix, gpost_ref[...])
    h_ref[0] = h1
    f = _rms(h1, gpre_ref[...]) * (1.0 + mod[4:5]) + mod[3:4]
    f_ref[...] = f

    f_hi, f_lo = _split_bf16(f)
    w_hi, w_lo = _split_bf16(wr_ref[...])
    nt = (((1,), (1,)), ((), ()))
    logits = (lax.dot_general(w_hi, f_hi, nt, preferred_element_type=F32)
              + lax.dot_general(w_lo, f_hi, nt, preferred_element_type=F32)
              + lax.dot_general(w_hi, f_lo, nt, preferred_element_type=F32)) + br_ref[...]

    gl = [logits[g:g + 1] for g in range(N_GROUPS)]
    gmax = functools.reduce(jnp.maximum, gl)
    gidx = jnp.full(gmax.shape, N_GROUPS - 1, I32)
    for g in range(N_GROUPS - 2, -1, -1):
        gidx = jnp.where(gl[g] == gmax, g, gidx)
    gval = 1.0 / functools.reduce(lambda a, b: a + b, [jnp.exp(v - gmax) for v in gl])

    es = []
    for e in range(EXPERTS_PER_GROUP):
        r = N_GROUPS + (N_GROUPS - 1) * EXPERTS_PER_GROUP + e
        v = logits[r:r + 1]
        for g in range(N_GROUPS - 2, -1, -1):
            r = N_GROUPS + g * EXPERTS_PER_GROUP + e
            v = jnp.where(gidx == g, logits[r:r + 1], v)
        es.append(v)
    m1 = functools.reduce(jnp.maximum, es)
    i1 = jnp.full(m1.shape, EXPERTS_PER_GROUP - 1, I32)
    for e in range(EXPERTS_PER_GROUP - 2, -1, -1):
        i1 = jnp.where(es[e] == m1, e, i1)
    rest = [jnp.where(i1 == e, -jnp.inf, es[e]) for e in range(EXPERTS_PER_GROUP)]
    m2 = functools.reduce(jnp.maximum, rest)
    i2 = jnp.full(m2.shape, EXPERTS_PER_GROUP - 1, I32)
    for e in range(EXPERTS_PER_GROUP - 2, -1, -1):
        i2 = jnp.where(rest[e] == m2, e, i2)
    p2 = jnp.exp(m2 - m1)
    w1 = gval / (1.0 + p2)
    w2 = gval * p2 / (1.0 + p2)
    eid_ref[0:1, :] = gidx * EXPERTS_PER_GROUP + i1
    eid_ref[1:2, :] = gidx * EXPERTS_PER_GROUP + i2
    ew_ref[0:1, :] = w1
    ew_ref[1:2, :] = w2


def _outproj(ycat, w_out_bf16, x, mod, gpost, gpre, wr, br):
    bsz, s, d = x.shape
    t = bsz * s
    tm = min(256, s)
    nti = s // tm
    n = bsz * nti

    def proj(g):
        gg = jnp.minimum(g, n - 1)
        return gg // nti, gg % nti

    def fin(g):
        gg = jnp.maximum(g - 1, 0)
        return gg // nti, gg % nti

    return pl.pallas_call(
        _outproj_kernel,
        grid=(n + 1,),
        in_specs=[pl.BlockSpec((1, tm, d), lambda g: proj(g) + (0,)),
                  pl.BlockSpec((d, d), lambda g: (0, 0)),
                  pl.BlockSpec((1, tm, d), lambda g: fin(g) + (0,)),
                  pl.BlockSpec((1, N_MOD, d), lambda g: (fin(g)[0], 0, 0)),
                  pl.BlockSpec((1, d), lambda g: (0, 0)),
                  pl.BlockSpec((1, d), lambda g: (0, 0)),
                  pl.BlockSpec((ROUTER_ROWS, d), lambda g: (0, 0)),
                  pl.BlockSpec((ROUTER_ROWS, 1), lambda g: (0, 0))],
        out_specs=[pl.BlockSpec((1, tm, d), lambda g: fin(g) + (0,)),
                   pl.BlockSpec((tm, d), lambda g: (jnp.maximum(g - 1, 0), 0)),
                   pl.BlockSpec((TOP_K, tm), lambda g: (0, jnp.maximum(g - 1, 0))),
                   pl.BlockSpec((TOP_K, tm), lambda g: (0, jnp.maximum(g - 1, 0)))],
        out_shape=[jax.ShapeDtypeStruct((bsz, s, d), F32),
                   jax.ShapeDtypeStruct((t, d), F32),
                   jax.ShapeDtypeStruct((TOP_K, t), I32),
                   jax.ShapeDtypeStruct((TOP_K, t), F32)],
        scratch_shapes=[pltpu.VMEM((tm, d), F32), pltpu.VMEM((tm, d), F32)],
        compiler_params=pltpu.CompilerParams(dimension_semantics=("arbitrary",),
                                             vmem_limit_bytes=VMEM_LIMIT),
        name="outproj",
    )(ycat, w_out_bf16, x, mod, gpost.reshape(1, d), gpre.reshape(1, d), wr, br)


def _slots_kernel(eid_ref, pos_ref, te_ref, pad_ref, rank_scr, *, n_slots):
    t = eid_ref.shape[1]
    chunk = min(512, t)
    nchunk = t // chunk
    ntp = te_ref.shape[1]
    tri = jnp.where(lax.broadcasted_iota(I32, (chunk, chunk), 0)
                    <= lax.broadcasted_iota(I32, (chunk, chunk), 1), 1.0, 0.0).astype(BF16)
    e_io = lax.broadcasted_iota(I32, (N_EXPERTS, chunk), 0)

    cnt = jnp.zeros((N_EXPERTS, 1), F32)
    for k in range(TOP_K):
        def rank_body(c, carry, k=k):
            off = pl.multiple_of(c * chunk, chunk)
            onehot = e_io == eid_ref[pl.ds(k, 1), pl.ds(off, chunk)]
            ones = jnp.where(onehot, 1.0, 0.0)
            prefix = jnp.dot(ones.astype(BF16), tri, preferred_element_type=F32) + carry
            rank = jnp.sum(jnp.where(onehot, prefix, 0.0), axis=0, keepdims=True) - 1.0
            rank_scr[pl.ds(k, 1), pl.ds(off, chunk)] = rank
            return carry + jnp.sum(ones, axis=1, keepdims=True)
        cnt = lax.fori_loop(0, nchunk, rank_body, cnt)

    padded = jnp.floor((cnt + (SLOT_TILE - 1)) * (1.0 / SLOT_TILE)) * SLOT_TILE
    sub = lax.broadcasted_iota(I32, (N_EXPERTS, LANES), 0)
    lan = lax.broadcasted_iota(I32, (N_EXPERTS, LANES), 1)
    padded_row = jnp.sum(jnp.where(sub == lan, padded, 0.0), axis=0, keepdims=True)
    start = jnp.sum(jnp.where(lan < sub, padded_row, 0.0), axis=1, keepdims=True)
    end = start + padded
    pad_lo = start + cnt

    for k in range(TOP_K):
        def pos_body(c, carry, k=k):
            off = pl.multiple_of(c * chunk, chunk)
            onehot = e_io == eid_ref[pl.ds(k, 1), pl.ds(off, chunk)]
            base = jnp.sum(jnp.where(onehot, start, 0.0), axis=0, keepdims=True)
            pos_ref[pl.ds(k, 1), pl.ds(off, chunk)] = (
                base + rank_scr[pl.ds(k, 1), pl.ds(off, chunk)]).astype(I32)
            return carry
        lax.fori_loop(0, nchunk, pos_body, 0)

    tile_start = lax.broadcasted_iota(I32, (N_EXPERTS, ntp), 1).astype(F32) * SLOT_TILE
    te_ref[...] = jnp.sum(jnp.where(end <= tile_start, 1.0, 0.0), axis=0, keepdims=True).astype(I32)
    lo_row = jnp.sum(jnp.where(sub == lan, pad_lo, 0.0), axis=0, keepdims=True)
    hi_row = jnp.sum(jnp.where(sub + PAD_HI_LANE == lan, end, 0.0), axis=0, keepdims=True)
    lane_row = lax.broadcasted_iota(I32, (1, LANES), 1)
    total = jnp.sum(padded_row, axis=1, keepdims=True)
    tail = (jnp.where(lane_row == N_EXPERTS, total, 0.0)
            + jnp.where(lane_row == PAD_HI_LANE + N_EXPERTS, float(n_slots), 0.0))
    pad_ref[...] = (lo_row + hi_row + tail).astype(I32)


def _slots(eid, n_tiles):
    t = eid.shape[1]
    ntp = -(-n_tiles // LANES) * LANES
    return pl.pallas_call(
        functools.partial(_slots_kernel, n_slots=n_tiles * SLOT_TILE),
        out_shape=[jax.ShapeDtypeStruct((TOP_K, t), I32),
                   jax.ShapeDtypeStruct((1, ntp), I32),
                   jax.ShapeDtypeStruct((1, LANES), I32)],
        scratch_shapes=[pltpu.VMEM((TOP_K, t), F32)],
        compiler_params=pltpu.CompilerParams(vmem_limit_bytes=VMEM_LIMIT),
        name="slots",
    )(eid)


def _scatter_kernel(pos_ref, pad_ref, f_ref, xs_ref, zero_scr, sem, zsem, *, n_tokens):
    ts = f_ref.shape[0]
    i = pl.program_id(0)

    def zero_copy(r):
        return pltpu.make_async_copy(zero_scr.at[pl.ds(0, 1)], xs_ref.at[pl.ds(r, 1)], zsem)

    @pl.when(i == 0)
    def _():
        zero_scr[...] = jnp.zeros(zero_scr.shape, zero_scr.dtype)
        for e in range(N_EXPERTS + 1):
            lo, hi = pad_ref[e], pad_ref[PAD_HI_LANE + e]

            def start(r, c):
                zero_copy(r).start()
                return c

            def wait(r, c):
                zero_copy(r).wait()
                return c
            lax.fori_loop(lo, hi, start, 0)
            lax.fori_loop(lo, hi, wait, 0)

    def row_copy(tok, k):
        dst = pos_ref[k * n_tokens + i * ts + tok]
        return pltpu.make_async_copy(f_ref.at[pl.ds(tok, 1)], xs_ref.at[pl.ds(dst, 1)], sem)

    def start(tok, c):
        for k in range(TOP_K):
            row_copy(tok, k).start()
        return c

    def wait(tok, c):
        for k in range(TOP_K):
            row_copy(tok, k).wait()
        return c
    lax.fori_loop(0, ts, start, 0, unroll=8)
    lax.fori_loop(0, ts, wait, 0, unroll=8)


def _scatter(pos_flat, pad, f, n_slots):
    t, d = f.shape
    ts = min(512, t)
    kernel = functools.partial(_scatter_kernel, n_tokens=t)
    return pl.pallas_call(
        kernel,
        grid_spec=pltpu.PrefetchScalarGridSpec(
            num_scalar_prefetch=2,
            grid=(t // ts,),
            in_specs=[pl.BlockSpec((ts, d), lambda i, pos, pad: (i, 0))],
            out_specs=pl.BlockSpec(memory_space=pl.ANY),
            scratch_shapes=[pltpu.VMEM((8, d), F32), pltpu.SemaphoreType.DMA, pltpu.SemaphoreType.DMA]),
        out_shape=jax.ShapeDtypeStruct((n_slots, d), F32),
        compiler_params=pltpu.CompilerParams(dimension_semantics=("arbitrary",),
                                             vmem_limit_bytes=VMEM_LIMIT),
        name="scatter",
    )(pos_flat, pad, f)


def _experts_kernel(te_ref, x_ref, wg_ref, wu_ref, wd_ref, o_ref):
    active = te_ref[pl.program_id(0)] < N_EXPERTS

    @pl.when(active)
    def _():
        x = x_ref[...].astype(BF16)
        hg = jnp.dot(x, wg_ref[0], preferred_element_type=F32)
        hu = jnp.dot(x, wu_ref[0], preferred_element_type=F32)
        hid = (_silu(hg) * hu).astype(BF16)
        o_ref[...] = jnp.dot(hid, wd_ref[0], preferred_element_type=F32)

    @pl.when(jnp.logical_not(active))
    def _():
        o_ref[...] = jnp.zeros(o_ref.shape, o_ref.dtype)


def _experts(te, xs, wg, wu, wd, n_tiles):
    n_slots, d = xs.shape
    fdim = wg.shape[2]

    def w_idx(i, te):
        return (jnp.minimum(te[i], N_EXPERTS - 1), 0, 0)

    return pl.pallas_call(
        _experts_kernel,
        grid_spec=pltpu.PrefetchScalarGridSpec(
            num_scalar_prefetch=1,
            grid=(n_tiles,),
            in_specs=[pl.BlockSpec((SLOT_TILE, d), lambda i, te: (i, 0)),
                      pl.BlockSpec((1, d, fdim), w_idx),
                      pl.BlockSpec((1, d, fdim), w_idx),
                      pl.BlockSpec((1, fdim, d), w_idx)],
            out_specs=pl.BlockSpec((SLOT_TILE, d), lambda i, te: (i, 0))),
        out_shape=jax.ShapeDtypeStruct((n_slots, d), F32),
        compiler_params=pltpu.CompilerParams(dimension_semantics=("arbitrary",),
                                             vmem_limit_bytes=VMEM_LIMIT),
        name="experts",
    )(te, xs, wg, wu, wd)


def _final_kernel(pos_ref, ys_ref, h_ref, ew_ref, mod_ref, g_ref, o_ref, buf, sem, *, n_tokens):
    tg = h_ref.shape[1]
    step = pl.program_id(0) * pl.num_programs(1) + pl.program_id(1)
    n_steps = pl.num_programs(0) * pl.num_programs(1)

    def row_copy(tile, tok, k):
        slot = tile % 2
        src = pos_ref[k * n_tokens + tile * tg + tok]
        return pltpu.make_async_copy(ys_ref.at[pl.ds(src, 1)], buf.at[slot, k, pl.ds(tok, 1)],
                                     sem.at[slot])

    def gather(tile):
        def start(tok, c):
            for k in range(TOP_K):
                row_copy(tile, tok, k).start()
            return c
        lax.fori_loop(0, tg, start, 0, unroll=8)

    @pl.when(step == 0)
    def _():
        gather(0)

    @pl.when(step + 1 < n_steps)
    def _():
        gather(step + 1)

    def wait(tok, c):
        for k in range(TOP_K):
            row_copy(step, tok, k).wait()
        return c
    lax.fori_loop(0, tg, wait, 0, unroll=8)

    ew = ew_ref[...]
    slot = step % 2
    y = ew[:, 0:1] * buf[slot, 0] + ew[:, 1:2] * buf[slot, 1]
    o_ref[0] = h_ref[0] + mod_ref[0][5:6] * _rms(y, g_ref[...])


def _final(pos_flat, ys, h1, ew_t, mod, g):
    bsz, s, d = h1.shape
    t = bsz * s
    tg = min(256, s)
    nti = s // tg
    kernel = functools.partial(_final_kernel, n_tokens=t)
    return pl.pallas_call(
        kernel,
        grid_spec=pltpu.PrefetchScalarGridSpec(
            num_scalar_prefetch=1,
            grid=(bsz, nti),
            in_specs=[pl.BlockSpec(memory_space=pl.ANY),
                      pl.BlockSpec((1, tg, d), lambda b, i, pos: (b, i, 0)),
                      pl.BlockSpec((tg, TOP_K), lambda b, i, pos: (b * nti + i, 0)),
                      pl.BlockSpec((1, N_MOD, d), lambda b, i, pos: (b, 0, 0)),
                      pl.BlockSpec((1, d), lambda b, i, pos: (0, 0))],
            out_specs=pl.BlockSpec((1, tg, d), lambda b, i, pos: (b, i, 0)),
            scratch_shapes=[pltpu.VMEM((2, TOP_K, tg, d), F32), pltpu.SemaphoreType.DMA((2,))]),
        out_shape=jax.ShapeDtypeStruct((bsz, s, d), F32),
        compiler_params=pltpu.CompilerParams(dimension_semantics=("arbitrary", "arbitrary"),
                                             vmem_limit_bytes=VMEM_LIMIT),
        name="final",
    )(pos_flat, ys, h1, ew_t, mod, g.reshape(1, d))


def kernel(x, c, positions, w_ada, b_ada, g_mix_pre, g_mix_post, w_in, sink_logits, gmlp_norm_g,
           w_spatial, b_spatial, g_group_out, w_out, g_ffn_pre, g_ffn_post, w_router_group,
           b_router_group, w_router_expert, b_router_expert, w_gate, w_up, w_down):
    bsz, s, d = x.shape
    t = bsz * s
    depth = w_ada.shape[0]
    fdim = w_gate.shape[-1]
    n_tiles = (TOP_K * t) // SLOT_TILE + N_EXPERTS
    n_slots = n_tiles * SLOT_TILE
    cos, sin = _rope_tables(positions)
    h = x
    for l in range(depth):
        mod = _ada(c, w_ada[l], b_ada[l]).reshape(bsz, N_MOD, d)
        z = _inproj(h, mod, g_mix_pre[l], w_in[l].astype(BF16))
        ycat = _mixer(z, cos, sin, sink_logits[l], gmlp_norm_g[l], w_spatial[l],
                      b_spatial[l], g_group_out[l])
        wr = jnp.concatenate([w_router_group[l], w_router_expert[l].reshape(d, N_EXPERTS)], axis=1)
        wr = jnp.pad(wr.T, ((0, ROUTER_ROWS - N_GROUPS - N_EXPERTS), (0, 0)))
        br = jnp.concatenate([b_router_group[l], b_router_expert[l].reshape(N_EXPERTS)])
        br = jnp.pad(br, (0, ROUTER_ROWS - N_GROUPS - N_EXPERTS)).reshape(ROUTER_ROWS, 1)
        h1, f, eid, ew = _outproj(ycat, w_out[l].astype(BF16), h, mod, g_mix_post[l], g_ffn_pre[l],
                                  wr, br)
        pos, te, pad = _slots(eid, n_tiles)
        pos_flat = pos.reshape(TOP_K * t)
        xs = _scatter(pos_flat, pad.reshape(LANES), f, n_slots)
        ys = _experts(te.reshape(-1), xs,
                      w_gate[l].reshape(N_EXPERTS, d, fdim).astype(BF16),
                      w_up[l].reshape(N_EXPERTS, d, fdim).astype(BF16),
                      w_down[l].reshape(N_EXPERTS, fdim, d).astype(BF16), n_tiles)
        h = _final(pos_flat, ys, h1, ew.T, mod, g_ffn_post[l])
    return h
```

```python
import functools

import jax
import jax.numpy as jnp
from jax import lax
from jax.experimental import pallas as pl
from jax.experimental.pallas import tpu as pltpu

F32 = jnp.float32
BF16 = jnp.bfloat16
I32 = jnp.int32

HEAD_DIM = 128
ATTN_HEADS = 8
KV_HEADS = 2
Q_PER_KV = ATTN_HEADS // KV_HEADS
GMLP_HEADS = 8
WINDOW = 128
BLOCK = 128
ROPE_THETA = 500000.0
ROPE_DIM = HEAD_DIM // 4
ROPE_HALF = ROPE_DIM // 2
N_GROUPS = 4
EXPERTS_PER_GROUP = 4
N_EXPERTS = N_GROUPS * EXPERTS_PER_GROUP
TOP_K = 2
N_MOD = 6
EPS = 1e-6
NEG_INF = -1e30
LANES = 128
BF16_SUBLANES = 16
PAD_HI_LANE = 32
TILE_START_LANE = 64
N_WCHUNK = 4
EXPERTS_VMEM_LIMIT = 58 * 1024 * 1024
ROUTER_ROWS = 32

SLOT_TILE = 256
SLOT_TILE_LOG2 = 8
VMEM_LIMIT = 52 * 1024 * 1024


def _rms(x, g):
    ms = jnp.mean(x * x, axis=-1, keepdims=True)
    return x * lax.rsqrt(ms + EPS) * g


def _silu(x):
    return x / (1.0 + jnp.exp(-x))


def _gelu(x):
    return 0.5 * x * (1.0 + lax.erf(x * 0.7071067811865476))


def _ada_kernel(c_ref, w_ref, b_ref, o_ref):
    ca = _silu(c_ref[...]).astype(BF16)
    o_ref[...] = jnp.dot(ca, w_ref[...].astype(BF16), preferred_element_type=F32) + b_ref[...]


def _ada(c, w, b):
    bsz, d = c.shape
    n = w.shape[1]
    tn = 1024
    return pl.pallas_call(
        _ada_kernel,
        grid=(n // tn,),
        in_specs=[pl.BlockSpec((bsz, d), lambda j: (0, 0)),
                  pl.BlockSpec((d, tn), lambda j: (0, j)),
                  pl.BlockSpec((1, tn), lambda j: (0, j))],
        out_specs=pl.BlockSpec((bsz, tn), lambda j: (0, j)),
        out_shape=jax.ShapeDtypeStruct((bsz, n), F32),
        compiler_params=pltpu.CompilerParams(dimension_semantics=("arbitrary",),
                                             vmem_limit_bytes=VMEM_LIMIT),
        name="ada",
    )(c, w, b.reshape(1, n))


def _inproj_kernel(x_ref, mod_ref, g_ref, w_ref, o_ref, a_a, a_b, *, rows_per_step):
    g = pl.program_id(0)
    j = pl.program_id(1)
    tm = a_a.shape[0]

    @pl.when((g == 0) & (j == 0))
    def _():
        a_b[...] = jnp.zeros(a_b.shape, a_b.dtype)

    def step(a_w, a_r):
        o_ref[0] = jnp.dot(a_r[...], w_ref[...], preferred_element_type=F32).astype(BF16)
        start = pl.multiple_of(jnp.minimum(j * rows_per_step, tm - rows_per_step), BF16_SUBLANES)
        mod = mod_ref[0]
        a = _rms(x_ref[0, pl.ds(start, rows_per_step), :], g_ref[...]) * (1.0 + mod[1:2]) + mod[0:1]
        a_w[pl.ds(start, rows_per_step), :] = a.astype(BF16)

    @pl.when(g % 2 == 0)
    def _():
        step(a_a, a_b)

    @pl.when(g % 2 == 1)
    def _():
        step(a_b, a_a)


def _inproj(x, mod, g, w_bf16):
    bsz, s, d = x.shape
    n = w_bf16.shape[1]
    tm = min(1024, s)
    tn = 512
    nj = n // tn
    nti = s // tm
    ntile = bsz * nti
    rows_per_step = -(-tm // (nj * BF16_SUBLANES)) * BF16_SUBLANES

    def norm_tile(g):
        gg = jnp.minimum(g, ntile - 1)
        return gg // nti, gg % nti

    def mm_tile(g):
        gg = jnp.maximum(g - 1, 0)
        return gg // nti, gg % nti

    return pl.pallas_call(
        functools.partial(_inproj_kernel, rows_per_step=rows_per_step),
        grid=(ntile + 1, nj),
        in_specs=[pl.BlockSpec((1, tm, d), lambda g, j: norm_tile(g) + (0,)),
                  pl.BlockSpec((1, N_MOD, d), lambda g, j: (norm_tile(g)[0], 0, 0)),
                  pl.BlockSpec((1, d), lambda g, j: (0, 0)),
                  pl.BlockSpec((d, tn), lambda g, j: (0, j))],
        out_specs=pl.BlockSpec((1, tm, tn), lambda g, j: mm_tile(g) + (jnp.where(g == 0, 0, j),)),
        out_shape=jax.ShapeDtypeStruct((bsz, s, n), BF16),
        scratch_shapes=[pltpu.VMEM((tm, d), BF16), pltpu.VMEM((tm, d), BF16)],
        compiler_params=pltpu.CompilerParams(dimension_semantics=("arbitrary", "arbitrary"),
                                             vmem_limit_bytes=VMEM_LIMIT),
        name="inproj",
    )(x, mod, g.reshape(1, d), w_bf16)


def _angles_kernel(pos_ref, invf_ref, cos_ref, sin_ref):
    ang = pos_ref[...].astype(F32) * invf_ref[...]
    cos_ref[...] = jnp.cos(ang)
    sin_ref[...] = jnp.sin(ang)


def _rope_tables(positions):
    bsz, s = positions.shape
    per_row = LANES // ROPE_HALF
    rows = bsz * s // per_row
    inv = ROPE_THETA ** (-jnp.arange(ROPE_HALF, dtype=F32) * 2.0 / ROPE_DIM)
    invf = jnp.tile(inv, per_row).reshape(1, LANES)
    pos_rep = jnp.repeat(positions.reshape(rows, per_row), ROPE_HALF, axis=1)
    cos, sin = pl.pallas_call(
        _angles_kernel,
        out_shape=[jax.ShapeDtypeStruct((rows, LANES), F32)] * 2,
        name="angles",
    )(pos_rep, invf)
    cos = cos.reshape(bsz, s, ROPE_HALF)
    sin = sin.reshape(bsz, s, ROPE_HALF)
    rest = HEAD_DIM - ROPE_DIM
    cos = jnp.concatenate([cos, cos, jnp.ones((bsz, s, rest), F32)], axis=-1)
    sin = jnp.concatenate([sin, sin, jnp.zeros((bsz, s, rest), F32)], axis=-1)
    return cos, sin


def _rope(x, cos, sin):
    lane = lax.broadcasted_iota(I32, x.shape, 1)
    partner = jnp.where(lane < ROPE_HALF,
                        pltpu.roll(x, HEAD_DIM - ROPE_HALF, 1),
                        pltpu.roll(x, ROPE_HALF, 1))
    return x * cos + partner * sin


def _mixer_kernel(sink_ref, q_ref, kvp_ref, kvc_ref, kvn_ref, u0_ref, u1_ref, v0_ref, v1_ref,
                  cosp_ref, cosc_ref, cosn_ref, sinp_ref, sinc_ref, sinn_ref,
                  gn_ref, ws_ref, bst_ref, gg_ref, o_ref, ya_scr, yg_scr):
    tq = q_ref.shape[1]
    nsub = tq // BLOCK
    aw = ATTN_HEADS * HEAD_DIM
    i = pl.program_id(1)
    sign = jnp.where(lax.broadcasted_iota(I32, (1, HEAD_DIM), 1) < ROPE_HALF, -1.0, 1.0)

    cos_c, sin_c = cosc_ref[0], sinc_ref[0] * sign
    tabs = ((cosp_ref[0], sinp_ref[0] * sign), (cos_c, sin_c), (cosn_ref[0], sinn_ref[0] * sign))
    kbands, vbands = [], []
    for h in range(KV_HEADS):
        kparts, vparts = [], []
        for ref, (cs, sn) in zip((kvp_ref, kvc_ref, kvn_ref), tabs):
            k = ref[0, :, h * HEAD_DIM:(h + 1) * HEAD_DIM].astype(F32)
            kparts.append(_rope(k, cs, sn).astype(BF16))
            vparts.append(ref[0, :, (KV_HEADS + h) * HEAD_DIM:(KV_HEADS + h + 1) * HEAD_DIM])
        kbands.append(jnp.concatenate(kparts, axis=0))
        vbands.append(jnp.concatenate(vparts, axis=0))

    scale = HEAD_DIM ** -0.5
    cos_q, sin_q = cos_c * scale, sin_c * scale
    qh = [_rope(q_ref[0, :, h * HEAD_DIM:(h + 1) * HEAD_DIM].astype(F32), cos_q, sin_q).astype(BF16)
          for h in range(ATTN_HEADS)]

    rows = Q_PER_KV * BLOCK
    band = 3 * BLOCK
    q_off = lax.broadcasted_iota(I32, (rows, BLOCK), 0) & (BLOCK - 1)
    k_off = lax.broadcasted_iota(I32, (rows, BLOCK), 1)
    bias_prev = jnp.where(k_off >= q_off, 0.0, NEG_INF)
    bias_next = jnp.where(k_off <= q_off, 0.0, NEG_INF)
    first_tile = i == 0
    last_tile = i == pl.num_programs(1) - 1
    for s in range(nsub):
        bp = jnp.where(first_tile, NEG_INF, bias_prev) if s == 0 else bias_prev
        bn = jnp.where(last_tile, NEG_INF, bias_next) if s == nsub - 1 else bias_next
        for h in range(KV_HEADS):
            q4 = jnp.concatenate([qh[h * Q_PER_KV + g][s * BLOCK:(s + 1) * BLOCK]
                                  for g in range(Q_PER_KV)], axis=0)
            kb = kbands[h][s * BLOCK:s * BLOCK + band]
            vb = vbands[h][s * BLOCK:s * BLOCK + band]
            sc = lax.dot_general(q4, kb, (((1,), (1,)), ((), ())), preferred_element_type=F32)
            sc = jnp.concatenate([sc[:, :BLOCK] + bp, sc[:, BLOCK:2 * BLOCK], sc[:, 2 * BLOCK:] + bn],
                                 axis=1)
            sink = jnp.concatenate([jnp.full((BLOCK, 1), sink_ref[h * Q_PER_KV + g], F32)
                                    for g in range(Q_PER_KV)], axis=0)
            m = jnp.maximum(jnp.max(sc, axis=-1, keepdims=True), sink)
            p = jnp.exp(sc - m)
            den = jnp.sum(p, axis=-1, keepdims=True) + jnp.exp(sink - m)
            o = jnp.dot(p.astype(BF16), vb, preferred_element_type=F32) / den
            for g in range(Q_PER_KV):
                hq = h * Q_PER_KV + g
                ya_scr[s * BLOCK:(s + 1) * BLOCK, hq * HEAD_DIM:(hq + 1) * HEAD_DIM] = (
                    o[g * BLOCK:(g + 1) * BLOCK])

    half_heads = GMLP_HEADS // 2
    for h in range(GMLP_HEADS):
        u_ref = u0_ref if h < half_heads else u1_ref
        v_ref = v0_ref if h < half_heads else v1_ref
        hh = h % half_heads
        u = _gelu(u_ref[0, :, hh * HEAD_DIM:(hh + 1) * HEAD_DIM].astype(F32))
        v = _gelu(v_ref[0, :, hh * HEAD_DIM:(hh + 1) * HEAD_DIM].astype(F32))
        mu = jnp.mean(v, axis=-1, keepdims=True)
        dv = v - mu
        var = jnp.mean(dv * dv, axis=-1, keepdims=True)
        vn = (dv * lax.rsqrt(var + EPS) * gn_ref[h:h + 1, :]).astype(BF16)
        w = ws_ref[h].astype(BF16)
        bias = bst_ref[:, h:h + 1]
        for cidx in range(nsub):
            sl = slice(cidx * BLOCK, (cidx + 1) * BLOCK)
            mixed = jnp.dot(w, vn[sl], preferred_element_type=F32) + bias
            yg_scr[sl, h * HEAD_DIM:(h + 1) * HEAD_DIM] = u[sl] * mixed

    o_ref[0, :, :aw] = _rms(ya_scr[...], gg_ref[:, :aw]).astype(BF16)
    o_ref[0, :, aw:] = _rms(yg_scr[...], gg_ref[:, aw:]).astype(BF16)


def _mixer(z, cos, sin, sink, gn, ws, bs, gg):
    assert WINDOW == BLOCK
    bsz, s, _ = z.shape
    tq = min(256, s)
    nsub = tq // BLOCK
    nblk = s // BLOCK
    aw = ATTN_HEADS * HEAD_DIM
    gw = GMLP_HEADS * HEAD_DIM
    cw = 512

    def prev_blk(b, i):
        return (b, jnp.maximum(i * nsub - 1, 0), 2)

    def next_blk(b, i):
        return (b, jnp.minimum((i + 1) * nsub, nblk - 1), 2)

    tab_specs = [pl.BlockSpec((1, BLOCK, HEAD_DIM), lambda b, i: prev_blk(b, i)[:2] + (0,)),
                 pl.BlockSpec((1, tq, HEAD_DIM), lambda b, i: (b, i, 0)),
                 pl.BlockSpec((1, BLOCK, HEAD_DIM), lambda b, i: next_blk(b, i)[:2] + (0,))]
    return pl.pallas_call(
        _mixer_kernel,
        grid=(bsz, s // tq),
        in_specs=[pl.BlockSpec(memory_space=pltpu.SMEM),
                  pl.BlockSpec((1, tq, aw), lambda b, i: (b, i, 0)),
                  pl.BlockSpec((1, BLOCK, cw), prev_blk),
                  pl.BlockSpec((1, tq, cw), lambda b, i: (b, i, 2)),
                  pl.BlockSpec((1, BLOCK, cw), next_blk),
                  pl.BlockSpec((1, tq, cw), lambda b, i: (b, i, 3)),
                  pl.BlockSpec((1, tq, cw), lambda b, i: (b, i, 4)),
                  pl.BlockSpec((1, tq, cw), lambda b, i: (b, i, 5)),
                  pl.BlockSpec((1, tq, cw), lambda b, i: (b, i, 6)),
                  *tab_specs, *tab_specs,
                  pl.BlockSpec((GMLP_HEADS, HEAD_DIM), lambda b, i: (0, 0)),
                  pl.BlockSpec((GMLP_HEADS, BLOCK, BLOCK), lambda b, i: (0, 0, 0)),
                  pl.BlockSpec((BLOCK, GMLP_HEADS), lambda b, i: (0, 0)),
                  pl.BlockSpec((1, aw + gw), lambda b, i: (0, 0))],
        out_specs=pl.BlockSpec((1, tq, aw + gw), lambda b, i: (b, i, 0)),
        out_shape=jax.ShapeDtypeStruct((bsz, s, aw + gw), BF16),
        scratch_shapes=[pltpu.VMEM((tq, aw), F32), pltpu.VMEM((tq, gw), F32)],
        compiler_params=pltpu.CompilerParams(dimension_semantics=("arbitrary", "arbitrary"),
                                             vmem_limit_bytes=VMEM_LIMIT),
        name="mixer",
    )(sink, z, z, z, z, z, z, z, z, cos, cos, cos, sin, sin, sin, gn, ws, bs.T,
      gg.reshape(1, aw + gw))


def _split_bf16(x):
    hi = x.astype(BF16)
    lo = (x - hi.astype(F32)).astype(BF16)
    return hi, lo


def _outproj_kernel(y_ref, w_ref, x_ref, mod_ref, gpost_ref, gpre_ref, wr_ref, br_ref,
                    h_ref, f_ref, eid_ref, ew_ref, mix_a, mix_b):
    g = pl.program_id(0)

    @pl.when(g == 0)
    def _():
        mix_b[...] = jnp.zeros(mix_b.shape, mix_b.dtype)

    def step(mix_w, mix_r):
        mix_w[...] = jnp.dot(y_ref[0], w_ref[...], preferred_element_type=F32)
        _outproj_finish(mix_r[...], x_ref, mod_ref, gpost_ref, gpre_ref, wr_ref, br_ref,
                        h_ref, f_ref, eid_ref, ew_ref)

    @pl.when(g % 2 == 0)
    def _():
        step(mix_a, mix_b)

    @pl.when(g % 2 == 1)
    def _():
        step(mix_b, mix_a)


def _outproj_finish(mix, x_ref, mod_ref, gpost_ref, gpre_ref, wr_ref, br_ref,
                    h_ref, f_ref, eid_ref, ew_ref):
    mod = mod_ref[0]
    h1 = x_ref[0] + mod[2:3] * _rms(mix, gpost_ref[...])
    h_ref[0] = h1
    f = _rms(h1, gpre_ref[...]) * (1.0 + mod[4:5]) + mod[3:4]
    f_ref[...] = f

    f_hi, f_lo = _split_bf16(f)
    w_hi, w_lo = _split_bf16(wr_ref[...])
    nt = (((1,), (1,)), ((), ()))
    logits = (lax.dot_general(w_hi, f_hi, nt, preferred_element_type=F32)
              + lax.dot_general(w_lo, f_hi, nt, preferred_element_type=F32)
              + lax.dot_general(w_hi, f_lo, nt, preferred_element_type=F32)) + br_ref[...]

    gl = [logits[g:g + 1] for g in range(N_GROUPS)]
    gmax = functools.reduce(jnp.maximum, gl)
    gidx = jnp.full(gmax.shape, N_GROUPS - 1, I32)
    for g in range(N_GROUPS - 2, -1, -1):
        gidx = jnp.where(gl[g] == gmax, g, gidx)
    gval = 1.0 / functools.reduce(lambda a, b: a + b, [jnp.exp(v - gmax) for v in gl])

    es = []
    for e in range(EXPERTS_PER_GROUP):
        r = N_GROUPS + (N_GROUPS - 1) * EXPERTS_PER_GROUP + e
        v = logits[r:r + 1]
        for g in range(N_GROUPS - 2, -1, -1):
            r = N_GROUPS + g * EXPERTS_PER_GROUP + e
            v = jnp.where(gidx == g, logits[r:r + 1], v)
        es.append(v)
    m1 = functools.reduce(jnp.maximum, es)
    i1 = jnp.full(m1.shape, EXPERTS_PER_GROUP - 1, I32)
    for e in range(EXPERTS_PER_GROUP - 2, -1, -1):
        i1 = jnp.where(es[e] == m1, e, i1)
    rest = [jnp.where(i1 == e, -jnp.inf, es[e]) for e in range(EXPERTS_PER_GROUP)]
    m2 = functools.reduce(jnp.maximum, rest)
    i2 = jnp.full(m2.shape, EXPERTS_PER_GROUP - 1, I32)
    for e in range(EXPERTS_PER_GROUP - 2, -1, -1):
        i2 = jnp.where(rest[e] == m2, e, i2)
    p2 = jnp.exp(m2 - m1)
    w1 = gval / (1.0 + p2)
    w2 = gval * p2 / (1.0 + p2)
    eid_ref[0:1, :] = gidx * EXPERTS_PER_GROUP + i1
    eid_ref[1:2, :] = gidx * EXPERTS_PER_GROUP + i2
    ew_ref[0:1, :] = w1
    ew_ref[1:2, :] = w2


def _outproj(ycat, w_out_bf16, x, mod, gpost, gpre, wr, br):
    bsz, s, d = x.shape
    t = bsz * s
    tm = min(256, s)
    nti = s // tm
    n = bsz * nti

    def proj(g):
        gg = jnp.minimum(g, n - 1)
        return gg // nti, gg % nti

    def fin(g):
        gg = jnp.maximum(g - 1, 0)
        return gg // nti, gg % nti

    return pl.pallas_call(
        _outproj_kernel,
        grid=(n + 1,),
        in_specs=[pl.BlockSpec((1, tm, d), lambda g: proj(g) + (0,)),
                  pl.BlockSpec((d, d), lambda g: (0, 0)),
                  pl.BlockSpec((1, tm, d), lambda g: fin(g) + (0,)),
                  pl.BlockSpec((1, N_MOD, d), lambda g: (fin(g)[0], 0, 0)),
                  pl.BlockSpec((1, d), lambda g: (0, 0)),
                  pl.BlockSpec((1, d), lambda g: (0, 0)),
                  pl.BlockSpec((ROUTER_ROWS, d), lambda g: (0, 0)),
                  pl.BlockSpec((ROUTER_ROWS, 1), lambda g: (0, 0))],
        out_specs=[pl.BlockSpec((1, tm, d), lambda g: fin(g) + (0,)),
                   pl.BlockSpec((tm, d), lambda g: (jnp.maximum(g - 1, 0), 0)),
                   pl.BlockSpec((TOP_K, tm), lambda g: (0, jnp.maximum(g - 1, 0))),
                   pl.BlockSpec((TOP_K, tm), lambda g: (0, jnp.maximum(g - 1, 0)))],
        out_shape=[jax.ShapeDtypeStruct((bsz, s, d), F32),
                   jax.ShapeDtypeStruct((t, d), F32),
                   jax.ShapeDtypeStruct((TOP_K, t), I32),
                   jax.ShapeDtypeStruct((TOP_K, t), F32)],
        scratch_shapes=[pltpu.VMEM((tm, d), F32), pltpu.VMEM((tm, d), F32)],
        compiler_params=pltpu.CompilerParams(dimension_semantics=("arbitrary",),
                                             vmem_limit_bytes=VMEM_LIMIT),
        name="outproj",
    )(ycat, w_out_bf16, x, mod, gpost.reshape(1, d), gpre.reshape(1, d), wr, br)


def _slots_kernel(eid_ref, pos_ref, tab_ref, rank_scr, *, n_slots):
    t = eid_ref.shape[1]
    chunk = min(512, t)
    nchunk = t // chunk
    tri = jnp.where(lax.broadcasted_iota(I32, (chunk, chunk), 0)
                    <= lax.broadcasted_iota(I32, (chunk, chunk), 1), 1.0, 0.0).astype(BF16)
    e_io = lax.broadcasted_iota(I32, (N_EXPERTS, chunk), 0)

    cnt = jnp.zeros((N_EXPERTS, 1), F32)
    for k in range(TOP_K):
        def rank_body(c, carry, k=k):
            off = pl.multiple_of(c * chunk, chunk)
            onehot = e_io == eid_ref[pl.ds(k, 1), pl.ds(off, chunk)]
            ones = jnp.where(onehot, 1.0, 0.0)
            prefix = jnp.dot(ones.astype(BF16), tri, preferred_element_type=F32) + carry
            rank = jnp.sum(jnp.where(onehot, prefix, 0.0), axis=0, keepdims=True) - 1.0
            rank_scr[pl.ds(k, 1), pl.ds(off, chunk)] = rank
            return carry + jnp.sum(ones, axis=1, keepdims=True)
        cnt = lax.fori_loop(0, nchunk, rank_body, cnt)

    padded = jnp.floor((cnt + (SLOT_TILE - 1)) * (1.0 / SLOT_TILE)) * SLOT_TILE
    sub = lax.broadcasted_iota(I32, (N_EXPERTS, LANES), 0)
    lan = lax.broadcasted_iota(I32, (N_EXPERTS, LANES), 1)
    padded_row = jnp.sum(jnp.where(sub == lan, padded, 0.0), axis=0, keepdims=True)
    start = jnp.sum(jnp.where(lan < sub, padded_row, 0.0), axis=1, keepdims=True)
    end = start + padded
    pad_lo = start + cnt

    for k in range(TOP_K):
        def pos_body(c, carry, k=k):
            off = pl.multiple_of(c * chunk, chunk)
            onehot = e_io == eid_ref[pl.ds(k, 1), pl.ds(off, chunk)]
            base = jnp.sum(jnp.where(onehot, start, 0.0), axis=0, keepdims=True)
            pos_ref[pl.ds(k, 1), pl.ds(off, chunk)] = (
                base + rank_scr[pl.ds(k, 1), pl.ds(off, chunk)]).astype(I32)
            return carry
        lax.fori_loop(0, nchunk, pos_body, 0)

    lo_row = jnp.sum(jnp.where(sub == lan, pad_lo, 0.0), axis=0, keepdims=True)
    hi_row = jnp.sum(jnp.where(sub + PAD_HI_LANE == lan, end, 0.0), axis=0, keepdims=True)
    first_row = jnp.sum(jnp.where(sub + TILE_START_LANE == lan, start * (1.0 / SLOT_TILE), 0.0),
                        axis=0, keepdims=True)
    lane_row = lax.broadcasted_iota(I32, (1, LANES), 1)
    total = jnp.sum(padded_row, axis=1, keepdims=True)
    tail = (jnp.where(lane_row == N_EXPERTS, total, 0.0)
            + jnp.where(lane_row == PAD_HI_LANE + N_EXPERTS, float(n_slots), 0.0)
            + jnp.where(lane_row == TILE_START_LANE + N_EXPERTS, total * (1.0 / SLOT_TILE), 0.0))
    tab_ref[...] = (lo_row + hi_row + first_row + tail).astype(I32)


def _slots(eid, n_tiles):
    t = eid.shape[1]
    return pl.pallas_call(
        functools.partial(_slots_kernel, n_slots=n_tiles * SLOT_TILE),
        out_shape=[jax.ShapeDtypeStruct((TOP_K, t), I32),
                   jax.ShapeDtypeStruct((1, LANES), I32)],
        scratch_shapes=[pltpu.VMEM((TOP_K, t), F32)],
        compiler_params=pltpu.CompilerParams(vmem_limit_bytes=VMEM_LIMIT),
        name="slots",
    )(eid)


def _invert_kernel(pos_ref, tab_ref, inv_ref, *, n_pairs, n_slots):
    def guard(r, c):
        inv_ref[r] = n_slots + r
        inv_ref[SLOT_TILE + n_slots + r] = n_slots + r
        return c
    lax.fori_loop(0, SLOT_TILE, guard, 0, unroll=8)

    spare = n_pairs
    for e in range(N_EXPERTS + 1):
        lo, hi = tab_ref[e], tab_ref[PAD_HI_LANE + e]

        def fill(p, c):
            inv_ref[SLOT_TILE + p] = c
            return c + 1
        spare = lax.fori_loop(lo, hi, fill, spare)

    def place(n, c):
        inv_ref[SLOT_TILE + pos_ref[n]] = n
        return c
    lax.fori_loop(0, n_pairs, place, 0, unroll=8)


def _invert(pos_flat, tab, n_tiles):
    n_pairs = pos_flat.shape[0]
    n_slots = n_tiles * SLOT_TILE
    return pl.pallas_call(
        functools.partial(_invert_kernel, n_pairs=n_pairs, n_slots=n_slots),
        in_specs=[pl.BlockSpec(memory_space=pltpu.SMEM), pl.BlockSpec(memory_space=pltpu.SMEM)],
        out_specs=pl.BlockSpec(memory_space=pltpu.SMEM),
        out_shape=jax.ShapeDtypeStruct((n_slots + 2 * SLOT_TILE,), I32),
        name="invert",
    )(pos_flat, tab)


def _experts_kernel(inv_ref, tab_ref, f_ref, wg_ref, wu_ref, wd_ref, y_ref,
                    wbg, wbu, wbd, xa, xb, ya, yb, zrow, gsem, ssem, zsem, *, n_tokens, n_slots):
    e = pl.program_id(0)
    j = pl.program_id(1)
    fc = wg_ref.shape[2]
    total = tab_ref[TILE_START_LANE + N_EXPERTS]

    def gather_copy(tile, r, xbuf):
        v = inv_ref[(tile + 1) * SLOT_TILE + r]
        return pltpu.make_async_copy(f_ref.at[pl.ds(v & (n_tokens - 1), 1)], xbuf.at[pl.ds(r, 1)], gsem)

    def scatter_copy(tile, r, ybuf):
        v = inv_ref[(tile + 1) * SLOT_TILE + r]
        return pltpu.make_async_copy(ybuf.at[pl.ds(r, 1)], y_ref.at[pl.ds(v, 1)], ssem)

    def rows(fn):
        def body(r, c):
            fn(r)
            return c
        lax.fori_loop(0, SLOT_TILE, body, 0, unroll=8)

    @pl.when((e == 0) & (j == 0))
    def _():
        ya[...] = jnp.zeros(ya.shape, ya.dtype)
        yb[...] = jnp.zeros(yb.shape, yb.dtype)
        zrow[...] = jnp.zeros(zrow.shape, zrow.dtype)
        rows(lambda r: gather_copy(0, r, xa).start())

        def zero_copy(r):
            return pltpu.make_async_copy(zrow.at[pl.ds(0, 1)], y_ref.at[pl.ds(r, 1)], zsem)

        def zstart(r, c):
            zero_copy(r).start()
            return c

        def zwait(r, c):
            zero_copy(r).wait()
            return c
        lax.fori_loop(total * SLOT_TILE, n_slots, zstart, 0)
        lax.fori_loop(total * SLOT_TILE, n_slots, zwait, 0)

    @pl.when(e < N_EXPERTS)
    def _():
        slot = e % 2
        col = pl.multiple_of(j * fc, fc)
        wbg[slot, :, pl.ds(col, fc)] = wg_ref[0].astype(BF16)
        wbu[slot, :, pl.ds(col, fc)] = wu_ref[0].astype(BF16)
        wbd[slot, pl.ds(col, fc), :] = wd_ref[0].astype(BF16)

    prev = jnp.maximum(e - 1, 0)
    first = tab_ref[TILE_START_LANE + prev]
    count = jnp.where(e == 0, 0, tab_ref[TILE_START_LANE + prev + 1] - first)
    lo = first + (count * j) // N_WCHUNK
    hi = first + (count * (j + 1)) // N_WCHUNK
    wslot = prev % 2

    def tile(t, c):
        def run(xcur, xnext, ycur, yprev, ws):
            rows(lambda r: gather_copy(t, r, xcur).wait())
            for r in range(SLOT_TILE):
                gather_copy(t + 1, r, xnext).start()
            for r in range(SLOT_TILE):
                scatter_copy(t - 1, r, yprev).start()
            x = xcur[...].astype(BF16)
            hg = jnp.dot(x, wbg[ws], preferred_element_type=F32)
            hu = jnp.dot(x, wbu[ws], preferred_element_type=F32)
            hid = (_silu(hg) * hu).astype(BF16)
            ycur[...] = jnp.dot(hid, wbd[ws], preferred_element_type=F32)
            rows(lambda r: scatter_copy(t - 1, r, yprev).wait())

        for ws in range(2):
            for parity, bufs in enumerate(((xa, xb, ya, yb), (xb, xa, yb, ya))):
                @pl.when((wslot == ws) & (t % 2 == parity))
                def _(bufs=bufs, ws=ws):
                    run(*bufs, ws)
        return c
    lax.fori_loop(lo, hi, tile, 0)

    @pl.when((e == N_EXPERTS) & (j == N_WCHUNK - 1))
    def _():
        last = total - 1

        def finish(xnext, ylast):
            rows(lambda r: gather_copy(total, r, xnext).wait())
            rows(lambda r: scatter_copy(last, r, ylast).start())
            rows(lambda r: scatter_copy(last, r, ylast).wait())

        @pl.when(last % 2 == 0)
        def _():
            finish(xb, ya)

        @pl.when(last % 2 == 1)
        def _():
            finish(xa, yb)


def _experts(inv, tab, f, wg, wu, wd, n_tiles):
    t, d = f.shape
    assert t & (t - 1) == 0
    fdim = wg.shape[2]
    fc = fdim // N_WCHUNK
    n_slots = n_tiles * SLOT_TILE

    def expert(e):
        return jnp.minimum(e, N_EXPERTS - 1)

    def chunk(e, j):
        return jnp.where(e == N_EXPERTS, N_WCHUNK - 1, j)

    return pl.pallas_call(
        functools.partial(_experts_kernel, n_tokens=t, n_slots=n_slots),
        grid_spec=pltpu.PrefetchScalarGridSpec(
            num_scalar_prefetch=2,
            grid=(N_EXPERTS + 1, N_WCHUNK),
            in_specs=[pl.BlockSpec(memory_space=pl.ANY),
                      pl.BlockSpec((1, d, fc), lambda e, j, inv, tab: (expert(e), 0, chunk(e, j))),
                      pl.BlockSpec((1, d, fc), lambda e, j, inv, tab: (expert(e), 0, chunk(e, j))),
                      pl.BlockSpec((1, fc, d), lambda e, j, inv, tab: (expert(e), chunk(e, j), 0))],
            out_specs=pl.BlockSpec(memory_space=pl.ANY),
            scratch_shapes=[pltpu.VMEM((2, d, fdim), BF16), pltpu.VMEM((2, d, fdim), BF16),
                            pltpu.VMEM((2, fdim, d), BF16),
                            pltpu.VMEM((SLOT_TILE, d), F32), pltpu.VMEM((SLOT_TILE, d), F32),
                            pltpu.VMEM((SLOT_TILE, d), F32), pltpu.VMEM((SLOT_TILE, d), F32),
                            pltpu.VMEM((8, d), F32),
                            pltpu.SemaphoreType.DMA, pltpu.SemaphoreType.DMA, pltpu.SemaphoreType.DMA]),
        out_shape=jax.ShapeDtypeStruct((n_slots + SLOT_TILE, d), F32),
        compiler_params=pltpu.CompilerParams(dimension_semantics=("arbitrary", "arbitrary"),
                                             vmem_limit_bytes=EXPERTS_VMEM_LIMIT),
        name="experts",
    )(inv, tab, f, wg, wu, wd)


def _final_kernel(y0_ref, y1_ref, h_ref, ew_ref, mod_ref, g_ref, o_ref):
    ew = ew_ref[...]
    y = ew[:, 0:1] * y0_ref[...] + ew[:, 1:2] * y1_ref[...]
    o_ref[0] = h_ref[0] + mod_ref[0][5:6] * _rms(y, g_ref[...])


def _final(y_pairs, h1, ew_t, mod, g):
    bsz, s, d = h1.shape
    t = bsz * s
    tg = min(512, s)
    nti = s // tg
    return pl.pallas_call(
        _final_kernel,
        grid=(bsz, nti),
        in_specs=[pl.BlockSpec((tg, d), lambda b, i: (b * nti + i, 0)),
                  pl.BlockSpec((tg, d), lambda b, i: (t // tg + b * nti + i, 0)),
                  pl.BlockSpec((1, tg, d), lambda b, i: (b, i, 0)),
                  pl.BlockSpec((tg, TOP_K), lambda b, i: (b * nti + i, 0)),
                  pl.BlockSpec((1, N_MOD, d), lambda b, i: (b, 0, 0)),
                  pl.BlockSpec((1, d), lambda b, i: (0, 0))],
        out_specs=pl.BlockSpec((1, tg, d), lambda b, i: (b, i, 0)),
        out_shape=jax.ShapeDtypeStruct((bsz, s, d), F32),
        compiler_params=pltpu.CompilerParams(dimension_semantics=("arbitrary", "arbitrary"),
                                             vmem_limit_bytes=VMEM_LIMIT),
        name="final",
    )(y_pairs, y_pairs, h1, ew_t, mod, g.reshape(1, d))


def kernel(x, c, positions, w_ada, b_ada, g_mix_pre, g_mix_post, w_in, sink_logits, gmlp_norm_g,
           w_spatial, b_spatial, g_group_out, w_out, g_ffn_pre, g_ffn_post, w_router_group,
           b_router_group, w_router_expert, b_router_expert, w_gate, w_up, w_down):
    bsz, s, d = x.shape
    t = bsz * s
    depth = w_ada.shape[0]
    fdim = w_gate.shape[-1]
    n_tiles = (TOP_K * t) // SLOT_TILE + N_EXPERTS
    n_slots = n_tiles * SLOT_TILE
    cos, sin = _rope_tables(positions)
    h = x
    for l in range(depth):
        mod = _ada(c, w_ada[l], b_ada[l]).reshape(bsz, N_MOD, d)
        z = _inproj(h, mod, g_mix_pre[l], w_in[l].astype(BF16))
        ycat = _mixer(z, cos, sin, sink_logits[l], gmlp_norm_g[l], w_spatial[l],
                      b_spatial[l], g_group_out[l])
        wr = jnp.concatenate([w_router_group[l], w_router_expert[l].reshape(d, N_EXPERTS)], axis=1)
        wr = jnp.pad(wr.T, ((0, ROUTER_ROWS - N_GROUPS - N_EXPERTS), (0, 0)))
        br = jnp.concatenate([b_router_group[l], b_router_expert[l].reshape(N_EXPERTS)])
        br = jnp.pad(br, (0, ROUTER_ROWS - N_GROUPS - N_EXPERTS)).reshape(ROUTER_ROWS, 1)
        h1, f, eid, ew = _outproj(ycat, w_out[l].astype(BF16), h, mod, g_mix_post[l], g_ffn_pre[l],
                                  wr, br)
        pos, tab = _slots(eid, n_tiles)
        tab = tab.reshape(LANES)
        inv = _invert(pos.reshape(TOP_K * t), tab, n_tiles)
        y_pairs = _experts(inv, tab, f,
                           w_gate[l].reshape(N_EXPERTS, d, fdim),
                           w_up[l].reshape(N_EXPERTS, d, fdim),
                           w_down[l].reshape(N_EXPERTS, fdim, d), n_tiles)
        h = _final(y_pairs, h1, ew.T, mod, g_ffn_post[l])
    return h
```

```python
import functools

import jax
import jax.numpy as jnp
from jax import lax
from jax.experimental import pallas as pl
from jax.experimental.pallas import tpu as pltpu

F32 = jnp.float32
BF16 = jnp.bfloat16
I32 = jnp.int32

HEAD_DIM = 128
ATTN_HEADS = 8
KV_HEADS = 2
Q_PER_KV = ATTN_HEADS // KV_HEADS
GMLP_HEADS = 8
WINDOW = 128
BLOCK = 128
ROPE_THETA = 500000.0
ROPE_DIM = HEAD_DIM // 4
ROPE_HALF = ROPE_DIM // 2
N_GROUPS = 4
EXPERTS_PER_GROUP = 4
N_EXPERTS = N_GROUPS * EXPERTS_PER_GROUP
TOP_K = 2
N_MOD = 6
EPS = 1e-6
NEG_INF = -1e30
LOG2_E = 1.4426950408889634
LANES = 128
BF16_SUBLANES = 16
PAD_HI_LANE = 32
TILE_START_LANE = 64
N_WCHUNK = 4
EXPERTS_VMEM_LIMIT = 58 * 1024 * 1024
ROUTER_ROWS = 32

SLOT_TILE = 256
SLOT_TILE_LOG2 = 8
VMEM_LIMIT = 52 * 1024 * 1024


def _rms(x, g):
    ms = jnp.mean(x * x, axis=-1, keepdims=True)
    return x * lax.rsqrt(ms + EPS) * g


def _silu(x):
    return x / (1.0 + jnp.exp(-x))


def _gelu(x):
    return 0.5 * x * (1.0 + lax.erf(x * 0.7071067811865476))


def _ada_kernel(c_ref, w_ref, b_ref, o_ref):
    ca = _silu(c_ref[...]).astype(BF16)
    o_ref[...] = jnp.dot(ca, w_ref[...].astype(BF16), preferred_element_type=F32) + b_ref[...]


def _ada(c, w, b):
    bsz, d = c.shape
    n = w.shape[1]
    tn = 1024
    return pl.pallas_call(
        _ada_kernel,
        grid=(n // tn,),
        in_specs=[pl.BlockSpec((bsz, d), lambda j: (0, 0)),
                  pl.BlockSpec((d, tn), lambda j: (0, j)),
                  pl.BlockSpec((1, tn), lambda j: (0, j))],
        out_specs=pl.BlockSpec((bsz, tn), lambda j: (0, j)),
        out_shape=jax.ShapeDtypeStruct((bsz, n), F32),
        compiler_params=pltpu.CompilerParams(dimension_semantics=("arbitrary",),
                                             vmem_limit_bytes=VMEM_LIMIT),
        name="ada",
    )(c, w, b.reshape(1, n))


def _inproj_kernel(x_ref, mod_ref, g_ref, w_ref, o_ref, a_a, a_b, *, rows_per_step):
    g = pl.program_id(0)
    j = pl.program_id(1)
    tm = a_a.shape[0]

    @pl.when((g == 0) & (j == 0))
    def _():
        a_b[...] = jnp.zeros(a_b.shape, a_b.dtype)

    def step(a_w, a_r):
        o_ref[0] = jnp.dot(a_r[...], w_ref[...], preferred_element_type=F32).astype(BF16)
        start = pl.multiple_of(jnp.minimum(j * rows_per_step, tm - rows_per_step), BF16_SUBLANES)
        mod = mod_ref[0]
        a = _rms(x_ref[0, pl.ds(start, rows_per_step), :], g_ref[...]) * (1.0 + mod[1:2]) + mod[0:1]
        a_w[pl.ds(start, rows_per_step), :] = a.astype(BF16)

    @pl.when(g % 2 == 0)
    def _():
        step(a_a, a_b)

    @pl.when(g % 2 == 1)
    def _():
        step(a_b, a_a)


def _inproj(x, mod, g, w_bf16):
    bsz, s, d = x.shape
    n = w_bf16.shape[1]
    tm = min(1024, s)
    tn = 512
    nj = n // tn
    nti = s // tm
    ntile = bsz * nti
    rows_per_step = -(-tm // (nj * BF16_SUBLANES)) * BF16_SUBLANES

    def norm_tile(g):
        gg = jnp.minimum(g, ntile - 1)
        return gg // nti, gg % nti

    def mm_tile(g):
        gg = jnp.maximum(g - 1, 0)
        return gg // nti, gg % nti

    return pl.pallas_call(
        functools.partial(_inproj_kernel, rows_per_step=rows_per_step),
        grid=(ntile + 1, nj),
        in_specs=[pl.BlockSpec((1, tm, d), lambda g, j: norm_tile(g) + (0,)),
                  pl.BlockSpec((1, N_MOD, d), lambda g, j: (norm_tile(g)[0], 0, 0)),
                  pl.BlockSpec((1, d), lambda g, j: (0, 0)),
                  pl.BlockSpec((d, tn), lambda g, j: (0, j))],
        out_specs=pl.BlockSpec((1, tm, tn), lambda g, j: mm_tile(g) + (jnp.where(g == 0, 0, j),)),
        out_shape=jax.ShapeDtypeStruct((bsz, s, n), BF16),
        scratch_shapes=[pltpu.VMEM((tm, d), BF16), pltpu.VMEM((tm, d), BF16)],
        compiler_params=pltpu.CompilerParams(dimension_semantics=("arbitrary", "arbitrary"),
                                             vmem_limit_bytes=VMEM_LIMIT),
        name="inproj",
    )(x, mod, g.reshape(1, d), w_bf16)


def _angles_kernel(pos_ref, invf_ref, cos_ref, sin_ref):
    ang = pos_ref[...].astype(F32) * invf_ref[...]
    cos_ref[...] = jnp.cos(ang)
    sin_ref[...] = jnp.sin(ang)


def _rope_tables(positions):
    bsz, s = positions.shape
    per_row = LANES // ROPE_HALF
    rows = bsz * s // per_row
    inv = ROPE_THETA ** (-jnp.arange(ROPE_HALF, dtype=F32) * 2.0 / ROPE_DIM)
    invf = jnp.tile(inv, per_row).reshape(1, LANES)
    pos_rep = jnp.repeat(positions.reshape(rows, per_row), ROPE_HALF, axis=1)
    cos, sin = pl.pallas_call(
        _angles_kernel,
        out_shape=[jax.ShapeDtypeStruct((rows, LANES), F32)] * 2,
        name="angles",
    )(pos_rep, invf)
    cos = cos.reshape(bsz, s, ROPE_HALF)
    sin = sin.reshape(bsz, s, ROPE_HALF)
    rest = HEAD_DIM - ROPE_DIM
    cos = jnp.concatenate([cos, cos, jnp.ones((bsz, s, rest), F32)], axis=-1)
    sin = jnp.concatenate([sin, sin, jnp.zeros((bsz, s, rest), F32)], axis=-1)
    return cos, sin


def _lane_mean(x):
    k = x.shape[1]
    ones = jnp.full((k, LANES), 1.0 / k, BF16)
    return jnp.dot(x.astype(BF16), ones, preferred_element_type=F32)


def _rope(x, cos, sin):
    lane = lax.broadcasted_iota(I32, x.shape, 1)
    partner = jnp.where(lane < ROPE_HALF,
                        pltpu.roll(x, HEAD_DIM - ROPE_HALF, 1),
                        pltpu.roll(x, ROPE_HALF, 1))
    return x * cos + partner * sin


def _mixer_kernel(sink_ref, q_ref, kvp_ref, kvc_ref, kvn_ref, u0_ref, u1_ref, v0_ref, v1_ref,
                  cosp_ref, cosc_ref, cosn_ref, sinp_ref, sinc_ref, sinn_ref,
                  gn_ref, ws_ref, bst_ref, gg_ref, o_ref, ya_scr, yg_scr):
    tq = q_ref.shape[1]
    nsub = tq // BLOCK
    aw = ATTN_HEADS * HEAD_DIM
    i = pl.program_id(1)
    sign = jnp.where(lax.broadcasted_iota(I32, (1, HEAD_DIM), 1) < ROPE_HALF, -1.0, 1.0)

    cos_c, sin_c = cosc_ref[0], sinc_ref[0] * sign
    tabs = ((cosp_ref[0], sinp_ref[0] * sign), (cos_c, sin_c), (cosn_ref[0], sinn_ref[0] * sign))
    scale = HEAD_DIM ** -0.5 * LOG2_E
    cos_q, sin_q = cos_c * scale, sin_c * scale

    def prepare_kv_head(h):
        kparts, vparts = [], []
        for ref, (cs, sn) in zip((kvp_ref, kvc_ref, kvn_ref), tabs):
            k = ref[0, :, h * HEAD_DIM:(h + 1) * HEAD_DIM].astype(F32)
            kparts.append(_rope(k, cs, sn).astype(BF16))
            vparts.append(ref[0, :, (KV_HEADS + h) * HEAD_DIM:(KV_HEADS + h + 1) * HEAD_DIM])
        kband = jnp.concatenate(kparts, axis=0)
        vband = jnp.concatenate(
            [jnp.concatenate(vparts, axis=0), jnp.ones((tq + 2 * BLOCK, HEAD_DIM), BF16)], axis=1)
        qs = []
        for g in range(Q_PER_KV):
            hq = h * Q_PER_KV + g
            q = q_ref[0, :, hq * HEAD_DIM:(hq + 1) * HEAD_DIM].astype(F32)
            qs.append(_rope(q, cos_q, sin_q).astype(BF16))
        return kband, vband, qs

    rows = Q_PER_KV * BLOCK
    band = 3 * BLOCK
    q_off = lax.broadcasted_iota(I32, (rows, BLOCK), 0) & (BLOCK - 1)
    k_off = lax.broadcasted_iota(I32, (rows, BLOCK), 1)
    bias_prev = jnp.where(k_off >= q_off, 0.0, NEG_INF)
    bias_next = jnp.where(k_off <= q_off, 0.0, NEG_INF)
    first_tile = i == 0
    last_tile = i == pl.num_programs(1) - 1
    def attention_block(s, h, kband, vband, qs):
        bp = jnp.where(first_tile, NEG_INF, bias_prev) if s == 0 else bias_prev
        bn = jnp.where(last_tile, NEG_INF, bias_next) if s == nsub - 1 else bias_next
        q4 = jnp.concatenate([q[s * BLOCK:(s + 1) * BLOCK] for q in qs], axis=0)
        kb = kband[s * BLOCK:s * BLOCK + band]
        vb = vband[s * BLOCK:s * BLOCK + band]
        sc = lax.dot_general(q4, kb, (((1,), (1,)), ((), ())), preferred_element_type=F32)
        sc = jnp.concatenate([sc[:, :BLOCK] + bp, sc[:, BLOCK:2 * BLOCK], sc[:, 2 * BLOCK:] + bn],
                             axis=1)
        sink = jnp.concatenate([jnp.full((BLOCK, 1), sink_ref[h * Q_PER_KV + g] * LOG2_E, F32)
                                for g in range(Q_PER_KV)], axis=0)
        m = jnp.maximum(jnp.max(sc, axis=-1, keepdims=True), sink)
        p = jnp.exp2(sc - m)
        pv = jnp.dot(p.astype(BF16), vb, preferred_element_type=F32)
        den = pv[:, HEAD_DIM:] + jnp.exp2(sink - m)
        o = pv[:, :HEAD_DIM] / den
        for g in range(Q_PER_KV):
            hq = h * Q_PER_KV + g
            ya_scr[s * BLOCK:(s + 1) * BLOCK, hq * HEAD_DIM:(hq + 1) * HEAD_DIM] = (
                o[g * BLOCK:(g + 1) * BLOCK])

    half_heads = GMLP_HEADS // 2

    def gmlp_head(h):
        u_ref = u0_ref if h < half_heads else u1_ref
        v_ref = v0_ref if h < half_heads else v1_ref
        hh = h % half_heads
        u = _gelu(u_ref[0, :, hh * HEAD_DIM:(hh + 1) * HEAD_DIM].astype(F32))
        v = _gelu(v_ref[0, :, hh * HEAD_DIM:(hh + 1) * HEAD_DIM].astype(F32))
        mu = _lane_mean(v)
        dv = v - mu
        var = _lane_mean(dv * dv)
        vn = (dv * lax.rsqrt(var + EPS) * gn_ref[h:h + 1, :]).astype(BF16)
        w = ws_ref[h].astype(BF16)
        bias = bst_ref[:, h:h + 1]
        for cidx in range(nsub):
            sl = slice(cidx * BLOCK, (cidx + 1) * BLOCK)
            mixed = jnp.dot(w, vn[sl], preferred_element_type=F32) + bias
            yg_scr[sl, h * HEAD_DIM:(h + 1) * HEAD_DIM] = u[sl] * mixed

    heads_per_block = -(-GMLP_HEADS // (KV_HEADS * nsub))
    next_head = 0
    for h in range(KV_HEADS):
        prepared = prepare_kv_head(h)
        for s in range(nsub):
            attention_block(s, h, *prepared)
            for _ in range(heads_per_block):
                if next_head < GMLP_HEADS:
                    gmlp_head(next_head)
                    next_head += 1
    for h in range(next_head, GMLP_HEADS):
        gmlp_head(h)

    for scr, lo in ((ya_scr, 0), (yg_scr, aw)):
        y = scr[...]
        width = y.shape[1]
        inv_rms = lax.rsqrt(_lane_mean(y * y) + EPS)
        inv_rms = jnp.concatenate([inv_rms] * (width // LANES), axis=1)
        o_ref[0, :, lo:lo + width] = (y * inv_rms * gg_ref[:, lo:lo + width]).astype(BF16)


def _mixer(z, cos, sin, sink, gn, ws, bs, gg):
    assert WINDOW == BLOCK
    bsz, s, _ = z.shape
    tq = min(256, s)
    nsub = tq // BLOCK
    nblk = s // BLOCK
    aw = ATTN_HEADS * HEAD_DIM
    gw = GMLP_HEADS * HEAD_DIM
    cw = 512

    def prev_blk(b, i):
        return (b, jnp.maximum(i * nsub - 1, 0), 2)

    def next_blk(b, i):
        return (b, jnp.minimum((i + 1) * nsub, nblk - 1), 2)

    tab_specs = [pl.BlockSpec((1, BLOCK, HEAD_DIM), lambda b, i: prev_blk(b, i)[:2] + (0,)),
                 pl.BlockSpec((1, tq, HEAD_DIM), lambda b, i: (b, i, 0)),
                 pl.BlockSpec((1, BLOCK, HEAD_DIM), lambda b, i: next_blk(b, i)[:2] + (0,))]
    return pl.pallas_call(
        _mixer_kernel,
        grid=(bsz, s // tq),
        in_specs=[pl.BlockSpec(memory_space=pltpu.SMEM),
                  pl.BlockSpec((1, tq, aw), lambda b, i: (b, i, 0)),
                  pl.BlockSpec((1, BLOCK, cw), prev_blk),
                  pl.BlockSpec((1, tq, cw), lambda b, i: (b, i, 2)),
                  pl.BlockSpec((1, BLOCK, cw), next_blk),
                  pl.BlockSpec((1, tq, cw), lambda b, i: (b, i, 3)),
                  pl.BlockSpec((1, tq, cw), lambda b, i: (b, i, 4)),
                  pl.BlockSpec((1, tq, cw), lambda b, i: (b, i, 5)),
                  pl.BlockSpec((1, tq, cw), lambda b, i: (b, i, 6)),
                  *tab_specs, *tab_specs,
                  pl.BlockSpec((GMLP_HEADS, HEAD_DIM), lambda b, i: (0, 0)),
                  pl.BlockSpec((GMLP_HEADS, BLOCK, BLOCK), lambda b, i: (0, 0, 0)),
                  pl.BlockSpec((BLOCK, GMLP_HEADS), lambda b, i: (0, 0)),
                  pl.BlockSpec((1, aw + gw), lambda b, i: (0, 0))],
        out_specs=pl.BlockSpec((1, tq, aw + gw), lambda b, i: (b, i, 0)),
        out_shape=jax.ShapeDtypeStruct((bsz, s, aw + gw), BF16),
        scratch_shapes=[pltpu.VMEM((tq, aw), F32), pltpu.VMEM((tq, gw), F32)],
        compiler_params=pltpu.CompilerParams(dimension_semantics=("arbitrary", "arbitrary"),
                                             vmem_limit_bytes=VMEM_LIMIT),
        name="mixer",
    )(sink, z, z, z, z, z, z, z, z, cos, cos, cos, sin, sin, sin, gn, ws, bs.T,
      gg.reshape(1, aw + gw))


def _split_bf16(x):
    hi = x.astype(BF16)
    lo = (x - hi.astype(F32)).astype(BF16)
    return hi, lo


def _outproj_kernel(y_ref, w_ref, x_ref, mod_ref, gpost_ref, gpre_ref, wr_ref, br_ref,
                    h_ref, f_ref, eid_ref, ew_ref, mix_a, mix_b, wr_split):
    g = pl.program_id(0)

    @pl.when(g == 0)
    def _():
        mix_b[...] = jnp.zeros(mix_b.shape, mix_b.dtype)
        w_hi, w_lo = _split_bf16(wr_ref[...])
        wr_split[:, :LANES] = w_hi
        wr_split[:, LANES:] = w_lo

    def step(mix_w, mix_r):
        mix_w[...] = jnp.dot(y_ref[0], w_ref[...], preferred_element_type=F32)
        for stage in _outproj_finish(mix_r, x_ref, mod_ref, gpost_ref, gpre_ref, wr_split, br_ref,
                                     h_ref, f_ref, eid_ref, ew_ref):
            stage()

    @pl.when(g % 2 == 0)
    def _():
        step(mix_a, mix_b)

    @pl.when(g % 2 == 1)
    def _():
        step(mix_b, mix_a)


def _outproj_finish(mix_ref, x_ref, mod_ref, gpost_ref, gpre_ref, wr_ref, br_ref,
                    h_ref, f_ref, eid_ref, ew_ref):
    state = {}

    def residual():
        mod = mod_ref[0]
        h1 = x_ref[0] + mod[2:3] * _rms(mix_ref[...], gpost_ref[...])
        h_ref[0] = h1
        state["h1"] = h1

    def prenorm():
        mod = mod_ref[0]
        f = _rms(state["h1"], gpre_ref[...]) * (1.0 + mod[4:5]) + mod[3:4]
        f_ref[...] = f
        state["f"] = f

    def logits():
        f_hi, f_lo = _split_bf16(state["f"])
        tm = f_hi.shape[0]
        r = jnp.dot(jnp.concatenate([f_hi, f_lo], axis=0), wr_ref[...],
                    preferred_element_type=F32)
        lg = (r[:tm, :LANES] + r[:tm, LANES:]) + (r[tm:, :LANES] + r[tm:, LANES:])
        state["logits"] = lg.T[:ROUTER_ROWS] + br_ref[...]

    def route():
        _route(state["logits"], eid_ref, ew_ref)

    return [residual, prenorm, logits, route]


def _route(logits, eid_ref, ew_ref):
    gl = [logits[g:g + 1] for g in range(N_GROUPS)]
    gmax = functools.reduce(jnp.maximum, gl)
    gidx = jnp.full(gmax.shape, N_GROUPS - 1, I32)
    for g in range(N_GROUPS - 2, -1, -1):
        gidx = jnp.where(gl[g] == gmax, g, gidx)
    gval = 1.0 / functools.reduce(lambda a, b: a + b, [jnp.exp(v - gmax) for v in gl])

    es = []
    for e in range(EXPERTS_PER_GROUP):
        r = N_GROUPS + (N_GROUPS - 1) * EXPERTS_PER_GROUP + e
        v = logits[r:r + 1]
        for g in range(N_GROUPS - 2, -1, -1):
            r = N_GROUPS + g * EXPERTS_PER_GROUP + e
            v = jnp.where(gidx == g, logits[r:r + 1], v)
        es.append(v)
    m1 = functools.reduce(jnp.maximum, es)
    i1 = jnp.full(m1.shape, EXPERTS_PER_GROUP - 1, I32)
    for e in range(EXPERTS_PER_GROUP - 2, -1, -1):
        i1 = jnp.where(es[e] == m1, e, i1)
    rest = [jnp.where(i1 == e, -jnp.inf, es[e]) for e in range(EXPERTS_PER_GROUP)]
    m2 = functools.reduce(jnp.maximum, rest)
    i2 = jnp.full(m2.shape, EXPERTS_PER_GROUP - 1, I32)
    for e in range(EXPERTS_PER_GROUP - 2, -1, -1):
        i2 = jnp.where(rest[e] == m2, e, i2)
    p2 = jnp.exp(m2 - m1)
    w1 = gval / (1.0 + p2)
    w2 = gval * p2 / (1.0 + p2)
    eid_ref[0:1, :] = gidx * EXPERTS_PER_GROUP + i1
    eid_ref[1:2, :] = gidx * EXPERTS_PER_GROUP + i2
    ew_ref[0:1, :] = w1
    ew_ref[1:2, :] = w2


def _outproj(ycat, w_out_bf16, x, mod, gpost, gpre, wr, br):
    bsz, s, d = x.shape
    t = bsz * s
    tm = min(256, s)
    nti = s // tm
    n = bsz * nti

    def proj(g):
        gg = jnp.minimum(g, n - 1)
        return gg // nti, gg % nti

    def fin(g):
        gg = jnp.maximum(g - 1, 0)
        return gg // nti, gg % nti

    return pl.pallas_call(
        _outproj_kernel,
        grid=(n + 1,),
        in_specs=[pl.BlockSpec((1, tm, d), lambda g: proj(g) + (0,)),
                  pl.BlockSpec((d, d), lambda g: (0, 0)),
                  pl.BlockSpec((1, tm, d), lambda g: fin(g) + (0,)),
                  pl.BlockSpec((1, N_MOD, d), lambda g: (fin(g)[0], 0, 0)),
                  pl.BlockSpec((1, d), lambda g: (0, 0)),
                  pl.BlockSpec((1, d), lambda g: (0, 0)),
                  pl.BlockSpec((d, LANES), lambda g: (0, 0)),
                  pl.BlockSpec((ROUTER_ROWS, 1), lambda g: (0, 0))],
        out_specs=[pl.BlockSpec((1, tm, d), lambda g: fin(g) + (0,)),
                   pl.BlockSpec((tm, d), lambda g: (jnp.maximum(g - 1, 0), 0)),
                   pl.BlockSpec((TOP_K, tm), lambda g: (0, jnp.maximum(g - 1, 0))),
                   pl.BlockSpec((TOP_K, tm), lambda g: (0, jnp.maximum(g - 1, 0)))],
        out_shape=[jax.ShapeDtypeStruct((bsz, s, d), F32),
                   jax.ShapeDtypeStruct((t, d), F32),
                   jax.ShapeDtypeStruct((TOP_K, t), I32),
                   jax.ShapeDtypeStruct((TOP_K, t), F32)],
        scratch_shapes=[pltpu.VMEM((tm, d), F32), pltpu.VMEM((tm, d), F32),
                        pltpu.VMEM((d, 2 * LANES), BF16)],
        compiler_params=pltpu.CompilerParams(dimension_semantics=("arbitrary",),
                                             vmem_limit_bytes=VMEM_LIMIT),
        name="outproj",
    )(ycat, w_out_bf16, x, mod, gpost.reshape(1, d), gpre.reshape(1, d), wr, br)


def _slots_kernel(eid_ref, pos_ref, tab_ref, rank_scr, *, n_slots):
    t = eid_ref.shape[1]
    chunk = min(512, t)
    nchunk = t // chunk
    tri = jnp.where(lax.broadcasted_iota(I32, (chunk, chunk), 0)
                    <= lax.broadcasted_iota(I32, (chunk, chunk), 1), 1.0, 0.0).astype(BF16)
    e_io = lax.broadcasted_iota(I32, (N_EXPERTS, chunk), 0)

    cnt = jnp.zeros((N_EXPERTS, 1), F32)
    for k in range(TOP_K):
        def rank_body(c, carry, k=k):
            off = pl.multiple_of(c * chunk, chunk)
            onehot = e_io == eid_ref[pl.ds(k, 1), pl.ds(off, chunk)]
            ones = jnp.where(onehot, 1.0, 0.0)
            prefix = jnp.dot(ones.astype(BF16), tri, preferred_element_type=F32) + carry
            rank = jnp.sum(jnp.where(onehot, prefix, 0.0), axis=0, keepdims=True) - 1.0
            rank_scr[pl.ds(k, 1), pl.ds(off, chunk)] = rank
            return carry + jnp.sum(ones, axis=1, keepdims=True)
        cnt = lax.fori_loop(0, nchunk, rank_body, cnt)

    padded = jnp.floor((cnt + (SLOT_TILE - 1)) * (1.0 / SLOT_TILE)) * SLOT_TILE
    sub = lax.broadcasted_iota(I32, (N_EXPERTS, LANES), 0)
    lan = lax.broadcasted_iota(I32, (N_EXPERTS, LANES), 1)
    padded_row = jnp.sum(jnp.where(sub == lan, padded, 0.0), axis=0, keepdims=True)
    start = jnp.sum(jnp.where(lan < sub, padded_row, 0.0), axis=1, keepdims=True)
    end = start + padded
    pad_lo = start + cnt

    for k in range(TOP_K):
        def pos_body(c, carry, k=k):
            off = pl.multiple_of(c * chunk, chunk)
            onehot = e_io == eid_ref[pl.ds(k, 1), pl.ds(off, chunk)]
            base = jnp.sum(jnp.where(onehot, start, 0.0), axis=0, keepdims=True)
            pos_ref[pl.ds(k, 1), pl.ds(off, chunk)] = (
                base + rank_scr[pl.ds(k, 1), pl.ds(off, chunk)]).astype(I32)
            return carry
        lax.fori_loop(0, nchunk, pos_body, 0)

    lo_row = jnp.sum(jnp.where(sub == lan, pad_lo, 0.0), axis=0, keepdims=True)
    hi_row = jnp.sum(jnp.where(sub + PAD_HI_LANE == lan, end, 0.0), axis=0, keepdims=True)
    first_row = jnp.sum(jnp.where(sub + TILE_START_LANE == lan, start * (1.0 / SLOT_TILE), 0.0),
                        axis=0, keepdims=True)
    lane_row = lax.broadcasted_iota(I32, (1, LANES), 1)
    total = jnp.sum(padded_row, axis=1, keepdims=True)
    tail = (jnp.where(lane_row == N_EXPERTS, total, 0.0)
            + jnp.where(lane_row == PAD_HI_LANE + N_EXPERTS, float(n_slots), 0.0)
            + jnp.where(lane_row == TILE_START_LANE + N_EXPERTS, total * (1.0 / SLOT_TILE), 0.0))
    tab_ref[...] = (lo_row + hi_row + first_row + tail).astype(I32)


def _slots(eid, n_tiles):
    t = eid.shape[1]
    return pl.pallas_call(
        functools.partial(_slots_kernel, n_slots=n_tiles * SLOT_TILE),
        out_shape=[jax.ShapeDtypeStruct((TOP_K, t), I32),
                   jax.ShapeDtypeStruct((1, LANES), I32)],
        scratch_shapes=[pltpu.VMEM((TOP_K, t), F32)],
        compiler_params=pltpu.CompilerParams(vmem_limit_bytes=VMEM_LIMIT),
        name="slots",
    )(eid)


def _invert_kernel(pos_ref, tab_ref, inv_ref, *, n_pairs, n_slots):
    def guard(r, c):
        inv_ref[r] = n_slots + r
        inv_ref[SLOT_TILE + n_slots + r] = n_slots + r
        return c
    lax.fori_loop(0, SLOT_TILE, guard, 0, unroll=8)

    spare = n_pairs
    for e in range(N_EXPERTS + 1):
        lo, hi = tab_ref[e], tab_ref[PAD_HI_LANE + e]

        def fill(p, c):
            inv_ref[SLOT_TILE + p] = c
            return c + 1
        spare = lax.fori_loop(lo, hi, fill, spare)

    def place(n, c):
        inv_ref[SLOT_TILE + pos_ref[n]] = n
        return c
    lax.fori_loop(0, n_pairs, place, 0, unroll=8)


def _invert(pos_flat, tab, n_tiles):
    n_pairs = pos_flat.shape[0]
    n_slots = n_tiles * SLOT_TILE
    return pl.pallas_call(
        functools.partial(_invert_kernel, n_pairs=n_pairs, n_slots=n_slots),
        in_specs=[pl.BlockSpec(memory_space=pltpu.SMEM), pl.BlockSpec(memory_space=pltpu.SMEM)],
        out_specs=pl.BlockSpec(memory_space=pltpu.SMEM),
        out_shape=jax.ShapeDtypeStruct((n_slots + 2 * SLOT_TILE,), I32),
        name="invert",
    )(pos_flat, tab)


def _experts_kernel(inv_ref, tab_ref, f_ref, wg_ref, wu_ref, wd_ref, y_ref,
                    wbg, wbu, wbd, xa, xb, ya, yb, zrow, gsem, ssem, zsem, *, n_tokens, n_slots):
    e = pl.program_id(0)
    j = pl.program_id(1)
    fc = wg_ref.shape[2]
    total = tab_ref[TILE_START_LANE + N_EXPERTS]

    def gather_copy(tile, r, xbuf):
        v = inv_ref[(tile + 1) * SLOT_TILE + r]
        return pltpu.make_async_copy(f_ref.at[pl.ds(v & (n_tokens - 1), 1)], xbuf.at[pl.ds(r, 1)], gsem)

    def scatter_copy(tile, r, ybuf):
        v = inv_ref[(tile + 1) * SLOT_TILE + r]
        return pltpu.make_async_copy(ybuf.at[pl.ds(r, 1)], y_ref.at[pl.ds(v, 1)], ssem)

    def gather_wait(xbuf):
        pltpu.make_async_copy(f_ref.at[pl.ds(0, SLOT_TILE)], xbuf, gsem).wait()

    def scatter_wait(ybuf):
        pltpu.make_async_copy(ybuf, y_ref.at[pl.ds(0, SLOT_TILE)], ssem).wait()

    def rows(fn):
        def body(r, c):
            fn(r)
            return c
        lax.fori_loop(0, SLOT_TILE, body, 0, unroll=8)

    @pl.when((e == 0) & (j == 0))
    def _():
        ya[...] = jnp.zeros(ya.shape, ya.dtype)
        yb[...] = jnp.zeros(yb.shape, yb.dtype)
        zrow[...] = jnp.zeros(zrow.shape, zrow.dtype)
        rows(lambda r: gather_copy(0, r, xa).start())

        def zero_copy(r):
            return pltpu.make_async_copy(zrow.at[pl.ds(0, 1)], y_ref.at[pl.ds(r, 1)], zsem)

        def zstart(r, c):
            zero_copy(r).start()
            return c

        def zwait(r, c):
            zero_copy(r).wait()
            return c
        lax.fori_loop(total * SLOT_TILE, n_slots, zstart, 0)
        lax.fori_loop(total * SLOT_TILE, n_slots, zwait, 0)

    @pl.when(e < N_EXPERTS)
    def _():
        slot = e % 2
        col = pl.multiple_of(j * fc, fc)
        wbg[slot, :, pl.ds(col, fc)] = wg_ref[0].astype(BF16)
        wbu[slot, :, pl.ds(col, fc)] = wu_ref[0].astype(BF16)
        wbd[slot, pl.ds(col, fc), :] = wd_ref[0].astype(BF16)

    prev = jnp.maximum(e - 1, 0)
    first = tab_ref[TILE_START_LANE + prev]
    count = jnp.where(e == 0, 0, tab_ref[TILE_START_LANE + prev + 1] - first)
    lo = first + (count * j) // N_WCHUNK
    hi = first + (count * (j + 1)) // N_WCHUNK
    wslot = prev % 2

    def tile(t, c):
        def run(xcur, xnext, ycur, yprev, ws):
            gather_wait(xcur)
            for r in range(SLOT_TILE):
                gather_copy(t + 1, r, xnext).start()
            for r in range(SLOT_TILE):
                scatter_copy(t - 1, r, yprev).start()
            x = xcur[...].astype(BF16)
            hg = jnp.dot(x, wbg[ws], preferred_element_type=F32)
            hu = jnp.dot(x, wbu[ws], preferred_element_type=F32)
            hid = (_silu(hg) * hu).astype(BF16)
            ycur[...] = jnp.dot(hid, wbd[ws], preferred_element_type=F32)
            scatter_wait(yprev)

        for ws in range(2):
            for parity, bufs in enumerate(((xa, xb, ya, yb), (xb, xa, yb, ya))):
                @pl.when((wslot == ws) & (t % 2 == parity))
                def _(bufs=bufs, ws=ws):
                    run(*bufs, ws)
        return c
    lax.fori_loop(lo, hi, tile, 0)

    @pl.when((e == N_EXPERTS) & (j == N_WCHUNK - 1))
    def _():
        last = total - 1

        def finish(xnext, ylast):
            gather_wait(xnext)
            rows(lambda r: scatter_copy(last, r, ylast).start())
            scatter_wait(ylast)

        @pl.when(last % 2 == 0)
        def _():
            finish(xb, ya)

        @pl.when(last % 2 == 1)
        def _():
            finish(xa, yb)


def _experts(inv, tab, f, wg, wu, wd, n_tiles):
    t, d = f.shape
    assert t & (t - 1) == 0
    fdim = wg.shape[2]
    fc = fdim // N_WCHUNK
    n_slots = n_tiles * SLOT_TILE

    def expert(e):
        return jnp.minimum(e, N_EXPERTS - 1)

    def chunk(e, j):
        return jnp.where(e == N_EXPERTS, N_WCHUNK - 1, j)

    return pl.pallas_call(
        functools.partial(_experts_kernel, n_tokens=t, n_slots=n_slots),
        grid_spec=pltpu.PrefetchScalarGridSpec(
            num_scalar_prefetch=2,
            grid=(N_EXPERTS + 1, N_WCHUNK),
            in_specs=[pl.BlockSpec(memory_space=pl.ANY),
                      pl.BlockSpec((1, d, fc), lambda e, j, inv, tab: (expert(e), 0, chunk(e, j))),
                      pl.BlockSpec((1, d, fc), lambda e, j, inv, tab: (expert(e), 0, chunk(e, j))),
                      pl.BlockSpec((1, fc, d), lambda e, j, inv, tab: (expert(e), chunk(e, j), 0))],
            out_specs=pl.BlockSpec(memory_space=pl.ANY),
            scratch_shapes=[pltpu.VMEM((2, d, fdim), BF16), pltpu.VMEM((2, d, fdim), BF16),
                            pltpu.VMEM((2, fdim, d), BF16),
                            pltpu.VMEM((SLOT_TILE, d), F32), pltpu.VMEM((SLOT_TILE, d), F32),
                            pltpu.VMEM((SLOT_TILE, d), F32), pltpu.VMEM((SLOT_TILE, d), F32),
                            pltpu.VMEM((8, d), F32),
                            pltpu.SemaphoreType.DMA, pltpu.SemaphoreType.DMA, pltpu.SemaphoreType.DMA]),
        out_shape=jax.ShapeDtypeStruct((n_slots + SLOT_TILE, d), F32),
        compiler_params=pltpu.CompilerParams(dimension_semantics=("arbitrary", "arbitrary"),
                                             vmem_limit_bytes=EXPERTS_VMEM_LIMIT),
        name="experts",
    )(inv, tab, f, wg, wu, wd)


def _final_kernel(y0_ref, y1_ref, h_ref, ew_ref, mod_ref, g_ref, o_ref):
    ew = ew_ref[...]
    y = ew[:, 0:1] * y0_ref[...] + ew[:, 1:2] * y1_ref[...]
    o_ref[0] = h_ref[0] + mod_ref[0][5:6] * _rms(y, g_ref[...])


def _final(y_pairs, h1, ew_t, mod, g):
    bsz, s, d = h1.shape
    t = bsz * s
    tg = min(512, s)
    nti = s // tg
    return pl.pallas_call(
        _final_kernel,
        grid=(bsz, nti),
        in_specs=[pl.BlockSpec((tg, d), lambda b, i: (b * nti + i, 0)),
                  pl.BlockSpec((tg, d), lambda b, i: (t // tg + b * nti + i, 0)),
                  pl.BlockSpec((1, tg, d), lambda b, i: (b, i, 0)),
                  pl.BlockSpec((tg, TOP_K), lambda b, i: (b * nti + i, 0)),
                  pl.BlockSpec((1, N_MOD, d), lambda b, i: (b, 0, 0)),
                  pl.BlockSpec((1, d), lambda b, i: (0, 0))],
        out_specs=pl.BlockSpec((1, tg, d), lambda b, i: (b, i, 0)),
        out_shape=jax.ShapeDtypeStruct((bsz, s, d), F32),
        compiler_params=pltpu.CompilerParams(dimension_semantics=("arbitrary", "arbitrary"),
                                             vmem_limit_bytes=VMEM_LIMIT),
        name="final",
    )(y_pairs, y_pairs, h1, ew_t, mod, g.reshape(1, d))


def kernel(x, c, positions, w_ada, b_ada, g_mix_pre, g_mix_post, w_in, sink_logits, gmlp_norm_g,
           w_spatial, b_spatial, g_group_out, w_out, g_ffn_pre, g_ffn_post, w_router_group,
           b_router_group, w_router_expert, b_router_expert, w_gate, w_up, w_down):
    bsz, s, d = x.shape
    t = bsz * s
    depth = w_ada.shape[0]
    fdim = w_gate.shape[-1]
    n_tiles = (TOP_K * t) // SLOT_TILE + N_EXPERTS
    n_slots = n_tiles * SLOT_TILE
    cos, sin = _rope_tables(positions)
    h = x
    for l in range(depth):
        mod = _ada(c, w_ada[l], b_ada[l]).reshape(bsz, N_MOD, d)
        z = _inproj(h, mod, g_mix_pre[l], w_in[l].astype(BF16))
        ycat = _mixer(z, cos, sin, sink_logits[l], gmlp_norm_g[l], w_spatial[l],
                      b_spatial[l], g_group_out[l])
        wr = jnp.concatenate([w_router_group[l], w_router_expert[l].reshape(d, N_EXPERTS)], axis=1)
        wr = jnp.pad(wr, ((0, 0), (0, LANES - N_GROUPS - N_EXPERTS)))
        br = jnp.concatenate([b_router_group[l], b_router_expert[l].reshape(N_EXPERTS)])
        br = jnp.pad(br, (0, ROUTER_ROWS - N_GROUPS - N_EXPERTS)).reshape(ROUTER_ROWS, 1)
        h1, f, eid, ew = _outproj(ycat, w_out[l].astype(BF16), h, mod, g_mix_post[l], g_ffn_pre[l],
                                  wr, br)
        pos, tab = _slots(eid, n_tiles)
        tab = tab.reshape(LANES)
        inv = _invert(pos.reshape(TOP_K * t), tab, n_tiles)
        y_pairs = _experts(inv, tab, f,
                           w_gate[l].reshape(N_EXPERTS, d, fdim),
                           w_up[l].reshape(N_EXPERTS, d, fdim),
                           w_down[l].reshape(N_EXPERTS, fdim, d), n_tiles)
        h = _final(y_pairs, h1, ew.T, mod, g_ffn_post[l])
    return h
```

```python
import functools

import jax
import jax.numpy as jnp
from jax import lax
from jax.experimental import pallas as pl
from jax.experimental.pallas import tpu as pltpu

F32 = jnp.float32
BF16 = jnp.bfloat16
I32 = jnp.int32

HEAD_DIM = 128
ATTN_HEADS = 8
KV_HEADS = 2
Q_PER_KV = ATTN_HEADS // KV_HEADS
GMLP_HEADS = 8
WINDOW = 128
BLOCK = 128
ROPE_THETA = 500000.0
ROPE_DIM = HEAD_DIM // 4
ROPE_HALF = ROPE_DIM // 2
N_GROUPS = 4
EXPERTS_PER_GROUP = 4
N_EXPERTS = N_GROUPS * EXPERTS_PER_GROUP
TOP_K = 2
N_MOD = 6
EPS = 1e-6
NEG_INF = -1e30
LOG2_E = 1.4426950408889634
LANES = 128
BF16_SUBLANES = 16
PAD_HI_LANE = 32
TILE_START_LANE = 64
N_WCHUNK = 4
EXPERTS_VMEM_LIMIT = 58 * 1024 * 1024
ROUTER_ROWS = 32

SLOT_TILE = 256
SLOT_TILE_LOG2 = 8
VMEM_LIMIT = 52 * 1024 * 1024


def _rms(x, g):
    ms = jnp.mean(x * x, axis=-1, keepdims=True)
    return x * lax.rsqrt(ms + EPS) * g


def _silu(x):
    return x / (1.0 + jnp.exp(-x))


def _gelu(x):
    return 0.5 * x * (1.0 + lax.erf(x * 0.7071067811865476))


def _ada_kernel(c_ref, w_ref, b_ref, o_ref):
    ca = _silu(c_ref[...]).astype(BF16)
    o_ref[...] = jnp.dot(ca, w_ref[...].astype(BF16), preferred_element_type=F32) + b_ref[...]


def _ada(c, w, b):
    bsz, d = c.shape
    n = w.shape[1]
    tn = 1024
    return pl.pallas_call(
        _ada_kernel,
        grid=(n // tn,),
        in_specs=[pl.BlockSpec((bsz, d), lambda j: (0, 0)),
                  pl.BlockSpec((d, tn), lambda j: (0, j)),
                  pl.BlockSpec((1, tn), lambda j: (0, j))],
        out_specs=pl.BlockSpec((bsz, tn), lambda j: (0, j)),
        out_shape=jax.ShapeDtypeStruct((bsz, n), F32),
        compiler_params=pltpu.CompilerParams(dimension_semantics=("arbitrary",),
                                             vmem_limit_bytes=VMEM_LIMIT),
        name="ada",
    )(c, w, b.reshape(1, n))


def _inproj_kernel(x_ref, mod_ref, g_ref, w_ref, o_ref, a_a, a_b, *, rows_per_step):
    g = pl.program_id(0)
    j = pl.program_id(1)
    tm = a_a.shape[0]

    @pl.when((g == 0) & (j == 0))
    def _():
        a_b[...] = jnp.zeros(a_b.shape, a_b.dtype)

    def step(a_w, a_r):
        o_ref[0] = jnp.dot(a_r[...], w_ref[...], preferred_element_type=F32).astype(BF16)
        start = pl.multiple_of(jnp.minimum(j * rows_per_step, tm - rows_per_step), BF16_SUBLANES)
        mod = mod_ref[0]
        a = _rms(x_ref[0, pl.ds(start, rows_per_step), :], g_ref[...] * (1.0 + mod[1:2])) + mod[0:1]
        a_w[pl.ds(start, rows_per_step), :] = a.astype(BF16)

    @pl.when(g % 2 == 0)
    def _():
        step(a_a, a_b)

    @pl.when(g % 2 == 1)
    def _():
        step(a_b, a_a)


def _inproj(x, mod, g, w_bf16):
    bsz, s, d = x.shape
    n = w_bf16.shape[1]
    tm = min(1024, s)
    tn = 512
    nj = n // tn
    nti = s // tm
    ntile = bsz * nti
    rows_per_step = -(-tm // (nj * BF16_SUBLANES)) * BF16_SUBLANES

    def norm_tile(g):
        gg = jnp.minimum(g, ntile - 1)
        return gg // nti, gg % nti

    def mm_tile(g):
        gg = jnp.maximum(g - 1, 0)
        return gg // nti, gg % nti

    return pl.pallas_call(
        functools.partial(_inproj_kernel, rows_per_step=rows_per_step),
        grid=(ntile + 1, nj),
        in_specs=[pl.BlockSpec((1, tm, d), lambda g, j: norm_tile(g) + (0,)),
                  pl.BlockSpec((1, N_MOD, d), lambda g, j: (norm_tile(g)[0], 0, 0)),
                  pl.BlockSpec((1, d), lambda g, j: (0, 0)),
                  pl.BlockSpec((d, tn), lambda g, j: (0, j))],
        out_specs=pl.BlockSpec((1, tm, tn), lambda g, j: mm_tile(g) + (jnp.where(g == 0, 0, j),)),
        out_shape=jax.ShapeDtypeStruct((bsz, s, n), BF16),
        scratch_shapes=[pltpu.VMEM((tm, d), BF16), pltpu.VMEM((tm, d), BF16)],
        compiler_params=pltpu.CompilerParams(dimension_semantics=("arbitrary", "arbitrary"),
                                             vmem_limit_bytes=VMEM_LIMIT),
        name="inproj",
    )(x, mod, g.reshape(1, d), w_bf16)


def _angles_kernel(pos_ref, invf_ref, cos_ref, sin_ref):
    ang = pos_ref[...].astype(F32) * invf_ref[...]
    cos_ref[...] = jnp.cos(ang)
    sin_ref[...] = jnp.sin(ang)


def _rope_tables(positions):
    bsz, s = positions.shape
    per_row = LANES // ROPE_HALF
    rows = bsz * s // per_row
    inv = ROPE_THETA ** (-jnp.arange(ROPE_HALF, dtype=F32) * 2.0 / ROPE_DIM)
    invf = jnp.tile(inv, per_row).reshape(1, LANES)
    pos_rep = jnp.repeat(positions.reshape(rows, per_row), ROPE_HALF, axis=1)
    cos, sin = pl.pallas_call(
        _angles_kernel,
        out_shape=[jax.ShapeDtypeStruct((rows, LANES), F32)] * 2,
        name="angles",
    )(pos_rep, invf)
    cos = cos.reshape(bsz, s, ROPE_HALF)
    sin = sin.reshape(bsz, s, ROPE_HALF)
    rest = HEAD_DIM - ROPE_DIM
    cos = jnp.concatenate([cos, cos, jnp.ones((bsz, s, rest), F32)], axis=-1)
    sin = jnp.concatenate([sin, sin, jnp.zeros((bsz, s, rest), F32)], axis=-1)
    return cos, sin


def _lane_mean(x):
    k = x.shape[1]
    ones = jnp.full((k, LANES), 1.0 / k, BF16)
    return jnp.dot(x.astype(BF16), ones, preferred_element_type=F32)


def _rope(x, cos, sin):
    lane = lax.broadcasted_iota(I32, x.shape, 1)
    partner = jnp.where(lane < ROPE_HALF,
                        pltpu.roll(x, HEAD_DIM - ROPE_HALF, 1),
                        pltpu.roll(x, ROPE_HALF, 1))
    return x * cos + partner * sin


def _mixer_kernel(sink_ref, q_ref, kvp_ref, kvc_ref, kvn_ref, u0_ref, u1_ref, v0_ref, v1_ref,
                  cosp_ref, cosc_ref, cosn_ref, sinp_ref, sinc_ref, sinn_ref,
                  gn_ref, ws_ref, bst_ref, gg_ref, o_ref, ya_scr, yg_scr):
    tq = q_ref.shape[1]
    nsub = tq // BLOCK
    aw = ATTN_HEADS * HEAD_DIM
    i = pl.program_id(1)
    sign = jnp.where(lax.broadcasted_iota(I32, (1, HEAD_DIM), 1) < ROPE_HALF, -1.0, 1.0)

    cos_c, sin_c = cosc_ref[0], sinc_ref[0] * sign
    tabs = ((cosp_ref[0], sinp_ref[0] * sign), (cos_c, sin_c), (cosn_ref[0], sinn_ref[0] * sign))
    scale = HEAD_DIM ** -0.5 * LOG2_E
    cos_q, sin_q = cos_c * scale, sin_c * scale

    def prepare_kv_head(h):
        kparts, vparts = [], []
        for ref, (cs, sn) in zip((kvp_ref, kvc_ref, kvn_ref), tabs):
            k = ref[0, :, h * HEAD_DIM:(h + 1) * HEAD_DIM].astype(F32)
            kparts.append(_rope(k, cs, sn).astype(BF16))
            vparts.append(ref[0, :, (KV_HEADS + h) * HEAD_DIM:(KV_HEADS + h + 1) * HEAD_DIM])
        kband = jnp.concatenate(kparts, axis=0)
        vband = jnp.concatenate(
            [jnp.concatenate(vparts, axis=0), jnp.ones((tq + 2 * BLOCK, HEAD_DIM), BF16)], axis=1)
        qs = []
        for g in range(Q_PER_KV):
            hq = h * Q_PER_KV + g
            q = q_ref[0, :, hq * HEAD_DIM:(hq + 1) * HEAD_DIM].astype(F32)
            qs.append(_rope(q, cos_q, sin_q).astype(BF16))
        return kband, vband, qs

    rows = Q_PER_KV * BLOCK
    band = 3 * BLOCK
    q_off = lax.broadcasted_iota(I32, (rows, BLOCK), 0) & (BLOCK - 1)
    k_off = lax.broadcasted_iota(I32, (rows, BLOCK), 1)
    bias_prev = jnp.where(k_off >= q_off, 0.0, NEG_INF)
    bias_next = jnp.where(k_off <= q_off, 0.0, NEG_INF)
    first_tile = i == 0
    last_tile = i == pl.num_programs(1) - 1
    def attention_block(s, h, kband, vband, qs):
        bp = jnp.where(first_tile, NEG_INF, bias_prev) if s == 0 else bias_prev
        bn = jnp.where(last_tile, NEG_INF, bias_next) if s == nsub - 1 else bias_next
        q4 = jnp.concatenate([q[s * BLOCK:(s + 1) * BLOCK] for q in qs], axis=0)
        kb = kband[s * BLOCK:s * BLOCK + band]
        vb = vband[s * BLOCK:s * BLOCK + band]
        sc = lax.dot_general(q4, kb, (((1,), (1,)), ((), ())), preferred_element_type=F32)
        sc = jnp.concatenate([sc[:, :BLOCK] + bp, sc[:, BLOCK:2 * BLOCK], sc[:, 2 * BLOCK:] + bn],
                             axis=1)
        sink = jnp.concatenate([jnp.full((BLOCK, 1), sink_ref[h * Q_PER_KV + g] * LOG2_E, F32)
                                for g in range(Q_PER_KV)], axis=0)
        m = jnp.maximum(jnp.max(sc, axis=-1, keepdims=True), sink)
        p = jnp.exp2(sc - m)
        pv = jnp.dot(p.astype(BF16), vb, preferred_element_type=F32)
        den = pv[:, HEAD_DIM:] + jnp.exp2(sink - m)
        o = pv[:, :HEAD_DIM] / den
        for g in range(Q_PER_KV):
            hq = h * Q_PER_KV + g
            ya_scr[s * BLOCK:(s + 1) * BLOCK, hq * HEAD_DIM:(hq + 1) * HEAD_DIM] = (
                o[g * BLOCK:(g + 1) * BLOCK])

    half_heads = GMLP_HEADS // 2

    def gmlp_head(h):
        u_ref = u0_ref if h < half_heads else u1_ref
        v_ref = v0_ref if h < half_heads else v1_ref
        hh = h % half_heads
        u = _gelu(u_ref[0, :, hh * HEAD_DIM:(hh + 1) * HEAD_DIM].astype(F32))
        v = _gelu(v_ref[0, :, hh * HEAD_DIM:(hh + 1) * HEAD_DIM].astype(F32))
        mu = _lane_mean(v)
        dv = v - mu
        var = _lane_mean(dv * dv)
        vn = (dv * lax.rsqrt(var + EPS) * gn_ref[h:h + 1, :]).astype(BF16)
        w = ws_ref[h].astype(BF16)
        bias = bst_ref[:, h:h + 1]
        for cidx in range(nsub):
            sl = slice(cidx * BLOCK, (cidx + 1) * BLOCK)
            mixed = jnp.dot(w, vn[sl], preferred_element_type=F32) + bias
            yg_scr[sl, h * HEAD_DIM:(h + 1) * HEAD_DIM] = u[sl] * mixed

    heads_per_block = -(-GMLP_HEADS // (KV_HEADS * nsub))
    next_head = 0
    for h in range(KV_HEADS):
        prepared = prepare_kv_head(h)
        for s in range(nsub):
            attention_block(s, h, *prepared)
            for _ in range(heads_per_block):
                if next_head < GMLP_HEADS:
                    gmlp_head(next_head)
                    next_head += 1
    for h in range(next_head, GMLP_HEADS):
        gmlp_head(h)

    for scr, lo in ((ya_scr, 0), (yg_scr, aw)):
        y = scr[...]
        width = y.shape[1]
        inv_rms = lax.rsqrt(_lane_mean(y * y) + EPS)
        inv_rms = jnp.concatenate([inv_rms] * (width // LANES), axis=1)
        o_ref[0, :, lo:lo + width] = (y * inv_rms * gg_ref[:, lo:lo + width]).astype(BF16)


def _mixer(z, cos, sin, sink, gn, ws, bs, gg):
    assert WINDOW == BLOCK
    bsz, s, _ = z.shape
    tq = min(512, s)
    nsub = tq // BLOCK
    nblk = s // BLOCK
    aw = ATTN_HEADS * HEAD_DIM
    gw = GMLP_HEADS * HEAD_DIM
    cw = 512

    def prev_blk(b, i):
        return (b, jnp.maximum(i * nsub - 1, 0), 2)

    def next_blk(b, i):
        return (b, jnp.minimum((i + 1) * nsub, nblk - 1), 2)

    tab_specs = [pl.BlockSpec((1, BLOCK, HEAD_DIM), lambda b, i: prev_blk(b, i)[:2] + (0,)),
                 pl.BlockSpec((1, tq, HEAD_DIM), lambda b, i: (b, i, 0)),
                 pl.BlockSpec((1, BLOCK, HEAD_DIM), lambda b, i: next_blk(b, i)[:2] + (0,))]
    return pl.pallas_call(
        _mixer_kernel,
        grid=(bsz, s // tq),
        in_specs=[pl.BlockSpec(memory_space=pltpu.SMEM),
                  pl.BlockSpec((1, tq, aw), lambda b, i: (b, i, 0)),
                  pl.BlockSpec((1, BLOCK, cw), prev_blk),
                  pl.BlockSpec((1, tq, cw), lambda b, i: (b, i, 2)),
                  pl.BlockSpec((1, BLOCK, cw), next_blk),
                  pl.BlockSpec((1, tq, cw), lambda b, i: (b, i, 3)),
                  pl.BlockSpec((1, tq, cw), lambda b, i: (b, i, 4)),
                  pl.BlockSpec((1, tq, cw), lambda b, i: (b, i, 5)),
                  pl.BlockSpec((1, tq, cw), lambda b, i: (b, i, 6)),
                  *tab_specs, *tab_specs,
                  pl.BlockSpec((GMLP_HEADS, HEAD_DIM), lambda b, i: (0, 0)),
                  pl.BlockSpec((GMLP_HEADS, BLOCK, BLOCK), lambda b, i: (0, 0, 0)),
                  pl.BlockSpec((BLOCK, GMLP_HEADS), lambda b, i: (0, 0)),
                  pl.BlockSpec((1, aw + gw), lambda b, i: (0, 0))],
        out_specs=pl.BlockSpec((1, tq, aw + gw), lambda b, i: (b, i, 0)),
        out_shape=jax.ShapeDtypeStruct((bsz, s, aw + gw), BF16),
        scratch_shapes=[pltpu.VMEM((tq, aw), F32), pltpu.VMEM((tq, gw), F32)],
        compiler_params=pltpu.CompilerParams(dimension_semantics=("arbitrary", "arbitrary"),
                                             vmem_limit_bytes=VMEM_LIMIT),
        name="mixer",
    )(sink, z, z, z, z, z, z, z, z, cos, cos, cos, sin, sin, sin, gn, ws, bs.T,
      gg.reshape(1, aw + gw))


def _split_bf16(x):
    hi = x.astype(BF16)
    lo = (x - hi.astype(F32)).astype(BF16)
    return hi, lo


def _outproj_kernel(y_ref, w_ref, x_ref, mod_ref, gpost_ref, gpre_ref, wr_ref, br_ref,
                    h_ref, f_ref, eid_ref, ew_ref, mix_a, mix_b, wr_split):
    g = pl.program_id(0)

    @pl.when(g == 0)
    def _():
        mix_b[...] = jnp.zeros(mix_b.shape, mix_b.dtype)
        w_hi, w_lo = _split_bf16(wr_ref[...])
        wr_split[:, :LANES] = w_hi
        wr_split[:, LANES:] = w_lo

    def step(mix_w, mix_r):
        mix_w[...] = jnp.dot(y_ref[0], w_ref[...], preferred_element_type=F32)
        for stage in _outproj_finish(mix_r, x_ref, mod_ref, gpost_ref, gpre_ref, wr_split, br_ref,
                                     h_ref, f_ref, eid_ref, ew_ref):
            stage()

    @pl.when(g % 2 == 0)
    def _():
        step(mix_a, mix_b)

    @pl.when(g % 2 == 1)
    def _():
        step(mix_b, mix_a)


def _outproj_finish(mix_ref, x_ref, mod_ref, gpost_ref, gpre_ref, wr_ref, br_ref,
                    h_ref, f_ref, eid_ref, ew_ref):
    state = {}

    def residual():
        h1 = x_ref[0] + _rms(mix_ref[...], mod_ref[0][2:3] * gpost_ref[...])
        h_ref[0] = h1
        state["h1"] = h1

    def prenorm():
        mod = mod_ref[0]
        f = _rms(state["h1"], gpre_ref[...] * (1.0 + mod[4:5])) + mod[3:4]
        f_ref[...] = f
        state["f"] = f

    def logits():
        f_hi, f_lo = _split_bf16(state["f"])
        tm = f_hi.shape[0]
        r = jnp.dot(jnp.concatenate([f_hi, f_lo], axis=0), wr_ref[...],
                    preferred_element_type=F32)
        lg = (r[:tm, :LANES] + r[:tm, LANES:]) + (r[tm:, :LANES] + r[tm:, LANES:])
        state["logits"] = lg.T[:ROUTER_ROWS] + br_ref[...]

    def route():
        _route(state["logits"], eid_ref, ew_ref)

    return [residual, prenorm, logits, route]


def _route(logits, eid_ref, ew_ref):
    gl = [logits[g:g + 1] for g in range(N_GROUPS)]
    gmax = functools.reduce(jnp.maximum, gl)
    gidx = jnp.full(gmax.shape, N_GROUPS - 1, I32)
    for g in range(N_GROUPS - 2, -1, -1):
        gidx = jnp.where(gl[g] == gmax, g, gidx)
    gval = 1.0 / functools.reduce(lambda a, b: a + b, [jnp.exp(v - gmax) for v in gl])

    es = []
    for e in range(EXPERTS_PER_GROUP):
        r = N_GROUPS + (N_GROUPS - 1) * EXPERTS_PER_GROUP + e
        v = logits[r:r + 1]
        for g in range(N_GROUPS - 2, -1, -1):
            r = N_GROUPS + g * EXPERTS_PER_GROUP + e
            v = jnp.where(gidx == g, logits[r:r + 1], v)
        es.append(v)
    m1 = functools.reduce(jnp.maximum, es)
    i1 = jnp.full(m1.shape, EXPERTS_PER_GROUP - 1, I32)
    for e in range(EXPERTS_PER_GROUP - 2, -1, -1):
        i1 = jnp.where(es[e] == m1, e, i1)
    rest = [jnp.where(i1 == e, -jnp.inf, es[e]) for e in range(EXPERTS_PER_GROUP)]
    m2 = functools.reduce(jnp.maximum, rest)
    i2 = jnp.full(m2.shape, EXPERTS_PER_GROUP - 1, I32)
    for e in range(EXPERTS_PER_GROUP - 2, -1, -1):
        i2 = jnp.where(rest[e] == m2, e, i2)
    p2 = jnp.exp(m2 - m1)
    w1 = gval / (1.0 + p2)
    w2 = gval * p2 / (1.0 + p2)
    eid_ref[0:1, :] = gidx * EXPERTS_PER_GROUP + i1
    eid_ref[1:2, :] = gidx * EXPERTS_PER_GROUP + i2
    ew_ref[0:1, :] = w1
    ew_ref[1:2, :] = w2


def _outproj(ycat, w_out_bf16, x, mod, gpost, gpre, wr, br):
    bsz, s, d = x.shape
    t = bsz * s
    tm = min(256, s)
    nti = s // tm
    n = bsz * nti

    def proj(g):
        gg = jnp.minimum(g, n - 1)
        return gg // nti, gg % nti

    def fin(g):
        gg = jnp.maximum(g - 1, 0)
        return gg // nti, gg % nti

    return pl.pallas_call(
        _outproj_kernel,
        grid=(n + 1,),
        in_specs=[pl.BlockSpec((1, tm, d), lambda g: proj(g) + (0,)),
                  pl.BlockSpec((d, d), lambda g: (0, 0)),
                  pl.BlockSpec((1, tm, d), lambda g: fin(g) + (0,)),
                  pl.BlockSpec((1, N_MOD, d), lambda g: (fin(g)[0], 0, 0)),
                  pl.BlockSpec((1, d), lambda g: (0, 0)),
                  pl.BlockSpec((1, d), lambda g: (0, 0)),
                  pl.BlockSpec((d, LANES), lambda g: (0, 0)),
                  pl.BlockSpec((ROUTER_ROWS, 1), lambda g: (0, 0))],
        out_specs=[pl.BlockSpec((1, tm, d), lambda g: fin(g) + (0,)),
                   pl.BlockSpec((tm, d), lambda g: (jnp.maximum(g - 1, 0), 0)),
                   pl.BlockSpec((TOP_K, tm), lambda g: (0, jnp.maximum(g - 1, 0))),
                   pl.BlockSpec((TOP_K, tm), lambda g: (0, jnp.maximum(g - 1, 0)))],
        out_shape=[jax.ShapeDtypeStruct((bsz, s, d), F32),
                   jax.ShapeDtypeStruct((t, d), F32),
                   jax.ShapeDtypeStruct((TOP_K, t), I32),
                   jax.ShapeDtypeStruct((TOP_K, t), F32)],
        scratch_shapes=[pltpu.VMEM((tm, d), F32), pltpu.VMEM((tm, d), F32),
                        pltpu.VMEM((d, 2 * LANES), BF16)],
        compiler_params=pltpu.CompilerParams(dimension_semantics=("arbitrary",),
                                             vmem_limit_bytes=VMEM_LIMIT),
        name="outproj",
    )(ycat, w_out_bf16, x, mod, gpost.reshape(1, d), gpre.reshape(1, d), wr, br)


def _slots_kernel(eid_ref, pos_ref, tab_ref, rank_scr, *, n_slots):
    t = eid_ref.shape[1]
    chunk = min(512, t)
    nchunk = t // chunk
    tri = jnp.where(lax.broadcasted_iota(I32, (chunk, chunk), 0)
                    <= lax.broadcasted_iota(I32, (chunk, chunk), 1), 1.0, 0.0).astype(BF16)
    e_io = lax.broadcasted_iota(I32, (N_EXPERTS, chunk), 0)

    cnt = jnp.zeros((N_EXPERTS, 1), F32)
    for k in range(TOP_K):
        def rank_body(c, carry, k=k):
            off = pl.multiple_of(c * chunk, chunk)
            onehot = e_io == eid_ref[pl.ds(k, 1), pl.ds(off, chunk)]
            ones = jnp.where(onehot, 1.0, 0.0)
            prefix = jnp.dot(ones.astype(BF16), tri, preferred_element_type=F32) + carry
            rank = jnp.sum(jnp.where(onehot, prefix, 0.0), axis=0, keepdims=True) - 1.0
            rank_scr[pl.ds(k, 1), pl.ds(off, chunk)] = rank
            return carry + jnp.sum(ones, axis=1, keepdims=True)
        cnt = lax.fori_loop(0, nchunk, rank_body, cnt)

    padded = jnp.floor((cnt + (SLOT_TILE - 1)) * (1.0 / SLOT_TILE)) * SLOT_TILE
    sub = lax.broadcasted_iota(I32, (N_EXPERTS, LANES), 0)
    lan = lax.broadcasted_iota(I32, (N_EXPERTS, LANES), 1)
    padded_row = jnp.sum(jnp.where(sub == lan, padded, 0.0), axis=0, keepdims=True)
    start = jnp.sum(jnp.where(lan < sub, padded_row, 0.0), axis=1, keepdims=True)
    end = start + padded
    pad_lo = start + cnt

    for k in range(TOP_K):
        def pos_body(c, carry, k=k):
            off = pl.multiple_of(c * chunk, chunk)
            onehot = e_io == eid_ref[pl.ds(k, 1), pl.ds(off, chunk)]
            base = jnp.sum(jnp.where(onehot, start, 0.0), axis=0, keepdims=True)
            pos_ref[pl.ds(k, 1), pl.ds(off, chunk)] = (
                base + rank_scr[pl.ds(k, 1), pl.ds(off, chunk)]).astype(I32)
            return carry
        lax.fori_loop(0, nchunk, pos_body, 0)

    lo_row = jnp.sum(jnp.where(sub == lan, pad_lo, 0.0), axis=0, keepdims=True)
    hi_row = jnp.sum(jnp.where(sub + PAD_HI_LANE == lan, end, 0.0), axis=0, keepdims=True)
    first_row = jnp.sum(jnp.where(sub + TILE_START_LANE == lan, start * (1.0 / SLOT_TILE), 0.0),
                        axis=0, keepdims=True)
    lane_row = lax.broadcasted_iota(I32, (1, LANES), 1)
    total = jnp.sum(padded_row, axis=1, keepdims=True)
    tail = (jnp.where(lane_row == N_EXPERTS, total, 0.0)
            + jnp.where(lane_row == PAD_HI_LANE + N_EXPERTS, float(n_slots), 0.0)
            + jnp.where(lane_row == TILE_START_LANE + N_EXPERTS, total * (1.0 / SLOT_TILE), 0.0))
    tab_ref[...] = (lo_row + hi_row + first_row + tail).astype(I32)


def _slots(eid, n_tiles):
    t = eid.shape[1]
    return pl.pallas_call(
        functools.partial(_slots_kernel, n_slots=n_tiles * SLOT_TILE),
        out_shape=[jax.ShapeDtypeStruct((TOP_K, t), I32),
                   jax.ShapeDtypeStruct((1, LANES), I32)],
        scratch_shapes=[pltpu.VMEM((TOP_K, t), F32)],
        compiler_params=pltpu.CompilerParams(vmem_limit_bytes=VMEM_LIMIT),
        name="slots",
    )(eid)


def _invert_kernel(pos_ref, tab_ref, inv_ref, *, n_pairs, n_slots):
    group = 8
    spare = n_pairs
    for e in range(N_EXPERTS + 1):
        lo, hi = tab_ref[e], tab_ref[PAD_HI_LANE + e]

        def fill(i, c, lo=lo, spare=spare):
            for k in range(group):
                inv_ref[SLOT_TILE + lo + i * group + k] = spare + i * group + k
            return c
        lax.fori_loop(0, (hi - lo + group - 1) // group, fill, 0)
        spare = spare + hi - lo

    def guard(r, c):
        inv_ref[r] = n_slots + r
        inv_ref[SLOT_TILE + n_slots + r] = n_slots + r
        return c
    lax.fori_loop(0, SLOT_TILE, guard, 0, unroll=8)

    def place(n, c):
        inv_ref[SLOT_TILE + pos_ref[n]] = n
        return c
    lax.fori_loop(0, n_pairs, place, 0, unroll=16)


def _invert(pos_flat, tab, n_tiles):
    n_pairs = pos_flat.shape[0]
    n_slots = n_tiles * SLOT_TILE
    return pl.pallas_call(
        functools.partial(_invert_kernel, n_pairs=n_pairs, n_slots=n_slots),
        in_specs=[pl.BlockSpec(memory_space=pltpu.SMEM), pl.BlockSpec(memory_space=pltpu.SMEM)],
        out_specs=pl.BlockSpec(memory_space=pltpu.SMEM),
        out_shape=jax.ShapeDtypeStruct((n_slots + 2 * SLOT_TILE,), I32),
        name="invert",
    )(pos_flat, tab)


def _experts_kernel(inv_ref, tab_ref, f_ref, wg_ref, wu_ref, wd_ref, y_ref,
                    wbg, wbu, wbd, xa, xb, ya, yb, zrow, gsem, ssem, zsem, *, n_tokens, n_slots):
    e = pl.program_id(0)
    j = pl.program_id(1)
    fc = wg_ref.shape[2]
    total = tab_ref[TILE_START_LANE + N_EXPERTS]

    def gather_copy(tile, r, xbuf):
        v = inv_ref[(tile + 1) * SLOT_TILE + r]
        return pltpu.make_async_copy(f_ref.at[pl.ds(v & (n_tokens - 1), 1)], xbuf.at[pl.ds(r, 1)], gsem)

    def scatter_copy(tile, r, ybuf):
        v = inv_ref[(tile + 1) * SLOT_TILE + r]
        return pltpu.make_async_copy(ybuf.at[pl.ds(r, 1)], y_ref.at[pl.ds(v, 1)], ssem)

    def gather_wait(xbuf):
        pltpu.make_async_copy(f_ref.at[pl.ds(0, SLOT_TILE)], xbuf, gsem).wait()

    def scatter_wait(ybuf):
        pltpu.make_async_copy(ybuf, y_ref.at[pl.ds(0, SLOT_TILE)], ssem).wait()

    def rows(fn):
        def body(r, c):
            fn(r)
            return c
        lax.fori_loop(0, SLOT_TILE, body, 0, unroll=8)

    @pl.when((e == 0) & (j == 0))
    def _():
        ya[...] = jnp.zeros(ya.shape, ya.dtype)
        yb[...] = jnp.zeros(yb.shape, yb.dtype)
        zrow[...] = jnp.zeros(zrow.shape, zrow.dtype)
        rows(lambda r: gather_copy(0, r, xa).start())

        def zero_copy(r):
            return pltpu.make_async_copy(zrow.at[pl.ds(0, 1)], y_ref.at[pl.ds(r, 1)], zsem)

        def zstart(r, c):
            zero_copy(r).start()
            return c

        def zwait(r, c):
            zero_copy(r).wait()
            return c
        lax.fori_loop(total * SLOT_TILE, n_slots, zstart, 0)
        lax.fori_loop(total * SLOT_TILE, n_slots, zwait, 0)

    @pl.when(e < N_EXPERTS)
    def _():
        slot = e % 2
        col = pl.multiple_of(j * fc, fc)
        wbg[slot, :, pl.ds(col, fc)] = wg_ref[0].astype(BF16)
        wbu[slot, :, pl.ds(col, fc)] = wu_ref[0].astype(BF16)
        wbd[slot, pl.ds(col, fc), :] = wd_ref[0].astype(BF16)

    prev = jnp.maximum(e - 1, 0)
    first = tab_ref[TILE_START_LANE + prev]
    count = jnp.where(e == 0, 0, tab_ref[TILE_START_LANE + prev + 1] - first)
    lo = first + (count * j) // N_WCHUNK
    hi = first + (count * (j + 1)) // N_WCHUNK
    wslot = prev % 2

    def tile(t, c):
        def run(xcur, xnext, ycur, yprev, ws):
            gather_wait(xcur)
            for r in range(SLOT_TILE):
                gather_copy(t + 1, r, xnext).start()
            for r in range(SLOT_TILE):
                scatter_copy(t - 1, r, yprev).start(priority=1)
            x = xcur[...].astype(BF16)
            hg = jnp.dot(x, wbg[ws], preferred_element_type=F32)
            hu = jnp.dot(x, wbu[ws], preferred_element_type=F32)
            hid = (_silu(hg) * hu).astype(BF16)
            ycur[...] = jnp.dot(hid, wbd[ws], preferred_element_type=F32)
            scatter_wait(yprev)

        for ws in range(2):
            for parity, bufs in enumerate(((xa, xb, ya, yb), (xb, xa, yb, ya))):
                @pl.when((wslot == ws) & (t % 2 == parity))
                def _(bufs=bufs, ws=ws):
                    run(*bufs, ws)
        return c
    lax.fori_loop(lo, hi, tile, 0)

    @pl.when((e == N_EXPERTS) & (j == N_WCHUNK - 1))
    def _():
        last = total - 1

        def finish(xnext, ylast):
            gather_wait(xnext)
            rows(lambda r: scatter_copy(last, r, ylast).start())
            scatter_wait(ylast)

        @pl.when(last % 2 == 0)
        def _():
            finish(xb, ya)

        @pl.when(last % 2 == 1)
        def _():
            finish(xa, yb)


def _experts(inv, tab, f, wg, wu, wd, n_tiles):
    t, d = f.shape
    assert t & (t - 1) == 0
    fdim = wg.shape[2]
    fc = fdim // N_WCHUNK
    n_slots = n_tiles * SLOT_TILE

    def expert(e):
        return jnp.minimum(e, N_EXPERTS - 1)

    def chunk(e, j):
        return jnp.where(e == N_EXPERTS, N_WCHUNK - 1, j)

    return pl.pallas_call(
        functools.partial(_experts_kernel, n_tokens=t, n_slots=n_slots),
        grid_spec=pltpu.PrefetchScalarGridSpec(
            num_scalar_prefetch=2,
            grid=(N_EXPERTS + 1, N_WCHUNK),
            in_specs=[pl.BlockSpec(memory_space=pl.ANY),
                      pl.BlockSpec((1, d, fc), lambda e, j, inv, tab: (expert(e), 0, chunk(e, j))),
                      pl.BlockSpec((1, d, fc), lambda e, j, inv, tab: (expert(e), 0, chunk(e, j))),
                      pl.BlockSpec((1, fc, d), lambda e, j, inv, tab: (expert(e), chunk(e, j), 0))],
            out_specs=pl.BlockSpec(memory_space=pl.ANY),
            scratch_shapes=[pltpu.VMEM((2, d, fdim), BF16), pltpu.VMEM((2, d, fdim), BF16),
                            pltpu.VMEM((2, fdim, d), BF16),
                            pltpu.VMEM((SLOT_TILE, d), F32), pltpu.VMEM((SLOT_TILE, d), F32),
                            pltpu.VMEM((SLOT_TILE, d), F32), pltpu.VMEM((SLOT_TILE, d), F32),
                            pltpu.VMEM((8, d), F32),
                            pltpu.SemaphoreType.DMA, pltpu.SemaphoreType.DMA, pltpu.SemaphoreType.DMA]),
        out_shape=jax.ShapeDtypeStruct((n_slots + SLOT_TILE, d), F32),
        compiler_params=pltpu.CompilerParams(dimension_semantics=("arbitrary", "arbitrary"),
                                             vmem_limit_bytes=EXPERTS_VMEM_LIMIT),
        name="experts",
    )(inv, tab, f, wg, wu, wd)


def _final_kernel(y0_ref, y1_ref, h_ref, ew_ref, mod_ref, g_ref, o_ref):
    ew = ew_ref[...]
    y = ew[:, 0:1] * y0_ref[...] + ew[:, 1:2] * y1_ref[...]
    o_ref[0] = h_ref[0] + _rms(y, mod_ref[0][5:6] * g_ref[...])


def _final(y_pairs, h1, ew_t, mod, g):
    bsz, s, d = h1.shape
    t = bsz * s
    tg = min(512, s)
    nti = s // tg
    return pl.pallas_call(
        _final_kernel,
        grid=(bsz, nti),
        in_specs=[pl.BlockSpec((tg, d), lambda b, i: (b * nti + i, 0)),
                  pl.BlockSpec((tg, d), lambda b, i: (t // tg + b * nti + i, 0)),
                  pl.BlockSpec((1, tg, d), lambda b, i: (b, i, 0)),
                  pl.BlockSpec((tg, TOP_K), lambda b, i: (b * nti + i, 0)),
                  pl.BlockSpec((1, N_MOD, d), lambda b, i: (b, 0, 0)),
                  pl.BlockSpec((1, d), lambda b, i: (0, 0))],
        out_specs=pl.BlockSpec((1, tg, d), lambda b, i: (b, i, 0)),
        out_shape=jax.ShapeDtypeStruct((bsz, s, d), F32),
        compiler_params=pltpu.CompilerParams(dimension_semantics=("arbitrary", "arbitrary"),
                                             vmem_limit_bytes=VMEM_LIMIT),
        name="final",
    )(y_pairs, y_pairs, h1, ew_t, mod, g.reshape(1, d))


def kernel(x, c, positions, w_ada, b_ada, g_mix_pre, g_mix_post, w_in, sink_logits, gmlp_norm_g,
           w_spatial, b_spatial, g_group_out, w_out, g_ffn_pre, g_ffn_post, w_router_group,
           b_router_group, w_router_expert, b_router_expert, w_gate, w_up, w_down):
    bsz, s, d = x.shape
    t = bsz * s
    depth = w_ada.shape[0]
    fdim = w_gate.shape[-1]
    n_tiles = (TOP_K * t) // SLOT_TILE + N_EXPERTS
    n_slots = n_tiles * SLOT_TILE
    cos, sin = _rope_tables(positions)
    h = x
    for l in range(depth):
        mod = _ada(c, w_ada[l], b_ada[l]).reshape(bsz, N_MOD, d)
        z = _inproj(h, mod, g_mix_pre[l], w_in[l].astype(BF16))
        ycat = _mixer(z, cos, sin, sink_logits[l], gmlp_norm_g[l], w_spatial[l],
                      b_spatial[l], g_group_out[l])
        wr = jnp.concatenate([w_router_group[l], w_router_expert[l].reshape(d, N_EXPERTS)], axis=1)
        wr = jnp.pad(wr, ((0, 0), (0, LANES - N_GROUPS - N_EXPERTS)))
        br = jnp.concatenate([b_router_group[l], b_router_expert[l].reshape(N_EXPERTS)])
        br = jnp.pad(br, (0, ROUTER_ROWS - N_GROUPS - N_EXPERTS)).reshape(ROUTER_ROWS, 1)
        h1, f, eid, ew = _outproj(ycat, w_out[l].astype(BF16), h, mod, g_mix_post[l], g_ffn_pre[l],
                                  wr, br)
        pos, tab = _slots(eid, n_tiles)
        tab = tab.reshape(LANES)
        inv = _invert(pos.reshape(TOP_K * t), tab, n_tiles)
        y_pairs = _experts(inv, tab, f,
                           w_gate[l].reshape(N_EXPERTS, d, fdim),
                           w_up[l].reshape(N_EXPERTS, d, fdim),
                           w_down[l].reshape(N_EXPERTS, fdim, d), n_tiles)
        h = _final(y_pairs, h1, ew.T, mod, g_ffn_post[l])
    return h
```

```python
import functools

import jax
import jax.numpy as jnp
from jax import lax
from jax.experimental import pallas as pl
from jax.experimental.pallas import tpu as pltpu

F32 = jnp.float32
BF16 = jnp.bfloat16
I32 = jnp.int32

HEAD_DIM = 128
ATTN_HEADS = 8
KV_HEADS = 2
Q_PER_KV = ATTN_HEADS // KV_HEADS
GMLP_HEADS = 8
WINDOW = 128
BLOCK = 128
ROPE_THETA = 500000.0
ROPE_DIM = HEAD_DIM // 4
ROPE_HALF = ROPE_DIM // 2
N_GROUPS = 4
EXPERTS_PER_GROUP = 4
N_EXPERTS = N_GROUPS * EXPERTS_PER_GROUP
TOP_K = 2
N_MOD = 6
EPS = 1e-6
NEG_INF = -1e30
LOG2_E = 1.4426950408889634
LANES = 128
BF16_SUBLANES = 16
PAD_HI_LANE = 32
TILE_START_LANE = 64
N_WCHUNK = 4
EXPERTS_VMEM_LIMIT = 58 * 1024 * 1024
ROUTER_ROWS = 32

SLOT_TILE = 256
SLOT_TILE_LOG2 = 8
VMEM_LIMIT = 52 * 1024 * 1024


def _rms(x, g):
    ms = jnp.mean(x * x, axis=-1, keepdims=True)
    return x * lax.rsqrt(ms + EPS) * g


def _silu(x):
    return x / (1.0 + jnp.exp(-x))


def _gelu(x):
    return 0.5 * x * (1.0 + lax.erf(x * 0.7071067811865476))


def _ada_kernel(c_ref, w_ref, b_ref, o_ref):
    ca = _silu(c_ref[...]).astype(BF16)
    o_ref[...] = jnp.dot(ca, w_ref[...].astype(BF16), preferred_element_type=F32) + b_ref[...]


def _ada(c, w, b):
    bsz, d = c.shape
    n = w.shape[1]
    tn = 1024
    return pl.pallas_call(
        _ada_kernel,
        grid=(n // tn,),
        in_specs=[pl.BlockSpec((bsz, d), lambda j: (0, 0)),
                  pl.BlockSpec((d, tn), lambda j: (0, j)),
                  pl.BlockSpec((1, tn), lambda j: (0, j))],
        out_specs=pl.BlockSpec((bsz, tn), lambda j: (0, j)),
        out_shape=jax.ShapeDtypeStruct((bsz, n), F32),
        compiler_params=pltpu.CompilerParams(dimension_semantics=("arbitrary",),
                                             vmem_limit_bytes=VMEM_LIMIT),
        name="ada",
    )(c, w, b.reshape(1, n))


def _inproj_kernel(x_ref, mod_ref, g_ref, w_ref, o_ref, a_a, a_b, *, rows_per_step):
    g = pl.program_id(0)
    j = pl.program_id(1)
    tm = a_a.shape[0]

    @pl.when((g == 0) & (j == 0))
    def _():
        a_b[...] = jnp.zeros(a_b.shape, a_b.dtype)

    def step(a_w, a_r):
        o_ref[0] = jnp.dot(a_r[...], w_ref[...], preferred_element_type=F32).astype(BF16)
        start = pl.multiple_of(jnp.minimum(j * rows_per_step, tm - rows_per_step), BF16_SUBLANES)
        mod = mod_ref[0]
        a = _rms(x_ref[0, pl.ds(start, rows_per_step), :], g_ref[...] * (1.0 + mod[1:2])) + mod[0:1]
        a_w[pl.ds(start, rows_per_step), :] = a.astype(BF16)

    @pl.when(g % 2 == 0)
    def _():
        step(a_a, a_b)

    @pl.when(g % 2 == 1)
    def _():
        step(a_b, a_a)


def _inproj(x, mod, g, w_bf16):
    bsz, s, d = x.shape
    n = w_bf16.shape[1]
    tm = min(1024, s)
    tn = 512
    nj = n // tn
    nti = s // tm
    ntile = bsz * nti
    rows_per_step = -(-tm // (nj * BF16_SUBLANES)) * BF16_SUBLANES

    def norm_tile(g):
        gg = jnp.minimum(g, ntile - 1)
        return gg // nti, gg % nti

    def mm_tile(g):
        gg = jnp.maximum(g - 1, 0)
        return gg // nti, gg % nti

    return pl.pallas_call(
        functools.partial(_inproj_kernel, rows_per_step=rows_per_step),
        grid=(ntile + 1, nj),
        in_specs=[pl.BlockSpec((1, tm, d), lambda g, j: norm_tile(g) + (0,)),
                  pl.BlockSpec((1, N_MOD, d), lambda g, j: (norm_tile(g)[0], 0, 0)),
                  pl.BlockSpec((1, d), lambda g, j: (0, 0)),
                  pl.BlockSpec((d, tn), lambda g, j: (0, j))],
        out_specs=pl.BlockSpec((1, tm, tn), lambda g, j: mm_tile(g) + (jnp.where(g == 0, 0, j),)),
        out_shape=jax.ShapeDtypeStruct((bsz, s, n), BF16),
        scratch_shapes=[pltpu.VMEM((tm, d), BF16), pltpu.VMEM((tm, d), BF16)],
        compiler_params=pltpu.CompilerParams(dimension_semantics=("arbitrary", "arbitrary"),
                                             vmem_limit_bytes=VMEM_LIMIT),
        name="inproj",
    )(x, mod, g.reshape(1, d), w_bf16)


def _angles_kernel(pos_ref, invf_ref, cos_ref, sin_ref):
    rows = pos_ref.shape[0]
    per_row = LANES // ROPE_HALF
    ang = pos_ref[...].astype(F32) * invf_ref[...]
    src = lax.broadcasted_iota(I32, (LANES, LANES), 0)
    dst = lax.broadcasted_iota(I32, (LANES, LANES), 1)
    rotary_lane = lax.broadcasted_iota(I32, (1, LANES), 1) < ROPE_DIM
    for out_ref, val, fill in ((cos_ref, jnp.cos(ang), 1.0), (sin_ref, jnp.sin(ang), 0.0)):
        p0 = val.astype(BF16)
        r0 = val - p0.astype(F32)
        p1 = r0.astype(BF16)
        p2 = (r0 - p1.astype(F32)).astype(BF16)
        rest = jnp.where(rotary_lane, 0.0, fill)
        for a in range(per_row):
            sel = jnp.where((dst < ROPE_DIM) & ((dst & (ROPE_HALF - 1)) + a * ROPE_HALF == src),
                            1.0, 0.0).astype(BF16)
            spread = (jnp.dot(p0, sel, preferred_element_type=F32)
                      + jnp.dot(p1, sel, preferred_element_type=F32)
                      + jnp.dot(p2, sel, preferred_element_type=F32))
            out_ref[pl.ds(a, rows, stride=per_row), :] = spread + rest


def _rope_tables(positions):
    bsz, s = positions.shape
    per_row = LANES // ROPE_HALF
    rows = bsz * s // per_row
    inv = ROPE_THETA ** (-jnp.arange(ROPE_HALF, dtype=F32) * 2.0 / ROPE_DIM)
    invf = jnp.tile(inv, per_row).reshape(1, LANES)
    pos_rep = jnp.repeat(positions.reshape(rows, per_row), ROPE_HALF, axis=1)
    cos, sin = pl.pallas_call(
        _angles_kernel,
        out_shape=[jax.ShapeDtypeStruct((bsz * s, LANES), F32)] * 2,
        name="angles",
    )(pos_rep, invf)
    return cos.reshape(bsz, s, LANES), sin.reshape(bsz, s, LANES)


def _lane_mean(x):
    k = x.shape[1]
    ones = jnp.full((k, LANES), 1.0 / k, BF16)
    return jnp.dot(x.astype(BF16), ones, preferred_element_type=F32)


def _rope(x, cos, sin):
    lane = lax.broadcasted_iota(I32, x.shape, 1)
    partner = jnp.where(lane < ROPE_HALF,
                        pltpu.roll(x, HEAD_DIM - ROPE_HALF, 1),
                        pltpu.roll(x, ROPE_HALF, 1))
    return x * cos + partner * sin


def _mixer_kernel(sink_ref, q_ref, kvp_ref, kvc_ref, kvn_ref, u0_ref, u1_ref, v0_ref, v1_ref,
                  cosp_ref, cosc_ref, cosn_ref, sinp_ref, sinc_ref, sinn_ref,
                  gn_ref, ws_ref, bst_ref, gg_ref, o_ref, ya_scr, yg_scr):
    tq = q_ref.shape[1]
    nsub = tq // BLOCK
    aw = ATTN_HEADS * HEAD_DIM
    i = pl.program_id(1)
    sign = jnp.where(lax.broadcasted_iota(I32, (1, HEAD_DIM), 1) < ROPE_HALF, -1.0, 1.0)

    cos_c, sin_c = cosc_ref[0], sinc_ref[0] * sign
    tabs = ((cosp_ref[0], sinp_ref[0] * sign), (cos_c, sin_c), (cosn_ref[0], sinn_ref[0] * sign))
    scale = HEAD_DIM ** -0.5 * LOG2_E
    cos_q, sin_q = cos_c * scale, sin_c * scale

    def prepare_kv_head(h):
        kparts, vparts = [], []
        for ref, (cs, sn) in zip((kvp_ref, kvc_ref, kvn_ref), tabs):
            k = ref[0, :, h * HEAD_DIM:(h + 1) * HEAD_DIM].astype(F32)
            kparts.append(_rope(k, cs, sn).astype(BF16))
            vparts.append(ref[0, :, (KV_HEADS + h) * HEAD_DIM:(KV_HEADS + h + 1) * HEAD_DIM])
        kband = jnp.concatenate(kparts, axis=0)
        vband = jnp.concatenate(
            [jnp.concatenate(vparts, axis=0), jnp.ones((tq + 2 * BLOCK, HEAD_DIM), BF16)], axis=1)
        qs = []
        for g in range(Q_PER_KV):
            hq = h * Q_PER_KV + g
            q = q_ref[0, :, hq * HEAD_DIM:(hq + 1) * HEAD_DIM].astype(F32)
            qs.append(_rope(q, cos_q, sin_q).astype(BF16))
        return kband, vband, qs

    rows = Q_PER_KV * BLOCK
    band = 3 * BLOCK
    q_off = lax.broadcasted_iota(I32, (rows, BLOCK), 0) & (BLOCK - 1)
    k_off = lax.broadcasted_iota(I32, (rows, BLOCK), 1)
    bias_prev = jnp.where(k_off >= q_off, 0.0, NEG_INF)
    bias_next = jnp.where(k_off <= q_off, 0.0, NEG_INF)
    first_tile = i == 0
    last_tile = i == pl.num_programs(1) - 1
    def attention_block(s, h, kband, vband, qs):
        bp = jnp.where(first_tile, NEG_INF, bias_prev) if s == 0 else bias_prev
        bn = jnp.where(last_tile, NEG_INF, bias_next) if s == nsub - 1 else bias_next
        q4 = jnp.concatenate([q[s * BLOCK:(s + 1) * BLOCK] for q in qs], axis=0)
        kb = kband[s * BLOCK:s * BLOCK + band]
        vb = vband[s * BLOCK:s * BLOCK + band]
        sc = lax.dot_general(q4, kb, (((1,), (1,)), ((), ())), preferred_element_type=F32)
        sc = jnp.concatenate([sc[:, :BLOCK] + bp, sc[:, BLOCK:2 * BLOCK], sc[:, 2 * BLOCK:] + bn],
                             axis=1)
        sink = jnp.concatenate([jnp.full((BLOCK, 1), sink_ref[h * Q_PER_KV + g] * LOG2_E, F32)
                                for g in range(Q_PER_KV)], axis=0)
        m = jnp.maximum(jnp.max(sc, axis=-1, keepdims=True), sink)
        p = jnp.exp2(sc - m)
        pv = jnp.dot(p.astype(BF16), vb, preferred_element_type=F32)
        den = pv[:, HEAD_DIM:] + jnp.exp2(sink - m)
        o = pv[:, :HEAD_DIM] / den
        for g in range(Q_PER_KV):
            hq = h * Q_PER_KV + g
            ya_scr[s * BLOCK:(s + 1) * BLOCK, hq * HEAD_DIM:(hq + 1) * HEAD_DIM] = (
                o[g * BLOCK:(g + 1) * BLOCK])

    half_heads = GMLP_HEADS // 2

    def gmlp_head(h):
        u_ref = u0_ref if h < half_heads else u1_ref
        v_ref = v0_ref if h < half_heads else v1_ref
        hh = h % half_heads
        u = _gelu(u_ref[0, :, hh * HEAD_DIM:(hh + 1) * HEAD_DIM].astype(F32))
        v = _gelu(v_ref[0, :, hh * HEAD_DIM:(hh + 1) * HEAD_DIM].astype(F32))
        mu = _lane_mean(v)
        dv = v - mu
        var = _lane_mean(dv * dv)
        vn = (dv * lax.rsqrt(var + EPS) * gn_ref[h:h + 1, :]).astype(BF16)
        w = ws_ref[h].astype(BF16)
        bias = bst_ref[:, h:h + 1]
        for cidx in range(nsub):
            sl = slice(cidx * BLOCK, (cidx + 1) * BLOCK)
            mixed = jnp.dot(w, vn[sl], preferred_element_type=F32) + bias
            yg_scr[sl, h * HEAD_DIM:(h + 1) * HEAD_DIM] = u[sl] * mixed

    heads_per_block = -(-GMLP_HEADS // (KV_HEADS * nsub))
    next_head = 0
    for h in range(KV_HEADS):
        prepared = prepare_kv_head(h)
        for s in range(nsub):
            attention_block(s, h, *prepared)
            for _ in range(heads_per_block):
                if next_head < GMLP_HEADS:
                    gmlp_head(next_head)
                    next_head += 1
    for h in range(next_head, GMLP_HEADS):
        gmlp_head(h)

    for scr, lo in ((ya_scr, 0), (yg_scr, aw)):
        y = scr[...]
        width = y.shape[1]
        inv_rms = lax.rsqrt(_lane_mean(y * y) + EPS)
        inv_rms = jnp.concatenate([inv_rms] * (width // LANES), axis=1)
        o_ref[0, :, lo:lo + width] = (y * inv_rms * gg_ref[:, lo:lo + width]).astype(BF16)


def _mixer(z, cos, sin, sink, gn, ws, bs, gg):
    assert WINDOW == BLOCK
    bsz, s, _ = z.shape
    tq = min(1024, s)
    nsub = tq // BLOCK
    nblk = s // BLOCK
    aw = ATTN_HEADS * HEAD_DIM
    gw = GMLP_HEADS * HEAD_DIM
    cw = 512

    def prev_blk(b, i):
        return (b, jnp.maximum(i * nsub - 1, 0), 2)

    def next_blk(b, i):
        return (b, jnp.minimum((i + 1) * nsub, nblk - 1), 2)

    tab_specs = [pl.BlockSpec((1, BLOCK, HEAD_DIM), lambda b, i: prev_blk(b, i)[:2] + (0,)),
                 pl.BlockSpec((1, tq, HEAD_DIM), lambda b, i: (b, i, 0)),
                 pl.BlockSpec((1, BLOCK, HEAD_DIM), lambda b, i: next_blk(b, i)[:2] + (0,))]
    return pl.pallas_call(
        _mixer_kernel,
        grid=(bsz, s // tq),
        in_specs=[pl.BlockSpec(memory_space=pltpu.SMEM),
                  pl.BlockSpec((1, tq, aw), lambda b, i: (b, i, 0)),
                  pl.BlockSpec((1, BLOCK, cw), prev_blk),
                  pl.BlockSpec((1, tq, cw), lambda b, i: (b, i, 2)),
                  pl.BlockSpec((1, BLOCK, cw), next_blk),
                  pl.BlockSpec((1, tq, cw), lambda b, i: (b, i, 3)),
                  pl.BlockSpec((1, tq, cw), lambda b, i: (b, i, 4)),
                  pl.BlockSpec((1, tq, cw), lambda b, i: (b, i, 5)),
                  pl.BlockSpec((1, tq, cw), lambda b, i: (b, i, 6)),
                  *tab_specs, *tab_specs,
                  pl.BlockSpec((GMLP_HEADS, HEAD_DIM), lambda b, i: (0, 0)),
                  pl.BlockSpec((GMLP_HEADS, BLOCK, BLOCK), lambda b, i: (0, 0, 0)),
                  pl.BlockSpec((BLOCK, GMLP_HEADS), lambda b, i: (0, 0)),
                  pl.BlockSpec((1, aw + gw), lambda b, i: (0, 0))],
        out_specs=pl.BlockSpec((1, tq, aw + gw), lambda b, i: (b, i, 0)),
        out_shape=jax.ShapeDtypeStruct((bsz, s, aw + gw), BF16),
        scratch_shapes=[pltpu.VMEM((tq, aw), F32), pltpu.VMEM((tq, gw), F32)],
        compiler_params=pltpu.CompilerParams(dimension_semantics=("arbitrary", "arbitrary"),
                                             vmem_limit_bytes=VMEM_LIMIT),
        name="mixer",
    )(sink, z, z, z, z, z, z, z, z, cos, cos, cos, sin, sin, sin, gn, ws, bs.T,
      gg.reshape(1, aw + gw))


def _split_bf16(x):
    hi = x.astype(BF16)
    lo = (x - hi.astype(F32)).astype(BF16)
    return hi, lo


def _outproj_kernel(y_ref, w_ref, x_ref, mod_ref, gpost_ref, gpre_ref, wr_ref, br_ref,
                    h_ref, f_ref, eid_ref, ew_ref, mix_a, mix_b, wr_split):
    g = pl.program_id(0)

    @pl.when(g == 0)
    def _():
        mix_b[...] = jnp.zeros(mix_b.shape, mix_b.dtype)
        w_hi, w_lo = _split_bf16(wr_ref[...])
        wr_split[:, :LANES] = w_hi
        wr_split[:, LANES:] = w_lo

    def step(mix_w, mix_r):
        mix_w[...] = jnp.dot(y_ref[0], w_ref[...], preferred_element_type=F32)
        for stage in _outproj_finish(mix_r, x_ref, mod_ref, gpost_ref, gpre_ref, wr_split, br_ref,
                                     h_ref, f_ref, eid_ref, ew_ref):
            stage()

    @pl.when(g % 2 == 0)
    def _():
        step(mix_a, mix_b)

    @pl.when(g % 2 == 1)
    def _():
        step(mix_b, mix_a)


def _outproj_finish(mix_ref, x_ref, mod_ref, gpost_ref, gpre_ref, wr_ref, br_ref,
                    h_ref, f_ref, eid_ref, ew_ref):
    state = {}

    def residual():
        h1 = x_ref[0] + _rms(mix_ref[...], mod_ref[0][2:3] * gpost_ref[...])
        h_ref[0] = h1
        state["h1"] = h1

    def prenorm():
        mod = mod_ref[0]
        f = _rms(state["h1"], gpre_ref[...] * (1.0 + mod[4:5])) + mod[3:4]
        f_ref[...] = f
        state["f"] = f

    def logits():
        f_hi, f_lo = _split_bf16(state["f"])
        tm = f_hi.shape[0]
        r = jnp.dot(jnp.concatenate([f_hi, f_lo], axis=0), wr_ref[...],
                    preferred_element_type=F32)
        lg = (r[:tm, :LANES] + r[:tm, LANES:]) + (r[tm:, :LANES] + r[tm:, LANES:])
        state["logits"] = lg.T[:ROUTER_ROWS] + br_ref[...]

    def route():
        _route(state["logits"], eid_ref, ew_ref)

    return [residual, prenorm, logits, route]


def _route(logits, eid_ref, ew_ref):
    gl = [logits[g:g + 1] for g in range(N_GROUPS)]
    gmax = functools.reduce(jnp.maximum, gl)
    gidx = jnp.full(gmax.shape, N_GROUPS - 1, I32)
    for g in range(N_GROUPS - 2, -1, -1):
        gidx = jnp.where(gl[g] == gmax, g, gidx)
    gval = 1.0 / functools.reduce(lambda a, b: a + b, [jnp.exp(v - gmax) for v in gl])

    es = []
    for e in range(EXPERTS_PER_GROUP):
        r = N_GROUPS + (N_GROUPS - 1) * EXPERTS_PER_GROUP + e
        v = logits[r:r + 1]
        for g in range(N_GROUPS - 2, -1, -1):
            r = N_GROUPS + g * EXPERTS_PER_GROUP + e
            v = jnp.where(gidx == g, logits[r:r + 1], v)
        es.append(v)
    m1 = functools.reduce(jnp.maximum, es)
    i1 = jnp.full(m1.shape, EXPERTS_PER_GROUP - 1, I32)
    for e in range(EXPERTS_PER_GROUP - 2, -1, -1):
        i1 = jnp.where(es[e] == m1, e, i1)
    rest = [jnp.where(i1 == e, -jnp.inf, es[e]) for e in range(EXPERTS_PER_GROUP)]
    m2 = functools.reduce(jnp.maximum, rest)
    i2 = jnp.full(m2.shape, EXPERTS_PER_GROUP - 1, I32)
    for e in range(EXPERTS_PER_GROUP - 2, -1, -1):
        i2 = jnp.where(rest[e] == m2, e, i2)
    p2 = jnp.exp(m2 - m1)
    w1 = gval / (1.0 + p2)
    w2 = gval * p2 / (1.0 + p2)
    eid_ref[0:1, :] = gidx * EXPERTS_PER_GROUP + i1
    eid_ref[1:2, :] = gidx * EXPERTS_PER_GROUP + i2
    ew_ref[0:1, :] = w1
    ew_ref[1:2, :] = w2


def _outproj(ycat, w_out_bf16, x, mod, gpost, gpre, wr, br):
    bsz, s, d = x.shape
    t = bsz * s
    tm = min(256, s)
    nti = s // tm
    n = bsz * nti

    def proj(g):
        gg = jnp.minimum(g, n - 1)
        return gg // nti, gg % nti

    def fin(g):
        gg = jnp.maximum(g - 1, 0)
        return gg // nti, gg % nti

    return pl.pallas_call(
        _outproj_kernel,
        grid=(n + 1,),
        in_specs=[pl.BlockSpec((1, tm, d), lambda g: proj(g) + (0,)),
                  pl.BlockSpec((d, d), lambda g: (0, 0)),
                  pl.BlockSpec((1, tm, d), lambda g: fin(g) + (0,)),
                  pl.BlockSpec((1, N_MOD, d), lambda g: (fin(g)[0], 0, 0)),
                  pl.BlockSpec((1, d), lambda g: (0, 0)),
                  pl.BlockSpec((1, d), lambda g: (0, 0)),
                  pl.BlockSpec((d, LANES), lambda g: (0, 0)),
                  pl.BlockSpec((ROUTER_ROWS, 1), lambda g: (0, 0))],
        out_specs=[pl.BlockSpec((1, tm, d), lambda g: fin(g) + (0,)),
                   pl.BlockSpec((tm, d), lambda g: (jnp.maximum(g - 1, 0), 0)),
                   pl.BlockSpec((TOP_K, tm), lambda g: (0, jnp.maximum(g - 1, 0))),
                   pl.BlockSpec((TOP_K, tm), lambda g: (0, jnp.maximum(g - 1, 0)))],
        out_shape=[jax.ShapeDtypeStruct((bsz, s, d), F32),
                   jax.ShapeDtypeStruct((t, d), F32),
                   jax.ShapeDtypeStruct((TOP_K, t), I32),
                   jax.ShapeDtypeStruct((TOP_K, t), F32)],
        scratch_shapes=[pltpu.VMEM((tm, d), F32), pltpu.VMEM((tm, d), F32),
                        pltpu.VMEM((d, 2 * LANES), BF16)],
        compiler_params=pltpu.CompilerParams(dimension_semantics=("arbitrary",),
                                             vmem_limit_bytes=VMEM_LIMIT),
        name="outproj",
    )(ycat, w_out_bf16, x, mod, gpost.reshape(1, d), gpre.reshape(1, d), wr, br)


def _slots_kernel(eid_ref, pos_ref, tab_ref, rank_scr, *, n_slots):
    t = eid_ref.shape[1]
    chunk = min(512, t)
    nchunk = t // chunk
    tri = jnp.where(lax.broadcasted_iota(I32, (chunk, chunk), 0)
                    <= lax.broadcasted_iota(I32, (chunk, chunk), 1), 1.0, 0.0).astype(BF16)
    e_io = lax.broadcasted_iota(I32, (N_EXPERTS, chunk), 0)

    cnt = jnp.zeros((N_EXPERTS, 1), F32)
    for k in range(TOP_K):
        def rank_body(c, carry, k=k):
            off = pl.multiple_of(c * chunk, chunk)
            onehot = e_io == eid_ref[pl.ds(k, 1), pl.ds(off, chunk)]
            ones = jnp.where(onehot, 1.0, 0.0)
            prefix = jnp.dot(ones.astype(BF16), tri, preferred_element_type=F32) + carry
            rank = jnp.sum(jnp.where(onehot, prefix, 0.0), axis=0, keepdims=True) - 1.0
            rank_scr[pl.ds(k, 1), pl.ds(off, chunk)] = rank
            return carry + jnp.sum(ones, axis=1, keepdims=True)
        cnt = lax.fori_loop(0, nchunk, rank_body, cnt)

    padded = jnp.floor((cnt + (SLOT_TILE - 1)) * (1.0 / SLOT_TILE)) * SLOT_TILE
    sub = lax.broadcasted_iota(I32, (N_EXPERTS, LANES), 0)
    lan = lax.broadcasted_iota(I32, (N_EXPERTS, LANES), 1)
    padded_row = jnp.sum(jnp.where(sub == lan, padded, 0.0), axis=0, keepdims=True)
    start = jnp.sum(jnp.where(lan < sub, padded_row, 0.0), axis=1, keepdims=True)
    end = start + padded
    pad_lo = start + cnt

    for k in range(TOP_K):
        def pos_body(c, carry, k=k):
            off = pl.multiple_of(c * chunk, chunk)
            onehot = e_io == eid_ref[pl.ds(k, 1), pl.ds(off, chunk)]
            base = jnp.sum(jnp.where(onehot, start + SLOT_TILE, 0.0), axis=0, keepdims=True)
            pos_ref[pl.ds(k, 1), pl.ds(off, chunk)] = (
                base + rank_scr[pl.ds(k, 1), pl.ds(off, chunk)]).astype(I32)
            return carry
        lax.fori_loop(0, nchunk, pos_body, 0)

    lo_row = jnp.sum(jnp.where(sub == lan, pad_lo, 0.0), axis=0, keepdims=True)
    hi_row = jnp.sum(jnp.where(sub + PAD_HI_LANE == lan, end, 0.0), axis=0, keepdims=True)
    first_row = jnp.sum(jnp.where(sub + TILE_START_LANE == lan, start * (1.0 / SLOT_TILE), 0.0),
                        axis=0, keepdims=True)
    lane_row = lax.broadcasted_iota(I32, (1, LANES), 1)
    total = jnp.sum(padded_row, axis=1, keepdims=True)
    tail = (jnp.where(lane_row == N_EXPERTS, total, 0.0)
            + jnp.where(lane_row == PAD_HI_LANE + N_EXPERTS, float(n_slots), 0.0)
            + jnp.where(lane_row == TILE_START_LANE + N_EXPERTS, total * (1.0 / SLOT_TILE), 0.0))
    tab_ref[...] = (lo_row + hi_row + first_row + tail).astype(I32)


def _slots(eid, n_tiles):
    t = eid.shape[1]
    return pl.pallas_call(
        functools.partial(_slots_kernel, n_slots=n_tiles * SLOT_TILE),
        out_shape=[jax.ShapeDtypeStruct((TOP_K, t), I32),
                   jax.ShapeDtypeStruct((1, LANES), I32)],
        scratch_shapes=[pltpu.VMEM((TOP_K, t), F32)],
        compiler_params=pltpu.CompilerParams(vmem_limit_bytes=VMEM_LIMIT),
        name="slots",
    )(eid)


def _invert_kernel(pos_ref, tab_ref, inv_ref, *, n_pairs, n_slots):
    group = 8
    spare = n_pairs
    for e in range(N_EXPERTS + 1):
        lo, hi = tab_ref[e], tab_ref[PAD_HI_LANE + e]

        def fill(i, c, lo=lo, spare=spare):
            for k in range(group):
                inv_ref[SLOT_TILE + lo + i * group + k] = spare + i * group + k
            return c
        lax.fori_loop(0, (hi - lo + group - 1) // group, fill, 0)
        spare = spare + hi - lo

    def guard(r, c):
        inv_ref[r] = n_slots + r
        inv_ref[SLOT_TILE + n_slots + r] = n_slots + r
        return c
    lax.fori_loop(0, SLOT_TILE, guard, 0, unroll=8)

    def place(n, c):
        inv_ref[pos_ref[n]] = n
        return c
    lax.fori_loop(0, n_pairs, place, 0, unroll=16)


def _invert(pos_flat, tab, n_tiles):
    n_pairs = pos_flat.shape[0]
    n_slots = n_tiles * SLOT_TILE
    return pl.pallas_call(
        functools.partial(_invert_kernel, n_pairs=n_pairs, n_slots=n_slots),
        in_specs=[pl.BlockSpec(memory_space=pltpu.SMEM), pl.BlockSpec(memory_space=pltpu.SMEM)],
        out_specs=pl.BlockSpec(memory_space=pltpu.SMEM),
        out_shape=jax.ShapeDtypeStruct((n_slots + 2 * SLOT_TILE,), I32),
        name="invert",
    )(pos_flat, tab)


def _experts_kernel(inv_ref, tab_ref, f_ref, wg_ref, wu_ref, wd_ref, y_ref,
                    wbg, wbu, wbd, xa, xb, ya, yb, gsem, ssem, zsem, *, n_tokens, n_slots):
    e = pl.program_id(0)
    j = pl.program_id(1)
    fc = wg_ref.shape[2]
    total = tab_ref[TILE_START_LANE + N_EXPERTS]

    def gather_copy(tile, r, xbuf):
        v = inv_ref[(tile + 1) * SLOT_TILE + r]
        return pltpu.make_async_copy(f_ref.at[pl.ds(v & (n_tokens - 1), 1)], xbuf.at[pl.ds(r, 1)], gsem)

    def scatter_copy(tile, r, ybuf):
        v = inv_ref[(tile + 1) * SLOT_TILE + r]
        return pltpu.make_async_copy(ybuf.at[pl.ds(r, 1)], y_ref.at[pl.ds(v, 1)], ssem)

    def gather_wait(xbuf):
        pltpu.make_async_copy(f_ref.at[pl.ds(0, SLOT_TILE)], xbuf, gsem).wait()

    def scatter_wait(ybuf):
        pltpu.make_async_copy(ybuf, y_ref.at[pl.ds(0, SLOT_TILE)], ssem).wait()

    def rows(fn):
        def body(r, c):
            fn(r)
            return c
        lax.fori_loop(0, SLOT_TILE, body, 0, unroll=8)

    @pl.when((e == 0) & (j == 0))
    def _():
        ya[...] = jnp.zeros(ya.shape, ya.dtype)
        yb[...] = jnp.zeros(yb.shape, yb.dtype)
        rows(lambda r: gather_copy(0, r, xa).start())

        def zero_copy(tile):
            dst = pl.multiple_of(tile * SLOT_TILE, SLOT_TILE)
            return pltpu.make_async_copy(ya, y_ref.at[pl.ds(dst, SLOT_TILE)], zsem)

        def zstart(tile, c):
            zero_copy(tile).start()
            return c

        def zwait(tile, c):
            zero_copy(tile).wait()
            return c
        lax.fori_loop(total, n_slots // SLOT_TILE, zstart, 0)
        lax.fori_loop(total, n_slots // SLOT_TILE, zwait, 0)

    @pl.when(e < N_EXPERTS)
    def _():
        slot = e % 2
        col = pl.multiple_of(j * fc, fc)
        wbg[slot, :, pl.ds(col, fc)] = wg_ref[0].astype(BF16)
        wbu[slot, :, pl.ds(col, fc)] = wu_ref[0].astype(BF16)
        wbd[slot, pl.ds(col, fc), :] = wd_ref[0].astype(BF16)

    prev = jnp.maximum(e - 1, 0)
    first = tab_ref[TILE_START_LANE + prev]
    count = jnp.where(e == 0, 0, tab_ref[TILE_START_LANE + prev + 1] - first)
    lo = first + (count * j) // N_WCHUNK
    hi = first + (count * (j + 1)) // N_WCHUNK
    wslot = prev % 2

    def tile(t, c):
        def run(xcur, xnext, ycur, yprev, ws):
            gather_wait(xcur)
            for r in range(SLOT_TILE):
                gather_copy(t + 1, r, xnext).start()
            for r in range(SLOT_TILE):
                scatter_copy(t - 1, r, yprev).start(priority=1)
            x = xcur[...].astype(BF16)
            hg = jnp.dot(x, wbg[ws], preferred_element_type=F32)
            hu = jnp.dot(x, wbu[ws], preferred_element_type=F32)
            hid = (_silu(hg) * hu).astype(BF16)
            ycur[...] = jnp.dot(hid, wbd[ws], preferred_element_type=F32)
            scatter_wait(yprev)

        for ws in range(2):
            for parity, bufs in enumerate(((xa, xb, ya, yb), (xb, xa, yb, ya))):
                @pl.when((wslot == ws) & (t % 2 == parity))
                def _(bufs=bufs, ws=ws):
                    run(*bufs, ws)
        return c
    lax.fori_loop(lo, hi, tile, 0)

    @pl.when((e == N_EXPERTS) & (j == N_WCHUNK - 1))
    def _():
        last = total - 1

        def finish(xnext, ylast):
            gather_wait(xnext)
            rows(lambda r: scatter_copy(last, r, ylast).start())
            scatter_wait(ylast)

        @pl.when(last % 2 == 0)
        def _():
            finish(xb, ya)

        @pl.when(last % 2 == 1)
        def _():
            finish(xa, yb)


def _experts(inv, tab, f, wg, wu, wd, n_tiles):
    t, d = f.shape
    assert t & (t - 1) == 0
    fdim = wg.shape[2]
    fc = fdim // N_WCHUNK
    n_slots = n_tiles * SLOT_TILE

    def expert(e):
        return jnp.minimum(e, N_EXPERTS - 1)

    def chunk(e, j):
        return jnp.where(e == N_EXPERTS, N_WCHUNK - 1, j)

    return pl.pallas_call(
        functools.partial(_experts_kernel, n_tokens=t, n_slots=n_slots),
        grid_spec=pltpu.PrefetchScalarGridSpec(
            num_scalar_prefetch=2,
            grid=(N_EXPERTS + 1, N_WCHUNK),
            in_specs=[pl.BlockSpec(memory_space=pl.ANY),
                      pl.BlockSpec((1, d, fc), lambda e, j, inv, tab: (expert(e), 0, chunk(e, j))),
                      pl.BlockSpec((1, d, fc), lambda e, j, inv, tab: (expert(e), 0, chunk(e, j))),
                      pl.BlockSpec((1, fc, d), lambda e, j, inv, tab: (expert(e), chunk(e, j), 0))],
            out_specs=pl.BlockSpec(memory_space=pl.ANY),
            scratch_shapes=[pltpu.VMEM((2, d, fdim), BF16), pltpu.VMEM((2, d, fdim), BF16),
                            pltpu.VMEM((2, fdim, d), BF16),
                            pltpu.VMEM((SLOT_TILE, d), F32), pltpu.VMEM((SLOT_TILE, d), F32),
                            pltpu.VMEM((SLOT_TILE, d), F32), pltpu.VMEM((SLOT_TILE, d), F32),
                            pltpu.SemaphoreType.DMA, pltpu.SemaphoreType.DMA, pltpu.SemaphoreType.DMA]),
        out_shape=jax.ShapeDtypeStruct((n_slots + SLOT_TILE, d), F32),
        compiler_params=pltpu.CompilerParams(dimension_semantics=("arbitrary", "arbitrary"),
                                             vmem_limit_bytes=EXPERTS_VMEM_LIMIT),
        name="experts",
    )(inv, tab, f, wg, wu, wd)


def _final_kernel(y0_ref, y1_ref, h_ref, ew_ref, mod_ref, g_ref, o_ref):
    ew = ew_ref[...]
    y = ew[:, 0:1] * y0_ref[...] + ew[:, 1:2] * y1_ref[...]
    o_ref[0] = h_ref[0] + _rms(y, mod_ref[0][5:6] * g_ref[...])


def _final(y_pairs, h1, ew_t, mod, g):
    bsz, s, d = h1.shape
    t = bsz * s
    tg = min(512, s)
    nti = s // tg
    return pl.pallas_call(
        _final_kernel,
        grid=(bsz, nti),
        in_specs=[pl.BlockSpec((tg, d), lambda b, i: (b * nti + i, 0)),
                  pl.BlockSpec((tg, d), lambda b, i: (t // tg + b * nti + i, 0)),
                  pl.BlockSpec((1, tg, d), lambda b, i: (b, i, 0)),
                  pl.BlockSpec((tg, TOP_K), lambda b, i: (b * nti + i, 0)),
                  pl.BlockSpec((1, N_MOD, d), lambda b, i: (b, 0, 0)),
                  pl.BlockSpec((1, d), lambda b, i: (0, 0))],
        out_specs=pl.BlockSpec((1, tg, d), lambda b, i: (b, i, 0)),
        out_shape=jax.ShapeDtypeStruct((bsz, s, d), F32),
        compiler_params=pltpu.CompilerParams(dimension_semantics=("arbitrary", "arbitrary"),
                                             vmem_limit_bytes=VMEM_LIMIT),
        name="final",
    )(y_pairs, y_pairs, h1, ew_t, mod, g.reshape(1, d))


def kernel(x, c, positions, w_ada, b_ada, g_mix_pre, g_mix_post, w_in, sink_logits, gmlp_norm_g,
           w_spatial, b_spatial, g_group_out, w_out, g_ffn_pre, g_ffn_post, w_router_group,
           b_router_group, w_router_expert, b_router_expert, w_gate, w_up, w_down):
    bsz, s, d = x.shape
    t = bsz * s
    depth = w_ada.shape[0]
    fdim = w_gate.shape[-1]
    n_tiles = (TOP_K * t) // SLOT_TILE + N_EXPERTS
    n_slots = n_tiles * SLOT_TILE
    cos, sin = _rope_tables(positions)
    h = x
    for l in range(depth):
        mod = _ada(c, w_ada[l], b_ada[l]).reshape(bsz, N_MOD, d)
        z = _inproj(h, mod, g_mix_pre[l], w_in[l].astype(BF16))
        ycat = _mixer(z, cos, sin, sink_logits[l], gmlp_norm_g[l], w_spatial[l],
                      b_spatial[l], g_group_out[l])
        wr = jnp.concatenate([w_router_group[l], w_router_expert[l].reshape(d, N_EXPERTS)], axis=1)
        wr = jnp.pad(wr, ((0, 0), (0, LANES - N_GROUPS - N_EXPERTS)))
        br = jnp.concatenate([b_router_group[l], b_router_expert[l].reshape(N_EXPERTS)])
        br = jnp.pad(br, (0, ROUTER_ROWS - N_GROUPS - N_EXPERTS)).reshape(ROUTER_ROWS, 1)
        h1, f, eid, ew = _outproj(ycat, w_out[l].astype(BF16), h, mod, g_mix_post[l], g_ffn_pre[l],
                                  wr, br)
        pos, tab = _slots(eid, n_tiles)
        tab = tab.reshape(LANES)
        inv = _invert(pos.reshape(TOP_K * t), tab, n_tiles)
        y_pairs = _experts(inv, tab, f,
                           w_gate[l].reshape(N_EXPERTS, d, fdim),
                           w_up[l].reshape(N_EXPERTS, d, fdim),
                           w_down[l].reshape(N_EXPERTS, fdim, d), n_tiles)
        h = _final(y_pairs, h1, ew.T, mod, g_ffn_post[l])
    return h
```

```python
import functools

import jax
import jax.numpy as jnp
from jax import lax
from jax.experimental import pallas as pl
from jax.experimental.pallas import tpu as pltpu

F32 = jnp.float32
BF16 = jnp.bfloat16
I32 = jnp.int32

HEAD_DIM = 128
ATTN_HEADS = 8
KV_HEADS = 2
Q_PER_KV = ATTN_HEADS // KV_HEADS
GMLP_HEADS = 8
WINDOW = 128
BLOCK = 128
ROPE_THETA = 500000.0
ROPE_DIM = HEAD_DIM // 4
ROPE_HALF = ROPE_DIM // 2
N_GROUPS = 4
EXPERTS_PER_GROUP = 4
N_EXPERTS = N_GROUPS * EXPERTS_PER_GROUP
TOP_K = 2
N_MOD = 6
EPS = 1e-6
NEG_INF = -1e30
LOG2_E = 1.4426950408889634
LANES = 128
BF16_SUBLANES = 16
PAD_HI_LANE = 32
TILE_START_LANE = 64
N_WCHUNK = 4
ROUTER_ROWS = 32
SLOT_TILE = 256

ADA_COL_TILE = 2048
INPROJ_ROW_TILE = 1024
INPROJ_COL_TILE = 512
MIXER_TILE = 1024
OUTPROJ_TILE = 512
FINAL_TILE = 512
VMEM_LIMIT = 52 * 1024 * 1024
LARGE_VMEM_LIMIT = 58 * 1024 * 1024


def _rms(x, g):
    ms = jnp.mean(x * x, axis=-1, keepdims=True)
    return x * lax.rsqrt(ms + EPS) * g


def _silu(x):
    return x / (1.0 + jnp.exp(-x))


def _gelu(x):
    return 0.5 * x * (1.0 + lax.erf(x * 0.7071067811865476))


def _ada_kernel(c_ref, w_ref, b_ref, o_ref):
    ca = _silu(c_ref[...]).astype(BF16)
    o_ref[...] = jnp.dot(ca, w_ref[...].astype(BF16), preferred_element_type=F32) + b_ref[...]


def _ada(c, w, b):
    bsz, d = c.shape
    n = w.shape[1]
    tn = ADA_COL_TILE
    return pl.pallas_call(
        _ada_kernel,
        grid=(n // tn,),
        in_specs=[pl.BlockSpec((bsz, d), lambda j: (0, 0)),
                  pl.BlockSpec((d, tn), lambda j: (0, j)),
                  pl.BlockSpec((1, tn), lambda j: (0, j))],
        out_specs=pl.BlockSpec((bsz, tn), lambda j: (0, j)),
        out_shape=jax.ShapeDtypeStruct((bsz, n), F32),
        compiler_params=pltpu.CompilerParams(dimension_semantics=("arbitrary",),
                                             vmem_limit_bytes=VMEM_LIMIT),
        name="ada",
    )(c, w, b.reshape(1, n))


def _inproj_kernel(x_ref, mod_ref, g_ref, w_ref, o_ref, a_a, a_b, *, rows_per_step):
    g = pl.program_id(0)
    j = pl.program_id(1)
    tm = a_a.shape[0]

    @pl.when((g == 0) & (j == 0))
    def _():
        a_b[...] = jnp.zeros(a_b.shape, a_b.dtype)

    def step(a_w, a_r):
        o_ref[0] = jnp.dot(a_r[...], w_ref[...], preferred_element_type=F32).astype(BF16)
        start = pl.multiple_of(jnp.minimum(j * rows_per_step, tm - rows_per_step), BF16_SUBLANES)
        mod = mod_ref[0]
        a = _rms(x_ref[0, pl.ds(start, rows_per_step), :], g_ref[...] * (1.0 + mod[1:2])) + mod[0:1]
        a_w[pl.ds(start, rows_per_step), :] = a.astype(BF16)

    @pl.when(g % 2 == 0)
    def _():
        step(a_a, a_b)

    @pl.when(g % 2 == 1)
    def _():
        step(a_b, a_a)


def _inproj(x, mod, g, w_bf16):
    bsz, s, d = x.shape
    n = w_bf16.shape[1]
    tm = min(INPROJ_ROW_TILE, s)
    tn = INPROJ_COL_TILE
    nj = n // tn
    nti = s // tm
    ntile = bsz * nti
    rows_per_step = -(-tm // (nj * BF16_SUBLANES)) * BF16_SUBLANES

    def norm_tile(g):
        gg = jnp.minimum(g, ntile - 1)
        return gg // nti, gg % nti

    def mm_tile(g):
        gg = jnp.maximum(g - 1, 0)
        return gg // nti, gg % nti

    return pl.pallas_call(
        functools.partial(_inproj_kernel, rows_per_step=rows_per_step),
        grid=(ntile + 1, nj),
        in_specs=[pl.BlockSpec((1, tm, d), lambda g, j: norm_tile(g) + (0,)),
                  pl.BlockSpec((1, N_MOD, d), lambda g, j: (norm_tile(g)[0], 0, 0)),
                  pl.BlockSpec((1, d), lambda g, j: (0, 0)),
                  pl.BlockSpec((d, tn), lambda g, j: (0, j))],
        out_specs=pl.BlockSpec((1, tm, tn), lambda g, j: mm_tile(g) + (jnp.where(g == 0, 0, j),)),
        out_shape=jax.ShapeDtypeStruct((bsz, s, n), BF16),
        scratch_shapes=[pltpu.VMEM((tm, d), BF16), pltpu.VMEM((tm, d), BF16)],
        compiler_params=pltpu.CompilerParams(dimension_semantics=("arbitrary", "arbitrary"),
                                             vmem_limit_bytes=VMEM_LIMIT),
        name="inproj",
    )(x, mod, g.reshape(1, d), w_bf16)


def _angles_kernel(pos_ref, invf_ref, cos_ref, sin_ref):
    rows = pos_ref.shape[0]
    per_row = LANES // ROPE_HALF
    ang = pos_ref[...].astype(F32) * invf_ref[...]
    src = lax.broadcasted_iota(I32, (LANES, LANES), 0)
    dst = lax.broadcasted_iota(I32, (LANES, LANES), 1)
    rotary_lane = lax.broadcasted_iota(I32, (1, LANES), 1) < ROPE_DIM
    for out_ref, val, fill in ((cos_ref, jnp.cos(ang), 1.0), (sin_ref, jnp.sin(ang), 0.0)):
        p0 = val.astype(BF16)
        r0 = val - p0.astype(F32)
        p1 = r0.astype(BF16)
        p2 = (r0 - p1.astype(F32)).astype(BF16)
        rest = jnp.where(rotary_lane, 0.0, fill)
        for a in range(per_row):
            sel = jnp.where((dst < ROPE_DIM) & ((dst & (ROPE_HALF - 1)) + a * ROPE_HALF == src),
                            1.0, 0.0).astype(BF16)
            spread = (jnp.dot(p0, sel, preferred_element_type=F32)
                      + jnp.dot(p1, sel, preferred_element_type=F32)
                      + jnp.dot(p2, sel, preferred_element_type=F32))
            out_ref[pl.ds(a, rows, stride=per_row), :] = spread + rest


def _rope_tables(positions):
    bsz, s = positions.shape
    per_row = LANES // ROPE_HALF
    rows = bsz * s // per_row
    inv = ROPE_THETA ** (-jnp.arange(ROPE_HALF, dtype=F32) * 2.0 / ROPE_DIM)
    invf = jnp.tile(inv, per_row).reshape(1, LANES)
    pos_rep = jnp.repeat(positions.reshape(rows, per_row), ROPE_HALF, axis=1)
    cos, sin = pl.pallas_call(
        _angles_kernel,
        out_shape=[jax.ShapeDtypeStruct((bsz * s, LANES), F32)] * 2,
        name="angles",
    )(pos_rep, invf)
    return cos.reshape(bsz, s, LANES), sin.reshape(bsz, s, LANES)


def _lane_mean(x):
    k = x.shape[1]
    ones = jnp.full((k, LANES), 1.0 / k, BF16)
    return jnp.dot(x.astype(BF16), ones, preferred_element_type=F32)


def _rope(x, cos, sin):
    lane = lax.broadcasted_iota(I32, x.shape, 1)
    partner = jnp.where(lane < ROPE_HALF,
                        pltpu.roll(x, HEAD_DIM - ROPE_HALF, 1),
                        pltpu.roll(x, ROPE_HALF, 1))
    return x * cos + partner * sin


def _mixer_kernel(sink_ref, q_ref, kvp_ref, kvc_ref, kvn_ref, u0_ref, u1_ref, v0_ref, v1_ref,
                  cosp_ref, cosc_ref, cosn_ref, sinp_ref, sinc_ref, sinn_ref,
                  gn_ref, ws_ref, bst_ref, gg_ref, o_ref, ya_scr, yg_scr):
    tq = q_ref.shape[1]
    nsub = tq // BLOCK
    aw = ATTN_HEADS * HEAD_DIM
    i = pl.program_id(1)
    sign = jnp.where(lax.broadcasted_iota(I32, (1, HEAD_DIM), 1) < ROPE_HALF, -1.0, 1.0)

    cos_c, sin_c = cosc_ref[0], sinc_ref[0] * sign
    tabs = ((cosp_ref[0], sinp_ref[0] * sign), (cos_c, sin_c), (cosn_ref[0], sinn_ref[0] * sign))
    scale = HEAD_DIM ** -0.5 * LOG2_E
    cos_q, sin_q = cos_c * scale, sin_c * scale

    def prepare_kv_head(h):
        kparts, vparts = [], []
        for ref, (cs, sn) in zip((kvp_ref, kvc_ref, kvn_ref), tabs):
            k = ref[0, :, h * HEAD_DIM:(h + 1) * HEAD_DIM].astype(F32)
            kparts.append(_rope(k, cs, sn).astype(BF16))
            vparts.append(ref[0, :, (KV_HEADS + h) * HEAD_DIM:(KV_HEADS + h + 1) * HEAD_DIM])
        kband = jnp.concatenate(kparts, axis=0)
        vband = jnp.concatenate(
            [jnp.concatenate(vparts, axis=0), jnp.ones((tq + 2 * BLOCK, HEAD_DIM), BF16)], axis=1)
        qs = []
        for g in range(Q_PER_KV):
            hq = h * Q_PER_KV + g
            q = q_ref[0, :, hq * HEAD_DIM:(hq + 1) * HEAD_DIM].astype(F32)
            qs.append(_rope(q, cos_q, sin_q).astype(BF16))
        return kband, vband, qs

    rows = Q_PER_KV * BLOCK
    band = 3 * BLOCK
    q_off = lax.broadcasted_iota(I32, (rows, BLOCK), 0) & (BLOCK - 1)
    k_off = lax.broadcasted_iota(I32, (rows, BLOCK), 1)
    bias_prev = jnp.where(k_off >= q_off, 0.0, NEG_INF)
    bias_next = jnp.where(k_off <= q_off, 0.0, NEG_INF)
    first_tile = i == 0
    last_tile = i == pl.num_programs(1) - 1
    def attention_block(s, h, kband, vband, qs):
        bp = jnp.where(first_tile, NEG_INF, bias_prev) if s == 0 else bias_prev
        bn = jnp.where(last_tile, NEG_INF, bias_next) if s == nsub - 1 else bias_next
        q4 = jnp.concatenate([q[s * BLOCK:(s + 1) * BLOCK] for q in qs], axis=0)
        kb = kband[s * BLOCK:s * BLOCK + band]
        vb = vband[s * BLOCK:s * BLOCK + band]
        sc = lax.dot_general(q4, kb, (((1,), (1,)), ((), ())), preferred_element_type=F32)
        sc = jnp.concatenate([sc[:, :BLOCK] + bp, sc[:, BLOCK:2 * BLOCK], sc[:, 2 * BLOCK:] + bn],
                             axis=1)
        sink = jnp.concatenate([jnp.full((BLOCK, 1), sink_ref[h * Q_PER_KV + g] * LOG2_E, F32)
                                for g in range(Q_PER_KV)], axis=0)
        m = jnp.maximum(jnp.max(sc, axis=-1, keepdims=True), sink)
        p = jnp.exp2(sc - m)
        pv = jnp.dot(p.astype(BF16), vb, preferred_element_type=F32)
        den = pv[:, HEAD_DIM:] + jnp.exp2(sink - m)
        o = pv[:, :HEAD_DIM] / den
        for g in range(Q_PER_KV):
            hq = h * Q_PER_KV + g
            ya_scr[s * BLOCK:(s + 1) * BLOCK, hq * HEAD_DIM:(hq + 1) * HEAD_DIM] = (
                o[g * BLOCK:(g + 1) * BLOCK])

    half_heads = GMLP_HEADS // 2

    def gmlp_head(h):
        u_ref = u0_ref if h < half_heads else u1_ref
        v_ref = v0_ref if h < half_heads else v1_ref
        hh = h % half_heads
        u = _gelu(u_ref[0, :, hh * HEAD_DIM:(hh + 1) * HEAD_DIM].astype(F32))
        v = _gelu(v_ref[0, :, hh * HEAD_DIM:(hh + 1) * HEAD_DIM].astype(F32))
        mu = _lane_mean(v)
        dv = v - mu
        var = _lane_mean(dv * dv)
        vn = (dv * lax.rsqrt(var + EPS) * gn_ref[h:h + 1, :]).astype(BF16)
        w = ws_ref[h].astype(BF16)
        bias = bst_ref[:, h:h + 1]
        for cidx in range(nsub):
            sl = slice(cidx * BLOCK, (cidx + 1) * BLOCK)
            mixed = jnp.dot(w, vn[sl], preferred_element_type=F32) + bias
            yg_scr[sl, h * HEAD_DIM:(h + 1) * HEAD_DIM] = u[sl] * mixed

    heads_per_block = -(-GMLP_HEADS // (KV_HEADS * nsub))
    next_head = 0
    for h in range(KV_HEADS):
        prepared = prepare_kv_head(h)
        for s in range(nsub):
            attention_block(s, h, *prepared)
            for _ in range(heads_per_block):
                if next_head < GMLP_HEADS:
                    gmlp_head(next_head)
                    next_head += 1
    for h in range(next_head, GMLP_HEADS):
        gmlp_head(h)

    for scr, lo in ((ya_scr, 0), (yg_scr, aw)):
        y = scr[...]
        width = y.shape[1]
        inv_rms = lax.rsqrt(_lane_mean(y * y) + EPS)
        inv_rms = jnp.concatenate([inv_rms] * (width // LANES), axis=1)
        o_ref[0, :, lo:lo + width] = (y * inv_rms * gg_ref[:, lo:lo + width]).astype(BF16)


def _mixer(z, cos, sin, sink, gn, ws, bs, gg):
    assert WINDOW == BLOCK
    bsz, s, _ = z.shape
    tq = min(MIXER_TILE, s)
    nsub = tq // BLOCK
    nblk = s // BLOCK
    aw = ATTN_HEADS * HEAD_DIM
    gw = GMLP_HEADS * HEAD_DIM
    cw = 512

    def prev_blk(b, i):
        return (b, jnp.maximum(i * nsub - 1, 0), 2)

    def next_blk(b, i):
        return (b, jnp.minimum((i + 1) * nsub, nblk - 1), 2)

    tab_specs = [pl.BlockSpec((1, BLOCK, HEAD_DIM), lambda b, i: prev_blk(b, i)[:2] + (0,)),
                 pl.BlockSpec((1, tq, HEAD_DIM), lambda b, i: (b, i, 0)),
                 pl.BlockSpec((1, BLOCK, HEAD_DIM), lambda b, i: next_blk(b, i)[:2] + (0,))]
    return pl.pallas_call(
        _mixer_kernel,
        grid=(bsz, s // tq),
        in_specs=[pl.BlockSpec(memory_space=pltpu.SMEM),
                  pl.BlockSpec((1, tq, aw), lambda b, i: (b, i, 0)),
                  pl.BlockSpec((1, BLOCK, cw), prev_blk),
                  pl.BlockSpec((1, tq, cw), lambda b, i: (b, i, 2)),
                  pl.BlockSpec((1, BLOCK, cw), next_blk),
                  pl.BlockSpec((1, tq, cw), lambda b, i: (b, i, 3)),
                  pl.BlockSpec((1, tq, cw), lambda b, i: (b, i, 4)),
                  pl.BlockSpec((1, tq, cw), lambda b, i: (b, i, 5)),
                  pl.BlockSpec((1, tq, cw), lambda b, i: (b, i, 6)),
                  *tab_specs, *tab_specs,
                  pl.BlockSpec((GMLP_HEADS, HEAD_DIM), lambda b, i: (0, 0)),
                  pl.BlockSpec((GMLP_HEADS, BLOCK, BLOCK), lambda b, i: (0, 0, 0)),
                  pl.BlockSpec((BLOCK, GMLP_HEADS), lambda b, i: (0, 0)),
                  pl.BlockSpec((1, aw + gw), lambda b, i: (0, 0))],
        out_specs=pl.BlockSpec((1, tq, aw + gw), lambda b, i: (b, i, 0)),
        out_shape=jax.ShapeDtypeStruct((bsz, s, aw + gw), BF16),
        scratch_shapes=[pltpu.VMEM((tq, aw), F32), pltpu.VMEM((tq, gw), F32)],
        compiler_params=pltpu.CompilerParams(dimension_semantics=("arbitrary", "arbitrary"),
                                             vmem_limit_bytes=VMEM_LIMIT),
        name="mixer",
    )(sink, z, z, z, z, z, z, z, z, cos, cos, cos, sin, sin, sin, gn, ws, bs.T,
      gg.reshape(1, aw + gw))


def _split_bf16(x):
    hi = x.astype(BF16)
    lo = (x - hi.astype(F32)).astype(BF16)
    return hi, lo


def _outproj_kernel(y_ref, w_ref, x_ref, mod_ref, gpost_ref, gpre_ref, wr_ref, br_ref,
                    h_ref, f_ref, eid_ref, ew_ref, mix_a, mix_b, wr_split):
    g = pl.program_id(0)

    @pl.when(g == 0)
    def _():
        mix_b[...] = jnp.zeros(mix_b.shape, mix_b.dtype)
        w_hi, w_lo = _split_bf16(wr_ref[...])
        wr_split[:, :LANES] = w_hi
        wr_split[:, LANES:] = w_lo

    def step(mix_w, mix_r):
        mix_w[...] = jnp.dot(y_ref[0], w_ref[...], preferred_element_type=F32)
        for stage in _outproj_finish(mix_r, x_ref, mod_ref, gpost_ref, gpre_ref, wr_split, br_ref,
                                     h_ref, f_ref, eid_ref, ew_ref):
            stage()

    @pl.when(g % 2 == 0)
    def _():
        step(mix_a, mix_b)

    @pl.when(g % 2 == 1)
    def _():
        step(mix_b, mix_a)


def _outproj_finish(mix_ref, x_ref, mod_ref, gpost_ref, gpre_ref, wr_ref, br_ref,
                    h_ref, f_ref, eid_ref, ew_ref):
    state = {}

    def residual():
        h1 = x_ref[0] + _rms(mix_ref[...], mod_ref[0][2:3] * gpost_ref[...])
        h_ref[0] = h1
        state["h1"] = h1

    def prenorm():
        mod = mod_ref[0]
        f = _rms(state["h1"], gpre_ref[...] * (1.0 + mod[4:5])) + mod[3:4]
        f_ref[...] = f
        state["f"] = f

    def logits():
        f_hi, f_lo = _split_bf16(state["f"])
        tm = f_hi.shape[0]
        r = jnp.dot(jnp.concatenate([f_hi, f_lo], axis=0), wr_ref[...],
                    preferred_element_type=F32)
        lg = (r[:tm, :LANES] + r[:tm, LANES:]) + (r[tm:, :LANES] + r[tm:, LANES:])
        state["logits"] = lg.T[:ROUTER_ROWS] + br_ref[...]

    def route():
        _route(state["logits"], eid_ref, ew_ref)

    return [residual, prenorm, logits, route]


def _route(logits, eid_ref, ew_ref):
    gl = [logits[g:g + 1] for g in range(N_GROUPS)]
    gmax = functools.reduce(jnp.maximum, gl)
    gidx = jnp.full(gmax.shape, N_GROUPS - 1, I32)
    for g in range(N_GROUPS - 2, -1, -1):
        gidx = jnp.where(gl[g] == gmax, g, gidx)
    gval = 1.0 / functools.reduce(lambda a, b: a + b, [jnp.exp(v - gmax) for v in gl])

    es = []
    for e in range(EXPERTS_PER_GROUP):
        r = N_GROUPS + (N_GROUPS - 1) * EXPERTS_PER_GROUP + e
        v = logits[r:r + 1]
        for g in range(N_GROUPS - 2, -1, -1):
            r = N_GROUPS + g * EXPERTS_PER_GROUP + e
            v = jnp.where(gidx == g, logits[r:r + 1], v)
        es.append(v)
    m1 = functools.reduce(jnp.maximum, es)
    i1 = jnp.full(m1.shape, EXPERTS_PER_GROUP - 1, I32)
    for e in range(EXPERTS_PER_GROUP - 2, -1, -1):
        i1 = jnp.where(es[e] == m1, e, i1)
    rest = [jnp.where(i1 == e, -jnp.inf, es[e]) for e in range(EXPERTS_PER_GROUP)]
    m2 = functools.reduce(jnp.maximum, rest)
    i2 = jnp.full(m2.shape, EXPERTS_PER_GROUP - 1, I32)
    for e in range(EXPERTS_PER_GROUP - 2, -1, -1):
        i2 = jnp.where(rest[e] == m2, e, i2)
    p2 = jnp.exp(m2 - m1)
    w1 = gval / (1.0 + p2)
    w2 = gval * p2 / (1.0 + p2)
    eid_ref[0:1, :] = gidx * EXPERTS_PER_GROUP + i1
    eid_ref[1:2, :] = gidx * EXPERTS_PER_GROUP + i2
    ew_ref[0:1, :] = w1
    ew_ref[1:2, :] = w2


def _outproj(ycat, w_out_bf16, x, mod, gpost, gpre, wr, br):
    bsz, s, d = x.shape
    t = bsz * s
    tm = min(OUTPROJ_TILE, s)
    nti = s // tm
    n = bsz * nti

    def proj(g):
        gg = jnp.minimum(g, n - 1)
        return gg // nti, gg % nti

    def fin(g):
        gg = jnp.maximum(g - 1, 0)
        return gg // nti, gg % nti

    return pl.pallas_call(
        _outproj_kernel,
        grid=(n + 1,),
        in_specs=[pl.BlockSpec((1, tm, d), lambda g: proj(g) + (0,)),
                  pl.BlockSpec((d, d), lambda g: (0, 0)),
                  pl.BlockSpec((1, tm, d), lambda g: fin(g) + (0,)),
                  pl.BlockSpec((1, N_MOD, d), lambda g: (fin(g)[0], 0, 0)),
                  pl.BlockSpec((1, d), lambda g: (0, 0)),
                  pl.BlockSpec((1, d), lambda g: (0, 0)),
                  pl.BlockSpec((d, LANES), lambda g: (0, 0)),
                  pl.BlockSpec((ROUTER_ROWS, 1), lambda g: (0, 0))],
        out_specs=[pl.BlockSpec((1, tm, d), lambda g: fin(g) + (0,)),
                   pl.BlockSpec((tm, d), lambda g: (jnp.maximum(g - 1, 0), 0)),
                   pl.BlockSpec((TOP_K, tm), lambda g: (0, jnp.maximum(g - 1, 0))),
                   pl.BlockSpec((TOP_K, tm), lambda g: (0, jnp.maximum(g - 1, 0)))],
        out_shape=[jax.ShapeDtypeStruct((bsz, s, d), F32),
                   jax.ShapeDtypeStruct((t, d), F32),
                   jax.ShapeDtypeStruct((TOP_K, t), I32),
                   jax.ShapeDtypeStruct((TOP_K, t), F32)],
        scratch_shapes=[pltpu.VMEM((tm, d), F32), pltpu.VMEM((tm, d), F32),
                        pltpu.VMEM((d, 2 * LANES), BF16)],
        compiler_params=pltpu.CompilerParams(dimension_semantics=("arbitrary",),
                                             vmem_limit_bytes=LARGE_VMEM_LIMIT),
        name="outproj",
    )(ycat, w_out_bf16, x, mod, gpost.reshape(1, d), gpre.reshape(1, d), wr, br)


def _slots_kernel(eid_ref, pos_ref, tab_ref, rank_scr, *, n_slots):
    t = eid_ref.shape[1]
    chunk = min(512, t)
    nchunk = t // chunk
    tri = jnp.where(lax.broadcasted_iota(I32, (chunk, chunk), 0)
                    <= lax.broadcasted_iota(I32, (chunk, chunk), 1), 1.0, 0.0).astype(BF16)
    e_io = lax.broadcasted_iota(I32, (N_EXPERTS, chunk), 0)

    cnt = jnp.zeros((N_EXPERTS, 1), F32)
    for k in range(TOP_K):
        def rank_body(c, carry, k=k):
            off = pl.multiple_of(c * chunk, chunk)
            onehot = e_io == eid_ref[pl.ds(k, 1), pl.ds(off, chunk)]
            ones = jnp.where(onehot, 1.0, 0.0)
            prefix = jnp.dot(ones.astype(BF16), tri, preferred_element_type=F32) + carry
            rank = jnp.sum(jnp.where(onehot, prefix, 0.0), axis=0, keepdims=True) - 1.0
            rank_scr[pl.ds(k, 1), pl.ds(off, chunk)] = rank
            return carry + jnp.sum(ones, axis=1, keepdims=True)
        cnt = lax.fori_loop(0, nchunk, rank_body, cnt)

    padded = jnp.floor((cnt + (SLOT_TILE - 1)) * (1.0 / SLOT_TILE)) * SLOT_TILE
    sub = lax.broadcasted_iota(I32, (N_EXPERTS, LANES), 0)
    lan = lax.broadcasted_iota(I32, (N_EXPERTS, LANES), 1)
    padded_row = jnp.sum(jnp.where(sub == lan, padded, 0.0), axis=0, keepdims=True)
    start = jnp.sum(jnp.where(lan < sub, padded_row, 0.0), axis=1, keepdims=True)
    end = start + padded
    pad_lo = start + cnt

    for k in range(TOP_K):
        def pos_body(c, carry, k=k):
            off = pl.multiple_of(c * chunk, chunk)
            onehot = e_io == eid_ref[pl.ds(k, 1), pl.ds(off, chunk)]
            base = jnp.sum(jnp.where(onehot, start + SLOT_TILE, 0.0), axis=0, keepdims=True)
            pos_ref[pl.ds(k, 1), pl.ds(off, chunk)] = (
                base + rank_scr[pl.ds(k, 1), pl.ds(off, chunk)]).astype(I32)
            return carry
        lax.fori_loop(0, nchunk, pos_body, 0)

    lo_row = jnp.sum(jnp.where(sub == lan, pad_lo, 0.0), axis=0, keepdims=True)
    hi_row = jnp.sum(jnp.where(sub + PAD_HI_LANE == lan, end, 0.0), axis=0, keepdims=True)
    first_row = jnp.sum(jnp.where(sub + TILE_START_LANE == lan, start * (1.0 / SLOT_TILE), 0.0),
                        axis=0, keepdims=True)
    lane_row = lax.broadcasted_iota(I32, (1, LANES), 1)
    total = jnp.sum(padded_row, axis=1, keepdims=True)
    tail = (jnp.where(lane_row == N_EXPERTS, total, 0.0)
            + jnp.where(lane_row == PAD_HI_LANE + N_EXPERTS, float(n_slots), 0.0)
            + jnp.where(lane_row == TILE_START_LANE + N_EXPERTS, total * (1.0 / SLOT_TILE), 0.0))
    tab_ref[...] = (lo_row + hi_row + first_row + tail).astype(I32)


def _slots(eid, n_tiles):
    t = eid.shape[1]
    return pl.pallas_call(
        functools.partial(_slots_kernel, n_slots=n_tiles * SLOT_TILE),
        out_shape=[jax.ShapeDtypeStruct((TOP_K, t), I32),
                   jax.ShapeDtypeStruct((1, LANES), I32)],
        scratch_shapes=[pltpu.VMEM((TOP_K, t), F32)],
        compiler_params=pltpu.CompilerParams(vmem_limit_bytes=VMEM_LIMIT),
        name="slots",
    )(eid)


def _invert_kernel(pos_ref, tab_ref, inv_ref, *, n_pairs, n_slots):
    group = 8
    spare = n_pairs
    for e in range(N_EXPERTS + 1):
        lo, hi = tab_ref[e], tab_ref[PAD_HI_LANE + e]

        def fill(i, c, lo=lo, spare=spare):
            for k in range(group):
                inv_ref[SLOT_TILE + lo + i * group + k] = spare + i * group + k
            return c
        lax.fori_loop(0, (hi - lo + group - 1) // group, fill, 0)
        spare = spare + hi - lo

    def guard(r, c):
        inv_ref[r] = n_slots + r
        inv_ref[SLOT_TILE + n_slots + r] = n_slots + r
        return c
    lax.fori_loop(0, SLOT_TILE, guard, 0, unroll=8)

    def place(n, c):
        inv_ref[pos_ref[n]] = n
        return c
    lax.fori_loop(0, n_pairs, place, 0, unroll=16)


def _invert(pos_flat, tab, n_tiles):
    n_pairs = pos_flat.shape[0]
    n_slots = n_tiles * SLOT_TILE
    return pl.pallas_call(
        functools.partial(_invert_kernel, n_pairs=n_pairs, n_slots=n_slots),
        in_specs=[pl.BlockSpec(memory_space=pltpu.SMEM), pl.BlockSpec(memory_space=pltpu.SMEM)],
        out_specs=pl.BlockSpec(memory_space=pltpu.SMEM),
        out_shape=jax.ShapeDtypeStruct((n_slots + 2 * SLOT_TILE,), I32),
        name="invert",
    )(pos_flat, tab)


def _experts_kernel(inv_ref, tab_ref, f_ref, wg_ref, wu_ref, wd_ref, y_ref,
                    wbg, wbu, wbd, xa, xb, ya, yb, gsem, ssem, zsem, *, n_tokens, n_slots):
    e = pl.program_id(0)
    j = pl.program_id(1)
    fc = wg_ref.shape[2]
    total = tab_ref[TILE_START_LANE + N_EXPERTS]

    def gather_copy(tile, r, xbuf):
        v = inv_ref[(tile + 1) * SLOT_TILE + r]
        return pltpu.make_async_copy(f_ref.at[pl.ds(v & (n_tokens - 1), 1)], xbuf.at[pl.ds(r, 1)], gsem)

    def scatter_copy(tile, r, ybuf):
        v = inv_ref[(tile + 1) * SLOT_TILE + r]
        return pltpu.make_async_copy(ybuf.at[pl.ds(r, 1)], y_ref.at[pl.ds(v, 1)], ssem)

    def gather_wait(xbuf):
        pltpu.make_async_copy(f_ref.at[pl.ds(0, SLOT_TILE)], xbuf, gsem).wait()

    def scatter_wait(ybuf):
        pltpu.make_async_copy(ybuf, y_ref.at[pl.ds(0, SLOT_TILE)], ssem).wait()

    def rows(fn):
        def body(r, c):
            fn(r)
            return c
        lax.fori_loop(0, SLOT_TILE, body, 0, unroll=8)

    @pl.when((e == 0) & (j == 0))
    def _():
        ya[...] = jnp.zeros(ya.shape, ya.dtype)
        yb[...] = jnp.zeros(yb.shape, yb.dtype)
        rows(lambda r: gather_copy(0, r, xa).start())

        def zero_copy(tile):
            dst = pl.multiple_of(tile * SLOT_TILE, SLOT_TILE)
            return pltpu.make_async_copy(ya, y_ref.at[pl.ds(dst, SLOT_TILE)], zsem)

        def zstart(tile, c):
            zero_copy(tile).start()
            return c

        def zwait(tile, c):
            zero_copy(tile).wait()
            return c
        lax.fori_loop(total, n_slots // SLOT_TILE, zstart, 0)
        lax.fori_loop(total, n_slots // SLOT_TILE, zwait, 0)

    @pl.when(e < N_EXPERTS)
    def _():
        slot = e % 2
        col = pl.multiple_of(j * fc, fc)
        wbg[slot, :, pl.ds(col, fc)] = wg_ref[0].astype(BF16)
        wbu[slot, :, pl.ds(col, fc)] = wu_ref[0].astype(BF16)
        wbd[slot, pl.ds(col, fc), :] = wd_ref[0].astype(BF16)

    prev = jnp.maximum(e - 1, 0)
    first = tab_ref[TILE_START_LANE + prev]
    count = jnp.where(e == 0, 0, tab_ref[TILE_START_LANE + prev + 1] - first)
    lo = first + (count * j) // N_WCHUNK
    hi = first + (count * (j + 1)) // N_WCHUNK
    wslot = prev % 2

    def tile(t, c):
        def run(xcur, xnext, ycur, yprev, ws):
            gather_wait(xcur)
            for r in range(SLOT_TILE):
                gather_copy(t + 1, r, xnext).start()
            for r in range(SLOT_TILE):
                scatter_copy(t - 1, r, yprev).start()
            x = xcur[...].astype(BF16)
            hg = jnp.dot(x, wbg[ws], preferred_element_type=F32)
            hu = jnp.dot(x, wbu[ws], preferred_element_type=F32)
            hid = (_silu(hg) * hu).astype(BF16)
            ycur[...] = jnp.dot(hid, wbd[ws], preferred_element_type=F32)
            scatter_wait(yprev)

        for ws in range(2):
            for parity, bufs in enumerate(((xa, xb, ya, yb), (xb, xa, yb, ya))):
                @pl.when((wslot == ws) & (t % 2 == parity))
                def _(bufs=bufs, ws=ws):
                    run(*bufs, ws)
        return c
    lax.fori_loop(lo, hi, tile, 0)

    @pl.when((e == N_EXPERTS) & (j == N_WCHUNK - 1))
    def _():
        last = total - 1

        def finish(xnext, ylast):
            gather_wait(xnext)
            rows(lambda r: scatter_copy(last, r, ylast).start())
            scatter_wait(ylast)

        @pl.when(last % 2 == 0)
        def _():
            finish(xb, ya)

        @pl.when(last % 2 == 1)
        def _():
            finish(xa, yb)


def _experts(inv, tab, f, wg, wu, wd, n_tiles):
    t, d = f.shape
    assert t & (t - 1) == 0
    fdim = wg.shape[2]
    fc = fdim // N_WCHUNK
    n_slots = n_tiles * SLOT_TILE

    def expert(e):
        return jnp.minimum(e, N_EXPERTS - 1)

    def chunk(e, j):
        return jnp.where(e == N_EXPERTS, N_WCHUNK - 1, j)

    return pl.pallas_call(
        functools.partial(_experts_kernel, n_tokens=t, n_slots=n_slots),
        grid_spec=pltpu.PrefetchScalarGridSpec(
            num_scalar_prefetch=2,
            grid=(N_EXPERTS + 1, N_WCHUNK),
            in_specs=[pl.BlockSpec(memory_space=pl.ANY),
                      pl.BlockSpec((1, d, fc), lambda e, j, inv, tab: (expert(e), 0, chunk(e, j))),
                      pl.BlockSpec((1, d, fc), lambda e, j, inv, tab: (expert(e), 0, chunk(e, j))),
                      pl.BlockSpec((1, fc, d), lambda e, j, inv, tab: (expert(e), chunk(e, j), 0))],
            out_specs=pl.BlockSpec(memory_space=pl.ANY),
            scratch_shapes=[pltpu.VMEM((2, d, fdim), BF16), pltpu.VMEM((2, d, fdim), BF16),
                            pltpu.VMEM((2, fdim, d), BF16),
                            pltpu.VMEM((SLOT_TILE, d), F32), pltpu.VMEM((SLOT_TILE, d), F32),
                            pltpu.VMEM((SLOT_TILE, d), F32), pltpu.VMEM((SLOT_TILE, d), F32),
                            pltpu.SemaphoreType.DMA, pltpu.SemaphoreType.DMA, pltpu.SemaphoreType.DMA]),
        out_shape=jax.ShapeDtypeStruct((n_slots + SLOT_TILE, d), F32),
        compiler_params=pltpu.CompilerParams(dimension_semantics=("arbitrary", "arbitrary"),
                                             vmem_limit_bytes=LARGE_VMEM_LIMIT),
        name="experts",
    )(inv, tab, f, wg, wu, wd)


def _final_kernel(y0_ref, y1_ref, h_ref, ew_ref, mod_ref, g_ref, o_ref):
    ew = ew_ref[...]
    y = ew[:, 0:1] * y0_ref[...] + ew[:, 1:2] * y1_ref[...]
    o_ref[0] = h_ref[0] + _rms(y, mod_ref[0][5:6] * g_ref[...])


def _final(y_pairs, h1, ew_t, mod, g):
    bsz, s, d = h1.shape
    t = bsz * s
    tg = min(FINAL_TILE, s)
    nti = s // tg
    return pl.pallas_call(
        _final_kernel,
        grid=(bsz, nti),
        in_specs=[pl.BlockSpec((tg, d), lambda b, i: (b * nti + i, 0)),
                  pl.BlockSpec((tg, d), lambda b, i: (t // tg + b * nti + i, 0)),
                  pl.BlockSpec((1, tg, d), lambda b, i: (b, i, 0)),
                  pl.BlockSpec((tg, TOP_K), lambda b, i: (b * nti + i, 0)),
                  pl.BlockSpec((1, N_MOD, d), lambda b, i: (b, 0, 0)),
                  pl.BlockSpec((1, d), lambda b, i: (0, 0))],
        out_specs=pl.BlockSpec((1, tg, d), lambda b, i: (b, i, 0)),
        out_shape=jax.ShapeDtypeStruct((bsz, s, d), F32),
        compiler_params=pltpu.CompilerParams(dimension_semantics=("arbitrary", "arbitrary"),
                                             vmem_limit_bytes=VMEM_LIMIT),
        name="final",
    )(y_pairs, y_pairs, h1, ew_t, mod, g.reshape(1, d))


def kernel(x, c, positions, w_ada, b_ada, g_mix_pre, g_mix_post, w_in, sink_logits, gmlp_norm_g,
           w_spatial, b_spatial, g_group_out, w_out, g_ffn_pre, g_ffn_post, w_router_group,
           b_router_group, w_router_expert, b_router_expert, w_gate, w_up, w_down):
    bsz, s, d = x.shape
    t = bsz * s
    depth = w_ada.shape[0]
    fdim = w_gate.shape[-1]
    n_tiles = (TOP_K * t) // SLOT_TILE + N_EXPERTS
    n_slots = n_tiles * SLOT_TILE
    cos, sin = _rope_tables(positions)
    h = x
    for l in range(depth):
        mod = _ada(c, w_ada[l], b_ada[l]).reshape(bsz, N_MOD, d)
        z = _inproj(h, mod, g_mix_pre[l], w_in[l].astype(BF16))
        ycat = _mixer(z, cos, sin, sink_logits[l], gmlp_norm_g[l], w_spatial[l],
                      b_spatial[l], g_group_out[l])
        wr = jnp.concatenate([w_router_group[l], w_router_expert[l].reshape(d, N_EXPERTS)], axis=1)
        wr = jnp.pad(wr, ((0, 0), (0, LANES - N_GROUPS - N_EXPERTS)))
        br = jnp.concatenate([b_router_group[l], b_router_expert[l].reshape(N_EXPERTS)])
        br = jnp.pad(br, (0, ROUTER_ROWS - N_GROUPS - N_EXPERTS)).reshape(ROUTER_ROWS, 1)
        h1, f, eid, ew = _outproj(ycat, w_out[l].astype(BF16), h, mod, g_mix_post[l], g_ffn_pre[l],
                                  wr, br)
        pos, tab = _slots(eid, n_tiles)
        tab = tab.reshape(LANES)
        inv = _invert(pos.reshape(TOP_K * t), tab, n_tiles)
        y_pairs = _experts(inv, tab, f,
                           w_gate[l].reshape(N_EXPERTS, d, fdim),
                           w_up[l].reshape(N_EXPERTS, d, fdim),
                           w_down[l].reshape(N_EXPERTS, fdim, d), n_tiles)
        h = _final(y_pairs, h1, ew.T, mod, g_ffn_post[l])
    return h
```

```python
import functools

import jax
import jax.numpy as jnp
from jax import lax
from jax.experimental import pallas as pl
from jax.experimental.pallas import tpu as pltpu

F32 = jnp.float32
BF16 = jnp.bfloat16
I32 = jnp.int32

HEAD_DIM = 128
ATTN_HEADS = 8
KV_HEADS = 2
Q_PER_KV = ATTN_HEADS // KV_HEADS
GMLP_HEADS = 8
WINDOW = 128
BLOCK = 128
ROPE_THETA = 500000.0
ROPE_DIM = HEAD_DIM // 4
ROPE_HALF = ROPE_DIM // 2
N_GROUPS = 4
EXPERTS_PER_GROUP = 4
N_EXPERTS = N_GROUPS * EXPERTS_PER_GROUP
TOP_K = 2
N_MOD = 6
EPS = 1e-6
NEG_INF = -1e30
LOG2_E = 1.4426950408889634
LANES = 128
BF16_SUBLANES = 16
PAD_HI_LANE = 32
TILE_START_LANE = 64
N_WCHUNK = 4
ROUTER_ROWS = 32
SLOT_TILE = 256

ADA_COL_TILE = 1024
INPROJ_ROW_TILE = 2048
INPROJ_COL_TILE = 512
INPROJ_SLICES = 4
MIXER_TILE = 1024
OUTPROJ_TILE = 512
FINAL_TILE = 512
VMEM_LIMIT = 52 * 1024 * 1024
LARGE_VMEM_LIMIT = 58 * 1024 * 1024


def _rms(x, g):
    ms = jnp.mean(x * x, axis=-1, keepdims=True)
    return x * lax.rsqrt(ms + EPS) * g


def _silu(x):
    return x / (1.0 + jnp.exp(-x))


def _gelu(x):
    return 0.5 * x * (1.0 + lax.erf(x * 0.7071067811865476))


def _ada_kernel(c_ref, w_ref, b_ref, o_ref):
    ca = _silu(c_ref[...]).astype(BF16)
    o_ref[...] = jnp.dot(ca, w_ref[...].astype(BF16), preferred_element_type=F32) + b_ref[...]


def _ada(c, w, b):
    bsz, d = c.shape
    n = w.shape[1]
    tn = ADA_COL_TILE
    return pl.pallas_call(
        _ada_kernel,
        grid=(n // tn,),
        in_specs=[pl.BlockSpec((bsz, d), lambda j: (0, 0)),
                  pl.BlockSpec((d, tn), lambda j: (0, j)),
                  pl.BlockSpec((1, tn), lambda j: (0, j))],
        out_specs=pl.BlockSpec((bsz, tn), lambda j: (0, j)),
        out_shape=jax.ShapeDtypeStruct((bsz, n), F32),
        compiler_params=pltpu.CompilerParams(dimension_semantics=("arbitrary",),
                                             vmem_limit_bytes=VMEM_LIMIT),
        name="ada",
    )(c, w, b.reshape(1, n))


def _inproj_kernel(x_ref, mod_ref, g_ref, w_ref, o_ref, a_scr, *, n_tiles):
    g = pl.program_id(0)
    j = pl.program_id(1)
    rows = x_ref.shape[1]

    @pl.when((g < n_tiles) & (j % 2 == 0) & (j < 2 * INPROJ_SLICES))
    def _():
        mod = mod_ref[0]
        a = _rms(x_ref[0], g_ref[...] * (1.0 + mod[1:2])) + mod[0:1]
        start = pl.multiple_of((j // 2) * rows, rows)
        a_scr[g % 2, pl.ds(start, rows), :] = a.astype(BF16)

    @pl.when(g > 0)
    def _():
        o_ref[0] = jnp.dot(a_scr[(g - 1) % 2], w_ref[...], preferred_element_type=F32).astype(BF16)


def _inproj(x, mod, g, w_bf16):
    bsz, s, d = x.shape
    n = w_bf16.shape[1]
    tm = min(INPROJ_ROW_TILE, s)
    tn = INPROJ_COL_TILE
    nj = n // tn
    nti = s // tm
    ntile = bsz * nti
    assert 2 * (INPROJ_SLICES - 1) < nj
    rows = tm // INPROJ_SLICES

    def norm_tile(g):
        gg = jnp.minimum(g, ntile - 1)
        return gg // nti, gg % nti

    def mm_tile(g):
        gg = jnp.maximum(g - 1, 0)
        return gg // nti, gg % nti

    def x_block(g, j):
        b, i = norm_tile(g)
        return b, i * INPROJ_SLICES + jnp.minimum(j // 2, INPROJ_SLICES - 1), 0

    return pl.pallas_call(
        functools.partial(_inproj_kernel, n_tiles=ntile),
        grid=(ntile + 1, nj),
        in_specs=[pl.BlockSpec((1, rows, d), x_block),
                  pl.BlockSpec((1, N_MOD, d), lambda g, j: (norm_tile(g)[0], 0, 0)),
                  pl.BlockSpec((1, d), lambda g, j: (0, 0)),
                  pl.BlockSpec((d, tn), lambda g, j: (0, j))],
        out_specs=pl.BlockSpec((1, tm, tn), lambda g, j: mm_tile(g) + (jnp.where(g == 0, 0, j),)),
        out_shape=jax.ShapeDtypeStruct((bsz, s, n), BF16),
        scratch_shapes=[pltpu.VMEM((2, tm, d), BF16)],
        compiler_params=pltpu.CompilerParams(dimension_semantics=("arbitrary", "arbitrary"),
                                             vmem_limit_bytes=VMEM_LIMIT),
        name="inproj",
    )(x, mod, g.reshape(1, d), w_bf16)


def _angles_kernel(pos_ref, invf_ref, cos_ref, sin_ref):
    rows = pos_ref.shape[0]
    per_row = LANES // ROPE_HALF
    ang = pos_ref[...].astype(F32) * invf_ref[...]
    src = lax.broadcasted_iota(I32, (LANES, LANES), 0)
    dst = lax.broadcasted_iota(I32, (LANES, LANES), 1)
    rotary_lane = lax.broadcasted_iota(I32, (1, LANES), 1) < ROPE_DIM
    for out_ref, val, fill in ((cos_ref, jnp.cos(ang), 1.0), (sin_ref, jnp.sin(ang), 0.0)):
        p0 = val.astype(BF16)
        r0 = val - p0.astype(F32)
        p1 = r0.astype(BF16)
        p2 = (r0 - p1.astype(F32)).astype(BF16)
        rest = jnp.where(rotary_lane, 0.0, fill)
        for a in range(per_row):
            sel = jnp.where((dst < ROPE_DIM) & ((dst & (ROPE_HALF - 1)) + a * ROPE_HALF == src),
                            1.0, 0.0).astype(BF16)
            spread = (jnp.dot(p0, sel, preferred_element_type=F32)
                      + jnp.dot(p1, sel, preferred_element_type=F32)
                      + jnp.dot(p2, sel, preferred_element_type=F32))
            out_ref[pl.ds(a, rows, stride=per_row), :] = spread + rest


def _rope_tables(positions):
    bsz, s = positions.shape
    per_row = LANES // ROPE_HALF
    rows = bsz * s // per_row
    inv = ROPE_THETA ** (-jnp.arange(ROPE_HALF, dtype=F32) * 2.0 / ROPE_DIM)
    invf = jnp.tile(inv, per_row).reshape(1, LANES)
    pos_rep = jnp.repeat(positions.reshape(rows, per_row), ROPE_HALF, axis=1)
    cos, sin = pl.pallas_call(
        _angles_kernel,
        out_shape=[jax.ShapeDtypeStruct((bsz * s, LANES), F32)] * 2,
        name="angles",
    )(pos_rep, invf)
    return cos.reshape(bsz, s, LANES), sin.reshape(bsz, s, LANES)


def _lane_mean(x):
    k = x.shape[1]
    ones = jnp.full((k, LANES), 1.0 / k, BF16)
    return jnp.dot(x.astype(BF16), ones, preferred_element_type=F32)


def _rope(x, cos, sin):
    lane = lax.broadcasted_iota(I32, x.shape, 1)
    partner = jnp.where(lane < ROPE_HALF,
                        pltpu.roll(x, HEAD_DIM - ROPE_HALF, 1),
                        pltpu.roll(x, ROPE_HALF, 1))
    return x * cos + partner * sin


def _mixer_kernel(sink_ref, q_ref, kvp_ref, kvc_ref, kvn_ref, u0_ref, u1_ref, v0_ref, v1_ref,
                  cosp_ref, cosc_ref, cosn_ref, sinp_ref, sinc_ref, sinn_ref,
                  gn_ref, ws_ref, bst_ref, gg_ref, o_ref, ya_scr, yg_scr):
    tq = q_ref.shape[1]
    nsub = tq // BLOCK
    aw = ATTN_HEADS * HEAD_DIM
    i = pl.program_id(1)
    sign = jnp.where(lax.broadcasted_iota(I32, (1, HEAD_DIM), 1) < ROPE_HALF, -1.0, 1.0)

    cos_c, sin_c = cosc_ref[0], sinc_ref[0] * sign
    tabs = ((cosp_ref[0], sinp_ref[0] * sign), (cos_c, sin_c), (cosn_ref[0], sinn_ref[0] * sign))
    scale = HEAD_DIM ** -0.5 * LOG2_E
    cos_q, sin_q = cos_c * scale, sin_c * scale

    def prepare_kv_head(h):
        kparts, vparts = [], []
        for ref, (cs, sn) in zip((kvp_ref, kvc_ref, kvn_ref), tabs):
            k = ref[0, :, h * HEAD_DIM:(h + 1) * HEAD_DIM].astype(F32)
            kparts.append(_rope(k, cs, sn).astype(BF16))
            vparts.append(ref[0, :, (KV_HEADS + h) * HEAD_DIM:(KV_HEADS + h + 1) * HEAD_DIM])
        kband = jnp.concatenate(kparts, axis=0)
        vband = jnp.concatenate(
            [jnp.concatenate(vparts, axis=0), jnp.ones((tq + 2 * BLOCK, HEAD_DIM), BF16)], axis=1)
        qs = []
        for g in range(Q_PER_KV):
            hq = h * Q_PER_KV + g
            q = q_ref[0, :, hq * HEAD_DIM:(hq + 1) * HEAD_DIM].astype(F32)
            qs.append(_rope(q, cos_q, sin_q).astype(BF16))
        return kband, vband, qs

    rows = Q_PER_KV * BLOCK
    band = 3 * BLOCK
    q_off = lax.broadcasted_iota(I32, (rows, BLOCK), 0) & (BLOCK - 1)
    k_off = lax.broadcasted_iota(I32, (rows, BLOCK), 1)
    bias_prev = jnp.where(k_off >= q_off, 0.0, NEG_INF)
    bias_next = jnp.where(k_off <= q_off, 0.0, NEG_INF)
    first_tile = i == 0
    last_tile = i == pl.num_programs(1) - 1
    def attention_block(s, h, kband, vband, qs):
        bp = jnp.where(first_tile, NEG_INF, bias_prev) if s == 0 else bias_prev
        bn = jnp.where(last_tile, NEG_INF, bias_next) if s == nsub - 1 else bias_next
        q4 = jnp.concatenate([q[s * BLOCK:(s + 1) * BLOCK] for q in qs], axis=0)
        kb = kband[s * BLOCK:s * BLOCK + band]
        vb = vband[s * BLOCK:s * BLOCK + band]
        sc = lax.dot_general(q4, kb, (((1,), (1,)), ((), ())), preferred_element_type=F32)
        sc = jnp.concatenate([sc[:, :BLOCK] + bp, sc[:, BLOCK:2 * BLOCK], sc[:, 2 * BLOCK:] + bn],
                             axis=1)
        sink = jnp.concatenate([jnp.full((BLOCK, 1), sink_ref[h * Q_PER_KV + g] * LOG2_E, F32)
                                for g in range(Q_PER_KV)], axis=0)
        m = jnp.maximum(jnp.max(sc, axis=-1, keepdims=True), sink)
        p = jnp.exp2(sc - m)
        pv = jnp.dot(p.astype(BF16), vb, preferred_element_type=F32)
        den = pv[:, HEAD_DIM:] + jnp.exp2(sink - m)
        o = pv[:, :HEAD_DIM] / den
        for g in range(Q_PER_KV):
            hq = h * Q_PER_KV + g
            ya_scr[s * BLOCK:(s + 1) * BLOCK, hq * HEAD_DIM:(hq + 1) * HEAD_DIM] = (
                o[g * BLOCK:(g + 1) * BLOCK])

    half_heads = GMLP_HEADS // 2

    def gmlp_head(h):
        u_ref = u0_ref if h < half_heads else u1_ref
        v_ref = v0_ref if h < half_heads else v1_ref
        hh = h % half_heads
        u = _gelu(u_ref[0, :, hh * HEAD_DIM:(hh + 1) * HEAD_DIM].astype(F32))
        v = _gelu(v_ref[0, :, hh * HEAD_DIM:(hh + 1) * HEAD_DIM].astype(F32))
        mu = _lane_mean(v)
        dv = v - mu
        var = _lane_mean(dv * dv)
        vn = (dv * lax.rsqrt(var + EPS) * gn_ref[h:h + 1, :]).astype(BF16)
        w = ws_ref[h].astype(BF16)
        bias = bst_ref[:, h:h + 1]
        for cidx in range(nsub):
            sl = slice(cidx * BLOCK, (cidx + 1) * BLOCK)
            mixed = jnp.dot(w, vn[sl], preferred_element_type=F32) + bias
            yg_scr[sl, h * HEAD_DIM:(h + 1) * HEAD_DIM] = u[sl] * mixed

    heads_per_block = -(-GMLP_HEADS // (KV_HEADS * nsub))
    next_head = 0
    for h in range(KV_HEADS):
        prepared = prepare_kv_head(h)
        for s in range(nsub):
            attention_block(s, h, *prepared)
            for _ in range(heads_per_block):
                if next_head < GMLP_HEADS:
                    gmlp_head(next_head)
                    next_head += 1
    for h in range(next_head, GMLP_HEADS):
        gmlp_head(h)

    for scr, lo in ((ya_scr, 0), (yg_scr, aw)):
        y = scr[...]
        width = y.shape[1]
        inv_rms = lax.rsqrt(_lane_mean(y * y) + EPS)
        inv_rms = jnp.concatenate([inv_rms] * (width // LANES), axis=1)
        o_ref[0, :, lo:lo + width] = (y * inv_rms * gg_ref[:, lo:lo + width]).astype(BF16)


def _mixer(z, cos, sin, sink, gn, ws, bs, gg):
    assert WINDOW == BLOCK
    bsz, s, _ = z.shape
    tq = min(MIXER_TILE, s)
    nsub = tq // BLOCK
    nblk = s // BLOCK
    aw = ATTN_HEADS * HEAD_DIM
    gw = GMLP_HEADS * HEAD_DIM
    cw = 512

    def prev_blk(b, i):
        return (b, jnp.maximum(i * nsub - 1, 0), 2)

    def next_blk(b, i):
        return (b, jnp.minimum((i + 1) * nsub, nblk - 1), 2)

    tab_specs = [pl.BlockSpec((1, BLOCK, HEAD_DIM), lambda b, i: prev_blk(b, i)[:2] + (0,)),
                 pl.BlockSpec((1, tq, HEAD_DIM), lambda b, i: (b, i, 0)),
                 pl.BlockSpec((1, BLOCK, HEAD_DIM), lambda b, i: next_blk(b, i)[:2] + (0,))]
    return pl.pallas_call(
        _mixer_kernel,
        grid=(bsz, s // tq),
        in_specs=[pl.BlockSpec(memory_space=pltpu.SMEM),
                  pl.BlockSpec((1, tq, aw), lambda b, i: (b, i, 0)),
                  pl.BlockSpec((1, BLOCK, cw), prev_blk),
                  pl.BlockSpec((1, tq, cw), lambda b, i: (b, i, 2)),
                  pl.BlockSpec((1, BLOCK, cw), next_blk),
                  pl.BlockSpec((1, tq, cw), lambda b, i: (b, i, 3)),
                  pl.BlockSpec((1, tq, cw), lambda b, i: (b, i, 4)),
                  pl.BlockSpec((1, tq, cw), lambda b, i: (b, i, 5)),
                  pl.BlockSpec((1, tq, cw), lambda b, i: (b, i, 6)),
                  *tab_specs, *tab_specs,
                  pl.BlockSpec((GMLP_HEADS, HEAD_DIM), lambda b, i: (0, 0)),
                  pl.BlockSpec((GMLP_HEADS, BLOCK, BLOCK), lambda b, i: (0, 0, 0)),
                  pl.BlockSpec((BLOCK, GMLP_HEADS), lambda b, i: (0, 0)),
                  pl.BlockSpec((1, aw + gw), lambda b, i: (0, 0))],
        out_specs=pl.BlockSpec((1, tq, aw + gw), lambda b, i: (b, i, 0)),
        out_shape=jax.ShapeDtypeStruct((bsz, s, aw + gw), BF16),
        scratch_shapes=[pltpu.VMEM((tq, aw), F32), pltpu.VMEM((tq, gw), F32)],
        compiler_params=pltpu.CompilerParams(dimension_semantics=("arbitrary", "arbitrary"),
                                             vmem_limit_bytes=VMEM_LIMIT),
        name="mixer",
    )(sink, z, z, z, z, z, z, z, z, cos, cos, cos, sin, sin, sin, gn, ws, bs.T,
      gg.reshape(1, aw + gw))


def _split_bf16(x):
    hi = x.astype(BF16)
    lo = (x - hi.astype(F32)).astype(BF16)
    return hi, lo


def _outproj_kernel(y_ref, w_ref, x_ref, mod_ref, gpost_ref, gpre_ref, wr_ref, br_ref,
                    h_ref, f_ref, eid_ref, ew_ref, mix_a, mix_b, wr_split):
    g = pl.program_id(0)

    @pl.when(g == 0)
    def _():
        mix_b[...] = jnp.zeros(mix_b.shape, mix_b.dtype)
        w_hi, w_lo = _split_bf16(wr_ref[...])
        wr_split[:, :LANES] = w_hi
        wr_split[:, LANES:] = w_lo

    def step(mix_w, mix_r):
        mix_w[...] = jnp.dot(y_ref[0], w_ref[...], preferred_element_type=F32)
        for stage in _outproj_finish(mix_r, x_ref, mod_ref, gpost_ref, gpre_ref, wr_split, br_ref,
                                     h_ref, f_ref, eid_ref, ew_ref):
            stage()

    @pl.when(g % 2 == 0)
    def _():
        step(mix_a, mix_b)

    @pl.when(g % 2 == 1)
    def _():
        step(mix_b, mix_a)


def _outproj_finish(mix_ref, x_ref, mod_ref, gpost_ref, gpre_ref, wr_ref, br_ref,
                    h_ref, f_ref, eid_ref, ew_ref):
    state = {}

    def residual():
        h1 = x_ref[0] + _rms(mix_ref[...], mod_ref[0][2:3] * gpost_ref[...])
        h_ref[0] = h1
        state["h1"] = h1

    def prenorm():
        mod = mod_ref[0]
        f = _rms(state["h1"], gpre_ref[...] * (1.0 + mod[4:5])) + mod[3:4]
        f_ref[...] = f
        state["f"] = f

    def logits():
        f_hi, f_lo = _split_bf16(state["f"])
        tm = f_hi.shape[0]
        r = jnp.dot(jnp.concatenate([f_hi, f_lo], axis=0), wr_ref[...],
                    preferred_element_type=F32)
        lg = (r[:tm, :LANES] + r[:tm, LANES:]) + (r[tm:, :LANES] + r[tm:, LANES:])
        state["logits"] = lg.T[:ROUTER_ROWS] + br_ref[...]

    def route():
        _route(state["logits"], eid_ref, ew_ref)

    return [residual, prenorm, logits, route]


def _route(logits, eid_ref, ew_ref):
    gl = [logits[g:g + 1] for g in range(N_GROUPS)]
    gmax = functools.reduce(jnp.maximum, gl)
    gidx = jnp.full(gmax.shape, N_GROUPS - 1, I32)
    for g in range(N_GROUPS - 2, -1, -1):
        gidx = jnp.where(gl[g] == gmax, g, gidx)
    gval = 1.0 / functools.reduce(lambda a, b: a + b, [jnp.exp(v - gmax) for v in gl])

    es = []
    for e in range(EXPERTS_PER_GROUP):
        r = N_GROUPS + (N_GROUPS - 1) * EXPERTS_PER_GROUP + e
        v = logits[r:r + 1]
        for g in range(N_GROUPS - 2, -1, -1):
            r = N_GROUPS + g * EXPERTS_PER_GROUP + e
            v = jnp.where(gidx == g, logits[r:r + 1], v)
        es.append(v)
    m1 = functools.reduce(jnp.maximum, es)
    i1 = jnp.full(m1.shape, EXPERTS_PER_GROUP - 1, I32)
    for e in range(EXPERTS_PER_GROUP - 2, -1, -1):
        i1 = jnp.where(es[e] == m1, e, i1)
    rest = [jnp.where(i1 == e, -jnp.inf, es[e]) for e in range(EXPERTS_PER_GROUP)]
    m2 = functools.reduce(jnp.maximum, rest)
    i2 = jnp.full(m2.shape, EXPERTS_PER_GROUP - 1, I32)
    for e in range(EXPERTS_PER_GROUP - 2, -1, -1):
        i2 = jnp.where(rest[e] == m2, e, i2)
    p2 = jnp.exp(m2 - m1)
    w1 = gval / (1.0 + p2)
    w2 = gval * p2 / (1.0 + p2)
    eid_ref[0:1, :] = gidx * EXPERTS_PER_GROUP + i1
    eid_ref[1:2, :] = gidx * EXPERTS_PER_GROUP + i2
    ew_ref[0:1, :] = w1
    ew_ref[1:2, :] = w2


def _outproj(ycat, w_out_bf16, x, mod, gpost, gpre, wr, br):
    bsz, s, d = x.shape
    t = bsz * s
    tm = min(OUTPROJ_TILE, s)
    nti = s // tm
    n = bsz * nti

    def proj(g):
        gg = jnp.minimum(g, n - 1)
        return gg // nti, gg % nti

    def fin(g):
        gg = jnp.maximum(g - 1, 0)
        return gg // nti, gg % nti

    return pl.pallas_call(
        _outproj_kernel,
        grid=(n + 1,),
        in_specs=[pl.BlockSpec((1, tm, d), lambda g: proj(g) + (0,)),
                  pl.BlockSpec((d, d), lambda g: (0, 0)),
                  pl.BlockSpec((1, tm, d), lambda g: fin(g) + (0,)),
                  pl.BlockSpec((1, N_MOD, d), lambda g: (fin(g)[0], 0, 0)),
                  pl.BlockSpec((1, d), lambda g: (0, 0)),
                  pl.BlockSpec((1, d), lambda g: (0, 0)),
                  pl.BlockSpec((d, LANES), lambda g: (0, 0)),
                  pl.BlockSpec((ROUTER_ROWS, 1), lambda g: (0, 0))],
        out_specs=[pl.BlockSpec((1, tm, d), lambda g: fin(g) + (0,)),
                   pl.BlockSpec((tm, d), lambda g: (jnp.maximum(g - 1, 0), 0)),
                   pl.BlockSpec((TOP_K, tm), lambda g: (0, jnp.maximum(g - 1, 0))),
                   pl.BlockSpec((TOP_K, tm), lambda g: (0, jnp.maximum(g - 1, 0)))],
        out_shape=[jax.ShapeDtypeStruct((bsz, s, d), F32),
                   jax.ShapeDtypeStruct((t, d), F32),
                   jax.ShapeDtypeStruct((TOP_K, t), I32),
                   jax.ShapeDtypeStruct((TOP_K, t), F32)],
        scratch_shapes=[pltpu.VMEM((tm, d), F32), pltpu.VMEM((tm, d), F32),
                        pltpu.VMEM((d, 2 * LANES), BF16)],
        compiler_params=pltpu.CompilerParams(dimension_semantics=("arbitrary",),
                                             vmem_limit_bytes=LARGE_VMEM_LIMIT),
        name="outproj",
    )(ycat, w_out_bf16, x, mod, gpost.reshape(1, d), gpre.reshape(1, d), wr, br)


def _slots_kernel(eid_ref, pos_ref, tab_ref, rank_scr, *, n_slots):
    t = eid_ref.shape[1]
    chunk = min(512, t)
    nchunk = t // chunk
    tri = jnp.where(lax.broadcasted_iota(I32, (chunk, chunk), 0)
                    <= lax.broadcasted_iota(I32, (chunk, chunk), 1), 1.0, 0.0).astype(BF16)
    e_io = lax.broadcasted_iota(I32, (N_EXPERTS, chunk), 0)

    cnt = jnp.zeros((N_EXPERTS, 1), F32)
    for k in range(TOP_K):
        def rank_body(c, carry, k=k):
            off = pl.multiple_of(c * chunk, chunk)
            onehot = e_io == eid_ref[pl.ds(k, 1), pl.ds(off, chunk)]
            ones = jnp.where(onehot, 1.0, 0.0)
            prefix = jnp.dot(ones.astype(BF16), tri, preferred_element_type=F32) + carry
            rank = jnp.sum(jnp.where(onehot, prefix, 0.0), axis=0, keepdims=True) - 1.0
            rank_scr[pl.ds(k, 1), pl.ds(off, chunk)] = rank
            return carry + jnp.sum(ones, axis=1, keepdims=True)
        cnt = lax.fori_loop(0, nchunk, rank_body, cnt)

    padded = jnp.floor((cnt + (SLOT_TILE - 1)) * (1.0 / SLOT_TILE)) * SLOT_TILE
    sub = lax.broadcasted_iota(I32, (N_EXPERTS, LANES), 0)
    lan = lax.broadcasted_iota(I32, (N_EXPERTS, LANES), 1)
    padded_row = jnp.sum(jnp.where(sub == lan, padded, 0.0), axis=0, keepdims=True)
    start = jnp.sum(jnp.where(lan < sub, padded_row, 0.0), axis=1, keepdims=True)
    end = start + padded
    pad_lo = start + cnt

    for k in range(TOP_K):
        def pos_body(c, carry, k=k):
            off = pl.multiple_of(c * chunk, chunk)
            onehot = e_io == eid_ref[pl.ds(k, 1), pl.ds(off, chunk)]
            base = jnp.sum(jnp.where(onehot, start + SLOT_TILE, 0.0), axis=0, keepdims=True)
            pos_ref[pl.ds(k, 1), pl.ds(off, chunk)] = (
                base + rank_scr[pl.ds(k, 1), pl.ds(off, chunk)]).astype(I32)
            return carry
        lax.fori_loop(0, nchunk, pos_body, 0)

    lo_row = jnp.sum(jnp.where(sub == lan, pad_lo, 0.0), axis=0, keepdims=True)
    hi_row = jnp.sum(jnp.where(sub + PAD_HI_LANE == lan, end, 0.0), axis=0, keepdims=True)
    first_row = jnp.sum(jnp.where(sub + TILE_START_LANE == lan, start * (1.0 / SLOT_TILE), 0.0),
                        axis=0, keepdims=True)
    lane_row = lax.broadcasted_iota(I32, (1, LANES), 1)
    total = jnp.sum(padded_row, axis=1, keepdims=True)
    tail = (jnp.where(lane_row == N_EXPERTS, total, 0.0)
            + jnp.where(lane_row == PAD_HI_LANE + N_EXPERTS, float(n_slots), 0.0)
            + jnp.where(lane_row == TILE_START_LANE + N_EXPERTS, total * (1.0 / SLOT_TILE), 0.0))
    tab_ref[...] = (lo_row + hi_row + first_row + tail).astype(I32)


def _slots(eid, n_tiles):
    t = eid.shape[1]
    return pl.pallas_call(
        functools.partial(_slots_kernel, n_slots=n_tiles * SLOT_TILE),
        out_shape=[jax.ShapeDtypeStruct((TOP_K, t), I32),
                   jax.ShapeDtypeStruct((1, LANES), I32)],
        scratch_shapes=[pltpu.VMEM((TOP_K, t), F32)],
        compiler_params=pltpu.CompilerParams(vmem_limit_bytes=VMEM_LIMIT),
        name="slots",
    )(eid)


def _invert_kernel(pos_ref, tab_ref, inv_ref, *, n_pairs, n_slots):
    group = 8
    spare = n_pairs
    for e in range(N_EXPERTS + 1):
        lo, hi = tab_ref[e], tab_ref[PAD_HI_LANE + e]

        def fill(i, c, lo=lo, spare=spare):
            for k in range(group):
                inv_ref[SLOT_TILE + lo + i * group + k] = spare + i * group + k
            return c
        lax.fori_loop(0, (hi - lo + group - 1) // group, fill, 0)
        spare = spare + hi - lo

    def guard(r, c):
        inv_ref[r] = n_slots + r
        inv_ref[SLOT_TILE + n_slots + r] = n_slots + r
        return c
    lax.fori_loop(0, SLOT_TILE, guard, 0, unroll=8)

    def place(n, c):
        inv_ref[pos_ref[n]] = n
        return c
    lax.fori_loop(0, n_pairs, place, 0, unroll=16)


def _invert(pos_flat, tab, n_tiles):
    n_pairs = pos_flat.shape[0]
    n_slots = n_tiles * SLOT_TILE
    return pl.pallas_call(
        functools.partial(_invert_kernel, n_pairs=n_pairs, n_slots=n_slots),
        in_specs=[pl.BlockSpec(memory_space=pltpu.SMEM), pl.BlockSpec(memory_space=pltpu.SMEM)],
        out_specs=pl.BlockSpec(memory_space=pltpu.SMEM),
        out_shape=jax.ShapeDtypeStruct((n_slots + 2 * SLOT_TILE,), I32),
        name="invert",
    )(pos_flat, tab)


def _experts_kernel(inv_ref, tab_ref, f_ref, wg_ref, wu_ref, wd_ref, y_ref,
                    wbg, wbu, wbd, xa, xb, ya, yb, gsem, ssem, zsem, *, n_tokens, n_slots):
    e = pl.program_id(0)
    j = pl.program_id(1)
    fc = wg_ref.shape[2]
    total = tab_ref[TILE_START_LANE + N_EXPERTS]

    def gather_copy(tile, r, xbuf):
        v = inv_ref[(tile + 1) * SLOT_TILE + r]
        return pltpu.make_async_copy(f_ref.at[pl.ds(v & (n_tokens - 1), 1)], xbuf.at[pl.ds(r, 1)], gsem)

    def scatter_copy(tile, r, ybuf):
        v = inv_ref[(tile + 1) * SLOT_TILE + r]
        return pltpu.make_async_copy(ybuf.at[pl.ds(r, 1)], y_ref.at[pl.ds(v, 1)], ssem)

    def gather_wait(xbuf):
        pltpu.make_async_copy(f_ref.at[pl.ds(0, SLOT_TILE)], xbuf, gsem).wait()

    def scatter_wait(ybuf):
        pltpu.make_async_copy(ybuf, y_ref.at[pl.ds(0, SLOT_TILE)], ssem).wait()

    def rows(fn):
        def body(r, c):
            fn(r)
            return c
        lax.fori_loop(0, SLOT_TILE, body, 0, unroll=8)

    @pl.when((e == 0) & (j == 0))
    def _():
        ya[...] = jnp.zeros(ya.shape, ya.dtype)
        yb[...] = jnp.zeros(yb.shape, yb.dtype)
        rows(lambda r: gather_copy(0, r, xa).start())

        def zero_copy(tile):
            dst = pl.multiple_of(tile * SLOT_TILE, SLOT_TILE)
            return pltpu.make_async_copy(ya, y_ref.at[pl.ds(dst, SLOT_TILE)], zsem)

        def zstart(tile, c):
            zero_copy(tile).start()
            return c

        def zwait(tile, c):
            zero_copy(tile).wait()
            return c
        lax.fori_loop(total, n_slots // SLOT_TILE, zstart, 0)
        lax.fori_loop(total, n_slots // SLOT_TILE, zwait, 0)

    @pl.when(e < N_EXPERTS)
    def _():
        slot = e % 2
        col = pl.multiple_of(j * fc, fc)
        wbg[slot, :, pl.ds(col, fc)] = wg_ref[0].astype(BF16)
        wbu[slot, :, pl.ds(col, fc)] = wu_ref[0].astype(BF16)
        wbd[slot, pl.ds(col, fc), :] = wd_ref[0].astype(BF16)

    prev = jnp.maximum(e - 1, 0)
    first = tab_ref[TILE_START_LANE + prev]
    count = jnp.where(e == 0, 0, tab_ref[TILE_START_LANE + prev + 1] - first)
    lo = first + (count * j) // N_WCHUNK
    hi = first + (count * (j + 1)) // N_WCHUNK
    wslot = prev % 2

    def tile(t, c):
        def run(xcur, xnext, ycur, yprev, ws):
            gather_wait(xcur)
            for r in range(SLOT_TILE):
                gather_copy(t + 1, r, xnext).start()
            for r in range(SLOT_TILE):
                scatter_copy(t - 1, r, yprev).start()
            x = xcur[...].astype(BF16)
            hg = jnp.dot(x, wbg[ws], preferred_element_type=F32)
            hu = jnp.dot(x, wbu[ws], preferred_element_type=F32)
            hid = (_silu(hg) * hu).astype(BF16)
            ycur[...] = jnp.dot(hid, wbd[ws], preferred_element_type=F32)
            scatter_wait(yprev)

        for ws in range(2):
            for parity, bufs in enumerate(((xa, xb, ya, yb), (xb, xa, yb, ya))):
                @pl.when((wslot == ws) & (t % 2 == parity))
                def _(bufs=bufs, ws=ws):
                    run(*bufs, ws)
        return c
    lax.fori_loop(lo, hi, tile, 0)

    @pl.when((e == N_EXPERTS) & (j == N_WCHUNK - 1))
    def _():
        last = total - 1

        def finish(xnext, ylast):
            gather_wait(xnext)
            rows(lambda r: scatter_copy(last, r, ylast).start())
            scatter_wait(ylast)

        @pl.when(last % 2 == 0)
        def _():
            finish(xb, ya)

        @pl.when(last % 2 == 1)
        def _():
            finish(xa, yb)


def _experts(inv, tab, f, wg, wu, wd, n_tiles):
    t, d = f.shape
    assert t & (t - 1) == 0
    fdim = wg.shape[2]
    fc = fdim // N_WCHUNK
    n_slots = n_tiles * SLOT_TILE

    def expert(e):
        return jnp.minimum(e, N_EXPERTS - 1)

    def chunk(e, j):
        return jnp.where(e == N_EXPERTS, N_WCHUNK - 1, j)

    return pl.pallas_call(
        functools.partial(_experts_kernel, n_tokens=t, n_slots=n_slots),
        grid_spec=pltpu.PrefetchScalarGridSpec(
            num_scalar_prefetch=2,
            grid=(N_EXPERTS + 1, N_WCHUNK),
            in_specs=[pl.BlockSpec(memory_space=pl.ANY),
                      pl.BlockSpec((1, d, fc), lambda e, j, inv, tab: (expert(e), 0, chunk(e, j))),
                      pl.BlockSpec((1, d, fc), lambda e, j, inv, tab: (expert(e), 0, chunk(e, j))),
                      pl.BlockSpec((1, fc, d), lambda e, j, inv, tab: (expert(e), chunk(e, j), 0))],
            out_specs=pl.BlockSpec(memory_space=pl.ANY),
            scratch_shapes=[pltpu.VMEM((2, d, fdim), BF16), pltpu.VMEM((2, d, fdim), BF16),
                            pltpu.VMEM((2, fdim, d), BF16),
                            pltpu.VMEM((SLOT_TILE, d), F32), pltpu.VMEM((SLOT_TILE, d), F32),
                            pltpu.VMEM((SLOT_TILE, d), F32), pltpu.VMEM((SLOT_TILE, d), F32),
                            pltpu.SemaphoreType.DMA, pltpu.SemaphoreType.DMA, pltpu.SemaphoreType.DMA]),
        out_shape=jax.ShapeDtypeStruct((n_slots + SLOT_TILE, d), F32),
        compiler_params=pltpu.CompilerParams(dimension_semantics=("arbitrary", "arbitrary"),
                                             vmem_limit_bytes=LARGE_VMEM_LIMIT),
        name="experts",
    )(inv, tab, f, wg, wu, wd)


def _final_kernel(y0_ref, y1_ref, h_ref, ew_ref, mod_ref, g_ref, o_ref):
    ew = ew_ref[...]
    y = ew[:, 0:1] * y0_ref[...] + ew[:, 1:2] * y1_ref[...]
    o_ref[0] = h_ref[0] + _rms(y, mod_ref[0][5:6] * g_ref[...])


def _final(y_pairs, h1, ew_t, mod, g):
    bsz, s, d = h1.shape
    t = bsz * s
    tg = min(FINAL_TILE, s)
    nti = s // tg
    return pl.pallas_call(
        _final_kernel,
        grid=(bsz, nti),
        in_specs=[pl.BlockSpec((tg, d), lambda b, i: (b * nti + i, 0)),
                  pl.BlockSpec((tg, d), lambda b, i: (t // tg + b * nti + i, 0)),
                  pl.BlockSpec((1, tg, d), lambda b, i: (b, i, 0)),
                  pl.BlockSpec((tg, TOP_K), lambda b, i: (b * nti + i, 0)),
                  pl.BlockSpec((1, N_MOD, d), lambda b, i: (b, 0, 0)),
                  pl.BlockSpec((1, d), lambda b, i: (0, 0))],
        out_specs=pl.BlockSpec((1, tg, d), lambda b, i: (b, i, 0)),
        out_shape=jax.ShapeDtypeStruct((bsz, s, d), F32),
        compiler_params=pltpu.CompilerParams(dimension_semantics=("arbitrary", "arbitrary"),
                                             vmem_limit_bytes=VMEM_LIMIT),
        name="final",
    )(y_pairs, y_pairs, h1, ew_t, mod, g.reshape(1, d))


def kernel(x, c, positions, w_ada, b_ada, g_mix_pre, g_mix_post, w_in, sink_logits, gmlp_norm_g,
           w_spatial, b_spatial, g_group_out, w_out, g_ffn_pre, g_ffn_post, w_router_group,
           b_router_group, w_router_expert, b_router_expert, w_gate, w_up, w_down):
    bsz, s, d = x.shape
    t = bsz * s
    depth = w_ada.shape[0]
    fdim = w_gate.shape[-1]
    n_tiles = (TOP_K * t) // SLOT_TILE + N_EXPERTS
    n_slots = n_tiles * SLOT_TILE
    cos, sin = _rope_tables(positions)
    h = x
    for l in range(depth):
        mod = _ada(c, w_ada[l], b_ada[l]).reshape(bsz, N_MOD, d)
        z = _inproj(h, mod, g_mix_pre[l], w_in[l].astype(BF16))
        ycat = _mixer(z, cos, sin, sink_logits[l], gmlp_norm_g[l], w_spatial[l],
                      b_spatial[l], g_group_out[l])
        wr = jnp.concatenate([w_router_group[l], w_router_expert[l].reshape(d, N_EXPERTS)], axis=1)
        wr = jnp.pad(wr, ((0, 0), (0, LANES - N_GROUPS - N_EXPERTS)))
        br = jnp.concatenate([b_router_group[l], b_router_expert[l].reshape(N_EXPERTS)])
        br = jnp.pad(br, (0, ROUTER_ROWS - N_GROUPS - N_EXPERTS)).reshape(ROUTER_ROWS, 1)
        h1, f, eid, ew = _outproj(ycat, w_out[l].astype(BF16), h, mod, g_mix_post[l], g_ffn_pre[l],
                                  wr, br)
        pos, tab = _slots(eid, n_tiles)
        tab = tab.reshape(LANES)
        inv = _invert(pos.reshape(TOP_K * t), tab, n_tiles)
        y_pairs = _experts(inv, tab, f,
                           w_gate[l].reshape(N_EXPERTS, d, fdim),
                           w_up[l].reshape(N_EXPERTS, d, fdim),
                           w_down[l].reshape(N_EXPERTS, fdim, d), n_tiles)
        h = _final(y_pairs, h1, ew.T, mod, g_ffn_post[l])
    return h
```

```python
import functools

import jax
import jax.numpy as jnp
from jax import lax
from jax.experimental import pallas as pl
from jax.experimental.pallas import tpu as pltpu

F32 = jnp.float32
BF16 = jnp.bfloat16
I32 = jnp.int32

HEAD_DIM = 128
ATTN_HEADS = 8
KV_HEADS = 2
Q_PER_KV = ATTN_HEADS // KV_HEADS
GMLP_HEADS = 8
WINDOW = 128
BLOCK = 128
ROPE_THETA = 500000.0
ROPE_DIM = HEAD_DIM // 4
ROPE_HALF = ROPE_DIM // 2
N_GROUPS = 4
EXPERTS_PER_GROUP = 4
N_EXPERTS = N_GROUPS * EXPERTS_PER_GROUP
TOP_K = 2
N_MOD = 6
EPS = 1e-6
NEG_INF = -1e30
LOG2_E = 1.4426950408889634
LANES = 128
BF16_SUBLANES = 16
PAD_HI_LANE = 32
TILE_START_LANE = 64
N_WCHUNK = 4
ROUTER_ROWS = 32
SLOT_TILE = 256

ADA_COL_TILE = 1024
INPROJ_ROW_TILE = 2048
INPROJ_COL_TILE = 512
INPROJ_SLICES = 4
MIXER_TILE = 1024
OUTPROJ_TILE = 512
FINAL_TILE = 512
VMEM_LIMIT = 52 * 1024 * 1024
LARGE_VMEM_LIMIT = 58 * 1024 * 1024


def _rms(x, g):
    ms = jnp.mean(x * x, axis=-1, keepdims=True)
    return x * lax.rsqrt(ms + EPS) * g


def _silu(x):
    return x / (1.0 + jnp.exp(-x))


def _gelu(x):
    return 0.5 * x * (1.0 + lax.erf(x * 0.7071067811865476))


def _ada_kernel(c_ref, w_ref, b_ref, o_ref):
    ca = _silu(c_ref[...]).astype(BF16)
    o_ref[...] = jnp.dot(ca, w_ref[...].astype(BF16), preferred_element_type=F32) + b_ref[...]


def _ada(c, w, b):
    bsz, d = c.shape
    n = w.shape[1]
    tn = ADA_COL_TILE
    return pl.pallas_call(
        _ada_kernel,
        grid=(n // tn,),
        in_specs=[pl.BlockSpec((bsz, d), lambda j: (0, 0)),
                  pl.BlockSpec((d, tn), lambda j: (0, j)),
                  pl.BlockSpec((1, tn), lambda j: (0, j))],
        out_specs=pl.BlockSpec((bsz, tn), lambda j: (0, j)),
        out_shape=jax.ShapeDtypeStruct((bsz, n), F32),
        compiler_params=pltpu.CompilerParams(dimension_semantics=("arbitrary",),
                                             vmem_limit_bytes=VMEM_LIMIT),
        name="ada",
    )(c, w, b.reshape(1, n))


def _inproj_kernel(x_ref, mod_ref, g_ref, w_ref, o_ref, a_scr, *, n_tiles):
    g = pl.program_id(0)
    j = pl.program_id(1)
    rows = x_ref.shape[1]

    @pl.when((g < n_tiles) & (j % 2 == 0) & (j < 2 * INPROJ_SLICES))
    def _():
        mod = mod_ref[0]
        a = _rms(x_ref[0], g_ref[...] * (1.0 + mod[1:2])) + mod[0:1]
        start = pl.multiple_of((j // 2) * rows, rows)
        a_scr[g % 2, pl.ds(start, rows), :] = a.astype(BF16)

    @pl.when(g > 0)
    def _():
        o_ref[0] = jnp.dot(a_scr[(g - 1) % 2], w_ref[...], preferred_element_type=F32).astype(BF16)


def _inproj(x, mod, g, w_bf16):
    bsz, s, d = x.shape
    n = w_bf16.shape[1]
    tm = min(INPROJ_ROW_TILE, s)
    tn = INPROJ_COL_TILE
    nj = n // tn
    nti = s // tm
    ntile = bsz * nti
    assert 2 * (INPROJ_SLICES - 1) < nj
    rows = tm // INPROJ_SLICES

    def norm_tile(g):
        gg = jnp.minimum(g, ntile - 1)
        return gg // nti, gg % nti

    def mm_tile(g):
        gg = jnp.maximum(g - 1, 0)
        return gg // nti, gg % nti

    def x_block(g, j):
        b, i = norm_tile(g)
        return b, i * INPROJ_SLICES + jnp.minimum(j // 2, INPROJ_SLICES - 1), 0

    return pl.pallas_call(
        functools.partial(_inproj_kernel, n_tiles=ntile),
        grid=(ntile + 1, nj),
        in_specs=[pl.BlockSpec((1, rows, d), x_block),
                  pl.BlockSpec((1, N_MOD, d), lambda g, j: (norm_tile(g)[0], 0, 0)),
                  pl.BlockSpec((1, d), lambda g, j: (0, 0)),
                  pl.BlockSpec((d, tn), lambda g, j: (0, j))],
        out_specs=pl.BlockSpec((1, tm, tn), lambda g, j: mm_tile(g) + (jnp.where(g == 0, 0, j),)),
        out_shape=jax.ShapeDtypeStruct((bsz, s, n), BF16),
        scratch_shapes=[pltpu.VMEM((2, tm, d), BF16)],
        compiler_params=pltpu.CompilerParams(dimension_semantics=("arbitrary", "arbitrary"),
                                             vmem_limit_bytes=VMEM_LIMIT),
        name="inproj",
    )(x, mod, g.reshape(1, d), w_bf16)


def _angles_kernel(pos_ref, invf_ref, cos_ref, sin_ref):
    rows = pos_ref.shape[0]
    per_row = LANES // ROPE_HALF
    ang = pos_ref[...].astype(F32) * invf_ref[...]
    src = lax.broadcasted_iota(I32, (LANES, LANES), 0)
    dst = lax.broadcasted_iota(I32, (LANES, LANES), 1)
    rotary_lane = lax.broadcasted_iota(I32, (1, LANES), 1) < ROPE_DIM
    for out_ref, val, fill in ((cos_ref, jnp.cos(ang), 1.0), (sin_ref, jnp.sin(ang), 0.0)):
        p0 = val.astype(BF16)
        r0 = val - p0.astype(F32)
        p1 = r0.astype(BF16)
        p2 = (r0 - p1.astype(F32)).astype(BF16)
        rest = jnp.where(rotary_lane, 0.0, fill)
        for a in range(per_row):
            sel = jnp.where((dst < ROPE_DIM) & ((dst & (ROPE_HALF - 1)) + a * ROPE_HALF == src),
                            1.0, 0.0).astype(BF16)
            spread = (jnp.dot(p0, sel, preferred_element_type=F32)
                      + jnp.dot(p1, sel, preferred_element_type=F32)
                      + jnp.dot(p2, sel, preferred_element_type=F32))
            out_ref[pl.ds(a, rows, stride=per_row), :] = spread + rest


def _rope_tables(positions):
    bsz, s = positions.shape
    per_row = LANES // ROPE_HALF
    rows = bsz * s // per_row
    inv = ROPE_THETA ** (-jnp.arange(ROPE_HALF, dtype=F32) * 2.0 / ROPE_DIM)
    invf = jnp.tile(inv, per_row).reshape(1, LANES)
    pos_rep = jnp.repeat(positions.reshape(rows, per_row), ROPE_HALF, axis=1)
    cos, sin = pl.pallas_call(
        _angles_kernel,
        out_shape=[jax.ShapeDtypeStruct((bsz * s, LANES), F32)] * 2,
        name="angles",
    )(pos_rep, invf)
    return cos.reshape(bsz, s, LANES), sin.reshape(bsz, s, LANES)


def _lane_mean(x):
    k = x.shape[1]
    ones = jnp.full((k, LANES), 1.0 / k, BF16)
    return jnp.dot(x.astype(BF16), ones, preferred_element_type=F32)


def _rope(x, cos, sin):
    lane = lax.broadcasted_iota(I32, x.shape, 1)
    partner = jnp.where(lane < ROPE_HALF,
                        pltpu.roll(x, HEAD_DIM - ROPE_HALF, 1),
                        pltpu.roll(x, ROPE_HALF, 1))
    return x * cos + partner * sin


def _mixer_kernel(sink_ref, q_ref, kvp_ref, kvc_ref, kvn_ref, u0_ref, u1_ref, v0_ref, v1_ref,
                  cosp_ref, cosc_ref, cosn_ref, sinp_ref, sinc_ref, sinn_ref,
                  gn_ref, ws_ref, bst_ref, gg_ref, o_ref, ya_scr, yg_scr):
    tq = q_ref.shape[1]
    nsub = tq // BLOCK
    aw = ATTN_HEADS * HEAD_DIM
    i = pl.program_id(1)
    sign = jnp.where(lax.broadcasted_iota(I32, (1, HEAD_DIM), 1) < ROPE_HALF, -1.0, 1.0)

    cos_c, sin_c = cosc_ref[0], sinc_ref[0] * sign
    tabs = ((cosp_ref[0], sinp_ref[0] * sign), (cos_c, sin_c), (cosn_ref[0], sinn_ref[0] * sign))
    scale = HEAD_DIM ** -0.5 * LOG2_E
    cos_q, sin_q = cos_c * scale, sin_c * scale

    def prepare_kv_head(h):
        kparts, vparts = [], []
        for ref, (cs, sn) in zip((kvp_ref, kvc_ref, kvn_ref), tabs):
            k = ref[0, :, h * HEAD_DIM:(h + 1) * HEAD_DIM].astype(F32)
            kparts.append(_rope(k, cs, sn).astype(BF16))
            vparts.append(ref[0, :, (KV_HEADS + h) * HEAD_DIM:(KV_HEADS + h + 1) * HEAD_DIM])
        kband = jnp.concatenate(kparts, axis=0)
        vband = jnp.concatenate(
            [jnp.concatenate(vparts, axis=0), jnp.ones((tq + 2 * BLOCK, HEAD_DIM), BF16)], axis=1)
        qs = []
        for g in range(Q_PER_KV):
            hq = h * Q_PER_KV + g
            q = q_ref[0, :, hq * HEAD_DIM:(hq + 1) * HEAD_DIM].astype(F32)
            qs.append(_rope(q, cos_q, sin_q).astype(BF16))
        return kband, vband, qs

    rows = Q_PER_KV * BLOCK
    band = 3 * BLOCK
    q_off = lax.broadcasted_iota(I32, (rows, BLOCK), 0) & (BLOCK - 1)
    k_off = lax.broadcasted_iota(I32, (rows, BLOCK), 1)
    bias_prev = jnp.where(k_off >= q_off, 0.0, NEG_INF)
    bias_next = jnp.where(k_off <= q_off, 0.0, NEG_INF)
    first_tile = i == 0
    last_tile = i == pl.num_programs(1) - 1
    def attention_block(s, h, kband, vband, qs):
        bp = jnp.where(first_tile, NEG_INF, bias_prev) if s == 0 else bias_prev
        bn = jnp.where(last_tile, NEG_INF, bias_next) if s == nsub - 1 else bias_next
        q4 = jnp.concatenate([q[s * BLOCK:(s + 1) * BLOCK] for q in qs], axis=0)
        kb = kband[s * BLOCK:s * BLOCK + band]
        vb = vband[s * BLOCK:s * BLOCK + band]
        sc = lax.dot_general(q4, kb, (((1,), (1,)), ((), ())), preferred_element_type=F32)
        sc = jnp.concatenate([sc[:, :BLOCK] + bp, sc[:, BLOCK:2 * BLOCK], sc[:, 2 * BLOCK:] + bn],
                             axis=1)
        sink = jnp.concatenate([jnp.full((BLOCK, 1), sink_ref[h * Q_PER_KV + g] * LOG2_E, F32)
                                for g in range(Q_PER_KV)], axis=0)
        m = jnp.maximum(jnp.max(sc, axis=-1, keepdims=True), sink)
        p = jnp.exp2(sc - m)
        pv = jnp.dot(p.astype(BF16), vb, preferred_element_type=F32)
        den = pv[:, HEAD_DIM:] + jnp.exp2(sink - m)
        o = pv[:, :HEAD_DIM] / den
        for g in range(Q_PER_KV):
            hq = h * Q_PER_KV + g
            ya_scr[s * BLOCK:(s + 1) * BLOCK, hq * HEAD_DIM:(hq + 1) * HEAD_DIM] = (
                o[g * BLOCK:(g + 1) * BLOCK])

    half_heads = GMLP_HEADS // 2

    def gmlp_head(h):
        u_ref = u0_ref if h < half_heads else u1_ref
        v_ref = v0_ref if h < half_heads else v1_ref
        hh = h % half_heads
        u = _gelu(u_ref[0, :, hh * HEAD_DIM:(hh + 1) * HEAD_DIM].astype(F32))
        v = _gelu(v_ref[0, :, hh * HEAD_DIM:(hh + 1) * HEAD_DIM].astype(F32))
        mu = _lane_mean(v)
        dv = v - mu
        var = _lane_mean(dv * dv)
        vn = (dv * lax.rsqrt(var + EPS) * gn_ref[h:h + 1, :]).astype(BF16)
        w = ws_ref[h].astype(BF16)
        bias = bst_ref[:, h:h + 1]
        for cidx in range(nsub):
            sl = slice(cidx * BLOCK, (cidx + 1) * BLOCK)
            mixed = jnp.dot(w, vn[sl], preferred_element_type=F32) + bias
            yg_scr[sl, h * HEAD_DIM:(h + 1) * HEAD_DIM] = u[sl] * mixed

    heads_per_block = -(-GMLP_HEADS // (KV_HEADS * nsub))
    next_head = 0
    for h in range(KV_HEADS):
        prepared = prepare_kv_head(h)
        for s in range(nsub):
            attention_block(s, h, *prepared)
            for _ in range(heads_per_block):
                if next_head < GMLP_HEADS:
                    gmlp_head(next_head)
                    next_head += 1
    for h in range(next_head, GMLP_HEADS):
        gmlp_head(h)

    for scr, lo in ((ya_scr, 0), (yg_scr, aw)):
        y = scr[...]
        width = y.shape[1]
        inv_rms = lax.rsqrt(_lane_mean(y * y) + EPS)
        inv_rms = jnp.concatenate([inv_rms] * (width // LANES), axis=1)
        o_ref[0, :, lo:lo + width] = (y * inv_rms * gg_ref[:, lo:lo + width]).astype(BF16)


def _mixer(z, cos, sin, sink, gn, ws, bs, gg):
    assert WINDOW == BLOCK
    bsz, s, _ = z.shape
    tq = min(MIXER_TILE, s)
    nsub = tq // BLOCK
    nblk = s // BLOCK
    aw = ATTN_HEADS * HEAD_DIM
    gw = GMLP_HEADS * HEAD_DIM
    cw = 512

    def prev_blk(b, i):
        return (b, jnp.maximum(i * nsub - 1, 0), 2)

    def next_blk(b, i):
        return (b, jnp.minimum((i + 1) * nsub, nblk - 1), 2)

    tab_specs = [pl.BlockSpec((1, BLOCK, HEAD_DIM), lambda b, i: prev_blk(b, i)[:2] + (0,)),
                 pl.BlockSpec((1, tq, HEAD_DIM), lambda b, i: (b, i, 0)),
                 pl.BlockSpec((1, BLOCK, HEAD_DIM), lambda b, i: next_blk(b, i)[:2] + (0,))]
    return pl.pallas_call(
        _mixer_kernel,
        grid=(bsz, s // tq),
        in_specs=[pl.BlockSpec(memory_space=pltpu.SMEM),
                  pl.BlockSpec((1, tq, aw), lambda b, i: (b, i, 0)),
                  pl.BlockSpec((1, BLOCK, cw), prev_blk),
                  pl.BlockSpec((1, tq, cw), lambda b, i: (b, i, 2)),
                  pl.BlockSpec((1, BLOCK, cw), next_blk),
                  pl.BlockSpec((1, tq, cw), lambda b, i: (b, i, 3)),
                  pl.BlockSpec((1, tq, cw), lambda b, i: (b, i, 4)),
                  pl.BlockSpec((1, tq, cw), lambda b, i: (b, i, 5)),
                  pl.BlockSpec((1, tq, cw), lambda b, i: (b, i, 6)),
                  *tab_specs, *tab_specs,
                  pl.BlockSpec((GMLP_HEADS, HEAD_DIM), lambda b, i: (0, 0)),
                  pl.BlockSpec((GMLP_HEADS, BLOCK, BLOCK), lambda b, i: (0, 0, 0)),
                  pl.BlockSpec((BLOCK, GMLP_HEADS), lambda b, i: (0, 0)),
                  pl.BlockSpec((1, aw + gw), lambda b, i: (0, 0))],
        out_specs=pl.BlockSpec((1, tq, aw + gw), lambda b, i: (b, i, 0)),
        out_shape=jax.ShapeDtypeStruct((bsz, s, aw + gw), BF16),
        scratch_shapes=[pltpu.VMEM((tq, aw), F32), pltpu.VMEM((tq, gw), F32)],
        compiler_params=pltpu.CompilerParams(dimension_semantics=("arbitrary", "arbitrary"),
                                             vmem_limit_bytes=VMEM_LIMIT),
        name="mixer",
    )(sink, z, z, z, z, z, z, z, z, cos, cos, cos, sin, sin, sin, gn, ws, bs.T,
      gg.reshape(1, aw + gw))


def _split_bf16(x):
    hi = x.astype(BF16)
    lo = (x - hi.astype(F32)).astype(BF16)
    return hi, lo


def _outproj_kernel(y_ref, w_ref, x_ref, mod_ref, gpost_ref, gpre_ref, wr_ref, br_ref,
                    h_ref, f_ref, eid_ref, ew_ref, mix, wr_split):
    @pl.when((pl.program_id(0) == 0) & (pl.program_id(1) == 0))
    def _():
        w_hi, w_lo = _split_bf16(wr_ref[...])
        wr_split[:, :LANES] = w_hi
        wr_split[:, LANES:] = w_lo

    mix[...] = jnp.dot(y_ref[0], w_ref[...], preferred_element_type=F32)
    for stage in _outproj_finish(mix, x_ref, mod_ref, gpost_ref, gpre_ref, wr_split, br_ref,
                                 h_ref, f_ref, eid_ref, ew_ref):
        stage()


def _outproj_finish(mix_ref, x_ref, mod_ref, gpost_ref, gpre_ref, wr_ref, br_ref,
                    h_ref, f_ref, eid_ref, ew_ref):
    state = {}

    def residual():
        h1 = x_ref[0] + _rms(mix_ref[...], mod_ref[0][2:3] * gpost_ref[...])
        h_ref[0] = h1
        state["h1"] = h1

    def prenorm():
        mod = mod_ref[0]
        f = _rms(state["h1"], gpre_ref[...] * (1.0 + mod[4:5])) + mod[3:4]
        f_ref[...] = f
        state["f"] = f

    def logits():
        f_hi, f_lo = _split_bf16(state["f"])
        tm = f_hi.shape[0]
        r = jnp.dot(jnp.concatenate([f_hi, f_lo], axis=0), wr_ref[...],
                    preferred_element_type=F32)
        lg = (r[:tm, :LANES] + r[:tm, LANES:]) + (r[tm:, :LANES] + r[tm:, LANES:])
        state["logits"] = lg.T[:ROUTER_ROWS] + br_ref[...]

    def route():
        _route(state["logits"], eid_ref, ew_ref)

    return [residual, prenorm, logits, route]


def _route(logits, eid_ref, ew_ref):
    gl = [logits[g:g + 1] for g in range(N_GROUPS)]
    gmax = functools.reduce(jnp.maximum, gl)
    gidx = jnp.full(gmax.shape, N_GROUPS - 1, I32)
    for g in range(N_GROUPS - 2, -1, -1):
        gidx = jnp.where(gl[g] == gmax, g, gidx)
    gval = 1.0 / functools.reduce(lambda a, b: a + b, [jnp.exp(v - gmax) for v in gl])

    es = []
    for e in range(EXPERTS_PER_GROUP):
        r = N_GROUPS + (N_GROUPS - 1) * EXPERTS_PER_GROUP + e
        v = logits[r:r + 1]
        for g in range(N_GROUPS - 2, -1, -1):
            r = N_GROUPS + g * EXPERTS_PER_GROUP + e
            v = jnp.where(gidx == g, logits[r:r + 1], v)
        es.append(v)
    m1 = functools.reduce(jnp.maximum, es)
    i1 = jnp.full(m1.shape, EXPERTS_PER_GROUP - 1, I32)
    for e in range(EXPERTS_PER_GROUP - 2, -1, -1):
        i1 = jnp.where(es[e] == m1, e, i1)
    rest = [jnp.where(i1 == e, -jnp.inf, es[e]) for e in range(EXPERTS_PER_GROUP)]
    m2 = functools.reduce(jnp.maximum, rest)
    i2 = jnp.full(m2.shape, EXPERTS_PER_GROUP - 1, I32)
    for e in range(EXPERTS_PER_GROUP - 2, -1, -1):
        i2 = jnp.where(rest[e] == m2, e, i2)
    p2 = jnp.exp(m2 - m1)
    w1 = gval / (1.0 + p2)
    w2 = gval * p2 / (1.0 + p2)
    eid_ref[0:1, :] = gidx * EXPERTS_PER_GROUP + i1
    eid_ref[1:2, :] = gidx * EXPERTS_PER_GROUP + i2
    ew_ref[0:1, :] = w1
    ew_ref[1:2, :] = w2


def _outproj(ycat, w_out_bf16, x, mod, gpost, gpre, wr, br):
    bsz, s, d = x.shape
    t = bsz * s
    tm = min(OUTPROJ_TILE, s)
    nti = s // tm
    return pl.pallas_call(
        _outproj_kernel,
        grid=(bsz, nti),
        in_specs=[pl.BlockSpec((1, tm, d), lambda b, i: (b, i, 0)),
                  pl.BlockSpec((d, d), lambda b, i: (0, 0)),
                  pl.BlockSpec((1, tm, d), lambda b, i: (b, i, 0)),
                  pl.BlockSpec((1, N_MOD, d), lambda b, i: (b, 0, 0)),
                  pl.BlockSpec((1, d), lambda b, i: (0, 0)),
                  pl.BlockSpec((1, d), lambda b, i: (0, 0)),
                  pl.BlockSpec((d, LANES), lambda b, i: (0, 0)),
                  pl.BlockSpec((ROUTER_ROWS, 1), lambda b, i: (0, 0))],
        out_specs=[pl.BlockSpec((1, tm, d), lambda b, i: (b, i, 0)),
                   pl.BlockSpec((tm, d), lambda b, i: (b * nti + i, 0)),
                   pl.BlockSpec((TOP_K, tm), lambda b, i: (0, b * nti + i)),
                   pl.BlockSpec((TOP_K, tm), lambda b, i: (0, b * nti + i))],
        out_shape=[jax.ShapeDtypeStruct((bsz, s, d), F32),
                   jax.ShapeDtypeStruct((t, d), F32),
                   jax.ShapeDtypeStruct((TOP_K, t), I32),
                   jax.ShapeDtypeStruct((TOP_K, t), F32)],
        scratch_shapes=[pltpu.VMEM((tm, d), F32), pltpu.VMEM((d, 2 * LANES), BF16)],
        compiler_params=pltpu.CompilerParams(dimension_semantics=("arbitrary", "arbitrary"),
                                             vmem_limit_bytes=LARGE_VMEM_LIMIT),
        name="outproj",
    )(ycat, w_out_bf16, x, mod, gpost.reshape(1, d), gpre.reshape(1, d), wr, br)


def _slots_kernel(eid_ref, pos_ref, tab_ref, rank_scr, *, n_slots):
    t = eid_ref.shape[1]
    chunk = min(512, t)
    nchunk = t // chunk
    tri = jnp.where(lax.broadcasted_iota(I32, (chunk, chunk), 0)
                    <= lax.broadcasted_iota(I32, (chunk, chunk), 1), 1.0, 0.0).astype(BF16)
    e_io = lax.broadcasted_iota(I32, (N_EXPERTS, chunk), 0)

    cnt = jnp.zeros((N_EXPERTS, 1), F32)
    for k in range(TOP_K):
        def rank_body(c, carry, k=k):
            off = pl.multiple_of(c * chunk, chunk)
            onehot = e_io == eid_ref[pl.ds(k, 1), pl.ds(off, chunk)]
            ones = jnp.where(onehot, 1.0, 0.0)
            prefix = jnp.dot(ones.astype(BF16), tri, preferred_element_type=F32) + carry
            rank = jnp.sum(jnp.where(onehot, prefix, 0.0), axis=0, keepdims=True) - 1.0
            rank_scr[pl.ds(k, 1), pl.ds(off, chunk)] = rank
            return carry + jnp.sum(ones, axis=1, keepdims=True)
        cnt = lax.fori_loop(0, nchunk, rank_body, cnt)

    padded = jnp.floor((cnt + (SLOT_TILE - 1)) * (1.0 / SLOT_TILE)) * SLOT_TILE
    sub = lax.broadcasted_iota(I32, (N_EXPERTS, LANES), 0)
    lan = lax.broadcasted_iota(I32, (N_EXPERTS, LANES), 1)
    padded_row = jnp.sum(jnp.where(sub == lan, padded, 0.0), axis=0, keepdims=True)
    start = jnp.sum(jnp.where(lan < sub, padded_row, 0.0), axis=1, keepdims=True)
    end = start + padded
    pad_lo = start + cnt

    for k in range(TOP_K):
        def pos_body(c, carry, k=k):
            off = pl.multiple_of(c * chunk, chunk)
            onehot = e_io == eid_ref[pl.ds(k, 1), pl.ds(off, chunk)]
            base = jnp.sum(jnp.where(onehot, start + SLOT_TILE, 0.0), axis=0, keepdims=True)
            pos_ref[pl.ds(k, 1), pl.ds(off, chunk)] = (
                base + rank_scr[pl.ds(k, 1), pl.ds(off, chunk)]).astype(I32)
            return carry
        lax.fori_loop(0, nchunk, pos_body, 0)

    lo_row = jnp.sum(jnp.where(sub == lan, pad_lo, 0.0), axis=0, keepdims=True)
    hi_row = jnp.sum(jnp.where(sub + PAD_HI_LANE == lan, end, 0.0), axis=0, keepdims=True)
    first_row = jnp.sum(jnp.where(sub + TILE_START_LANE == lan, start * (1.0 / SLOT_TILE), 0.0),
                        axis=0, keepdims=True)
    lane_row = lax.broadcasted_iota(I32, (1, LANES), 1)
    total = jnp.sum(padded_row, axis=1, keepdims=True)
    tail = (jnp.where(lane_row == N_EXPERTS, total, 0.0)
            + jnp.where(lane_row == PAD_HI_LANE + N_EXPERTS, float(n_slots), 0.0)
            + jnp.where(lane_row == TILE_START_LANE + N_EXPERTS, total * (1.0 / SLOT_TILE), 0.0))
    tab_ref[...] = (lo_row + hi_row + first_row + tail).astype(I32)


def _slots(eid, n_tiles):
    t = eid.shape[1]
    return pl.pallas_call(
        functools.partial(_slots_kernel, n_slots=n_tiles * SLOT_TILE),
        out_shape=[jax.ShapeDtypeStruct((TOP_K, t), I32),
                   jax.ShapeDtypeStruct((1, LANES), I32)],
        scratch_shapes=[pltpu.VMEM((TOP_K, t), F32)],
        compiler_params=pltpu.CompilerParams(vmem_limit_bytes=VMEM_LIMIT),
        name="slots",
    )(eid)


def _invert_kernel(pos_ref, tab_ref, inv_ref, *, n_pairs, n_slots):
    group = 8
    spare = n_pairs
    for e in range(N_EXPERTS + 1):
        lo, hi = tab_ref[e], tab_ref[PAD_HI_LANE + e]

        def fill(i, c, lo=lo, spare=spare):
            for k in range(group):
                inv_ref[SLOT_TILE + lo + i * group + k] = spare + i * group + k
            return c
        lax.fori_loop(0, (hi - lo + group - 1) // group, fill, 0)
        spare = spare + hi - lo

    def guard(r, c):
        inv_ref[r] = n_slots + r
        inv_ref[SLOT_TILE + n_slots + r] = n_slots + r
        return c
    lax.fori_loop(0, SLOT_TILE, guard, 0, unroll=8)

    def place(n, c):
        inv_ref[pos_ref[n]] = n
        return c
    lax.fori_loop(0, n_pairs, place, 0, unroll=16)


def _invert(pos_flat, tab, n_tiles):
    n_pairs = pos_flat.shape[0]
    n_slots = n_tiles * SLOT_TILE
    return pl.pallas_call(
        functools.partial(_invert_kernel, n_pairs=n_pairs, n_slots=n_slots),
        in_specs=[pl.BlockSpec(memory_space=pltpu.SMEM), pl.BlockSpec(memory_space=pltpu.SMEM)],
        out_specs=pl.BlockSpec(memory_space=pltpu.SMEM),
        out_shape=jax.ShapeDtypeStruct((n_slots + 2 * SLOT_TILE,), I32),
        name="invert",
    )(pos_flat, tab)


def _experts_kernel(inv_ref, tab_ref, f_ref, wg_ref, wu_ref, wd_ref, y_ref,
                    wbg, wbu, wbd, xa, xb, ya, yb, gsem, ssem, zsem, *, n_tokens, n_slots):
    e = pl.program_id(0)
    j = pl.program_id(1)
    fc = wg_ref.shape[2]
    total = tab_ref[TILE_START_LANE + N_EXPERTS]

    def gather_copy(tile, r, xbuf):
        v = inv_ref[(tile + 1) * SLOT_TILE + r]
        return pltpu.make_async_copy(f_ref.at[pl.ds(v & (n_tokens - 1), 1)], xbuf.at[pl.ds(r, 1)], gsem)

    def scatter_copy(tile, r, ybuf):
        v = inv_ref[(tile + 1) * SLOT_TILE + r]
        return pltpu.make_async_copy(ybuf.at[pl.ds(r, 1)], y_ref.at[pl.ds(v, 1)], ssem)

    def gather_wait(xbuf):
        pltpu.make_async_copy(f_ref.at[pl.ds(0, SLOT_TILE)], xbuf, gsem).wait()

    def scatter_wait(ybuf):
        pltpu.make_async_copy(ybuf, y_ref.at[pl.ds(0, SLOT_TILE)], ssem).wait()

    def rows(fn):
        def body(r, c):
            fn(r)
            return c
        lax.fori_loop(0, SLOT_TILE, body, 0, unroll=8)

    @pl.when((e == 0) & (j == 0))
    def _():
        ya[...] = jnp.zeros(ya.shape, ya.dtype)
        yb[...] = jnp.zeros(yb.shape, yb.dtype)
        rows(lambda r: gather_copy(0, r, xa).start())

        def zero_copy(tile):
            dst = pl.multiple_of(tile * SLOT_TILE, SLOT_TILE)
            return pltpu.make_async_copy(ya, y_ref.at[pl.ds(dst, SLOT_TILE)], zsem)

        def zstart(tile, c):
            zero_copy(tile).start()
            return c

        def zwait(tile, c):
            zero_copy(tile).wait()
            return c
        lax.fori_loop(total, n_slots // SLOT_TILE, zstart, 0)
        lax.fori_loop(total, n_slots // SLOT_TILE, zwait, 0)

    @pl.when(e < N_EXPERTS)
    def _():
        slot = e % 2
        col = pl.multiple_of(j * fc, fc)
        wbg[slot, :, pl.ds(col, fc)] = wg_ref[0].astype(BF16)
        wbu[slot, :, pl.ds(col, fc)] = wu_ref[0].astype(BF16)
        wbd[slot, pl.ds(col, fc), :] = wd_ref[0].astype(BF16)

    prev = jnp.maximum(e - 1, 0)
    first = tab_ref[TILE_START_LANE + prev]
    count = jnp.where(e == 0, 0, tab_ref[TILE_START_LANE + prev + 1] - first)
    lo = first + (count * j) // N_WCHUNK
    hi = first + (count * (j + 1)) // N_WCHUNK
    wslot = prev % 2

    def tile(t, c):
        def run(xcur, xnext, ycur, yprev, ws):
            gather_wait(xcur)
            for r in range(SLOT_TILE):
                gather_copy(t + 1, r, xnext).start()
            for r in range(SLOT_TILE):
                scatter_copy(t - 1, r, yprev).start()
            x = xcur[...].astype(BF16)
            hg = jnp.dot(x, wbg[ws], preferred_element_type=F32)
            hu = jnp.dot(x, wbu[ws], preferred_element_type=F32)
            hid = (_silu(hg) * hu).astype(BF16)
            ycur[...] = jnp.dot(hid, wbd[ws], preferred_element_type=F32)
            scatter_wait(yprev)

        for ws in range(2):
            for parity, bufs in enumerate(((xa, xb, ya, yb), (xb, xa, yb, ya))):
                @pl.when((wslot == ws) & (t % 2 == parity))
                def _(bufs=bufs, ws=ws):
                    run(*bufs, ws)
        return c
    lax.fori_loop(lo, hi, tile, 0)

    @pl.when((e == N_EXPERTS) & (j == N_WCHUNK - 1))
    def _():
        last = total - 1

        def finish(xnext, ylast):
            gather_wait(xnext)
            rows(lambda r: scatter_copy(last, r, ylast).start())
            scatter_wait(ylast)

        @pl.when(last % 2 == 0)
        def _():
            finish(xb, ya)

        @pl.when(last % 2 == 1)
        def _():
            finish(xa, yb)


def _experts(inv, tab, f, wg, wu, wd, n_tiles):
    t, d = f.shape
    assert t & (t - 1) == 0
    fdim = wg.shape[2]
    fc = fdim // N_WCHUNK
    n_slots = n_tiles * SLOT_TILE

    def expert(e):
        return jnp.minimum(e, N_EXPERTS - 1)

    def chunk(e, j):
        return jnp.where(e == N_EXPERTS, N_WCHUNK - 1, j)

    return pl.pallas_call(
        functools.partial(_experts_kernel, n_tokens=t, n_slots=n_slots),
        grid_spec=pltpu.PrefetchScalarGridSpec(
            num_scalar_prefetch=2,
            grid=(N_EXPERTS + 1, N_WCHUNK),
            in_specs=[pl.BlockSpec(memory_space=pl.ANY),
                      pl.BlockSpec((1, d, fc), lambda e, j, inv, tab: (expert(e), 0, chunk(e, j))),
                      pl.BlockSpec((1, d, fc), lambda e, j, inv, tab: (expert(e), 0, chunk(e, j))),
                      pl.BlockSpec((1, fc, d), lambda e, j, inv, tab: (expert(e), chunk(e, j), 0))],
            out_specs=pl.BlockSpec(memory_space=pl.ANY),
            scratch_shapes=[pltpu.VMEM((2, d, fdim), BF16), pltpu.VMEM((2, d, fdim), BF16),
                            pltpu.VMEM((2, fdim, d), BF16),
                            pltpu.VMEM((SLOT_TILE, d), F32), pltpu.VMEM((SLOT_TILE, d), F32),
                            pltpu.VMEM((SLOT_TILE, d), F32), pltpu.VMEM((SLOT_TILE, d), F32),
                            pltpu.SemaphoreType.DMA, pltpu.SemaphoreType.DMA, pltpu.SemaphoreType.DMA]),
        out_shape=jax.ShapeDtypeStruct((n_slots + SLOT_TILE, d), F32),
        compiler_params=pltpu.CompilerParams(dimension_semantics=("arbitrary", "arbitrary"),
                                             vmem_limit_bytes=LARGE_VMEM_LIMIT),
        name="experts",
    )(inv, tab, f, wg, wu, wd)


def _final_kernel(y0_ref, y1_ref, h_ref, ew_ref, mod_ref, g_ref, o_ref):
    ew = ew_ref[...]
    y = ew[:, 0:1] * y0_ref[...] + ew[:, 1:2] * y1_ref[...]
    o_ref[0] = h_ref[0] + _rms(y, mod_ref[0][5:6] * g_ref[...])


def _final(y_pairs, h1, ew_t, mod, g):
    bsz, s, d = h1.shape
    t = bsz * s
    tg = min(FINAL_TILE, s)
    nti = s // tg
    return pl.pallas_call(
        _final_kernel,
        grid=(bsz, nti),
        in_specs=[pl.BlockSpec((tg, d), lambda b, i: (b * nti + i, 0)),
                  pl.BlockSpec((tg, d), lambda b, i: (t // tg + b * nti + i, 0)),
                  pl.BlockSpec((1, tg, d), lambda b, i: (b, i, 0)),
                  pl.BlockSpec((tg, TOP_K), lambda b, i: (b * nti + i, 0)),
                  pl.BlockSpec((1, N_MOD, d), lambda b, i: (b, 0, 0)),
                  pl.BlockSpec((1, d), lambda b, i: (0, 0))],
        out_specs=pl.BlockSpec((1, tg, d), lambda b, i: (b, i, 0)),
        out_shape=jax.ShapeDtypeStruct((bsz, s, d), F32),
        compiler_params=pltpu.CompilerParams(dimension_semantics=("arbitrary", "arbitrary"),
                                             vmem_limit_bytes=VMEM_LIMIT),
        name="final",
    )(y_pairs, y_pairs, h1, ew_t, mod, g.reshape(1, d))


def kernel(x, c, positions, w_ada, b_ada, g_mix_pre, g_mix_post, w_in, sink_logits, gmlp_norm_g,
           w_spatial, b_spatial, g_group_out, w_out, g_ffn_pre, g_ffn_post, w_router_group,
           b_router_group, w_router_expert, b_router_expert, w_gate, w_up, w_down):
    bsz, s, d = x.shape
    t = bsz * s
    depth = w_ada.shape[0]
    fdim = w_gate.shape[-1]
    n_tiles = (TOP_K * t) // SLOT_TILE + N_EXPERTS
    n_slots = n_tiles * SLOT_TILE
    cos, sin = _rope_tables(positions)
    h = x
    for l in range(depth):
        mod = _ada(c, w_ada[l], b_ada[l]).reshape(bsz, N_MOD, d)
        z = _inproj(h, mod, g_mix_pre[l], w_in[l].astype(BF16))
        ycat = _mixer(z, cos, sin, sink_logits[l], gmlp_norm_g[l], w_spatial[l],
                      b_spatial[l], g_group_out[l])
        wr = jnp.concatenate([w_router_group[l], w_router_expert[l].reshape(d, N_EXPERTS)], axis=1)
        wr = jnp.pad(wr, ((0, 0), (0, LANES - N_GROUPS - N_EXPERTS)))
        br = jnp.concatenate([b_router_group[l], b_router_expert[l].reshape(N_EXPERTS)])
        br = jnp.pad(br, (0, ROUTER_ROWS - N_GROUPS - N_EXPERTS)).reshape(ROUTER_ROWS, 1)
        h1, f, eid, ew = _outproj(ycat, w_out[l].astype(BF16), h, mod, g_mix_post[l], g_ffn_pre[l],
                                  wr, br)
        pos, tab = _slots(eid, n_tiles)
        tab = tab.reshape(LANES)
        inv = _invert(pos.reshape(TOP_K * t), tab, n_tiles)
        y_pairs = _experts(inv, tab, f,
                           w_gate[l].reshape(N_EXPERTS, d, fdim),
                           w_up[l].reshape(N_EXPERTS, d, fdim),
                           w_down[l].reshape(N_EXPERTS, fdim, d), n_tiles)
        h = _final(y_pairs, h1, ew.T, mod, g_ffn_post[l])
    return h
```

```python
import functools

import jax
import jax.numpy as jnp
from jax import lax
from jax.experimental import pallas as pl
from jax.experimental.pallas import tpu as pltpu

F32 = jnp.float32
BF16 = jnp.bfloat16
I32 = jnp.int32

HEAD_DIM = 128
ATTN_HEADS = 8
KV_HEADS = 2
Q_PER_KV = ATTN_HEADS // KV_HEADS
GMLP_HEADS = 8
WINDOW = 128
BLOCK = 128
ROPE_THETA = 500000.0
ROPE_DIM = HEAD_DIM // 4
ROPE_HALF = ROPE_DIM // 2
N_GROUPS = 4
EXPERTS_PER_GROUP = 4
N_EXPERTS = N_GROUPS * EXPERTS_PER_GROUP
TOP_K = 2
N_MOD = 6
EPS = 1e-6
NEG_INF = -1e30
LOG2_E = 1.4426950408889634
LANES = 128
BF16_SUBLANES = 16
PAD_HI_LANE = 32
TILE_START_LANE = 64
N_WCHUNK = 4
ROUTER_ROWS = 32
SLOT_TILE = 256

ADA_COL_TILE = 1024
INPROJ_ROW_TILE = 2048
INPROJ_COL_TILE = 512
INPROJ_SLICES = 4
MIXER_TILE = 1024
OUTPROJ_TILE = 512
FINAL_TILE = 512
VMEM_LIMIT = 52 * 1024 * 1024
LARGE_VMEM_LIMIT = 58 * 1024 * 1024


def _rms(x, g):
    ms = jnp.mean(x * x, axis=-1, keepdims=True)
    return x * lax.rsqrt(ms + EPS) * g


def _silu(x):
    return x / (1.0 + jnp.exp(-x))


def _gelu(x):
    return 0.5 * x * (1.0 + lax.erf(x * 0.7071067811865476))


def _ada_kernel(c_ref, w_ref, b_ref, o_ref):
    ca = _silu(c_ref[...]).astype(BF16)
    o_ref[...] = jnp.dot(ca, w_ref[...].astype(BF16), preferred_element_type=F32) + b_ref[...]


def _ada(c, w, b):
    bsz, d = c.shape
    n = w.shape[1]
    tn = ADA_COL_TILE
    return pl.pallas_call(
        _ada_kernel,
        grid=(n // tn,),
        in_specs=[pl.BlockSpec((bsz, d), lambda j: (0, 0)),
                  pl.BlockSpec((d, tn), lambda j: (0, j)),
                  pl.BlockSpec((1, tn), lambda j: (0, j))],
        out_specs=pl.BlockSpec((bsz, tn), lambda j: (0, j)),
        out_shape=jax.ShapeDtypeStruct((bsz, n), F32),
        compiler_params=pltpu.CompilerParams(dimension_semantics=("arbitrary",),
                                             vmem_limit_bytes=VMEM_LIMIT),
        name="ada",
    )(c, w, b.reshape(1, n))


def _inproj_kernel(x_ref, mod_ref, g_ref, w_ref, o_ref, a_scr, *, n_tiles):
    g = pl.program_id(0)
    j = pl.program_id(1)
    rows = x_ref.shape[1]

    @pl.when((g < n_tiles) & (j % 2 == 0) & (j < 2 * INPROJ_SLICES))
    def _():
        mod = mod_ref[0]
        a = _rms(x_ref[0], g_ref[...] * (1.0 + mod[1:2])) + mod[0:1]
        start = pl.multiple_of((j // 2) * rows, rows)
        a_scr[g % 2, pl.ds(start, rows), :] = a.astype(BF16)

    @pl.when(g > 0)
    def _():
        o_ref[0] = jnp.dot(a_scr[(g - 1) % 2], w_ref[...], preferred_element_type=F32).astype(BF16)


def _inproj(x, mod, g, w_bf16):
    bsz, s, d = x.shape
    n = w_bf16.shape[1]
    tm = min(INPROJ_ROW_TILE, s)
    tn = INPROJ_COL_TILE
    nj = n // tn
    nti = s // tm
    ntile = bsz * nti
    assert 2 * (INPROJ_SLICES - 1) < nj
    rows = tm // INPROJ_SLICES

    def norm_tile(g):
        gg = jnp.minimum(g, ntile - 1)
        return gg // nti, gg % nti

    def mm_tile(g):
        gg = jnp.maximum(g - 1, 0)
        return gg // nti, gg % nti

    def x_block(g, j):
        b, i = norm_tile(g)
        return b, i * INPROJ_SLICES + jnp.minimum(j // 2, INPROJ_SLICES - 1), 0

    return pl.pallas_call(
        functools.partial(_inproj_kernel, n_tiles=ntile),
        grid=(ntile + 1, nj),
        in_specs=[pl.BlockSpec((1, rows, d), x_block),
                  pl.BlockSpec((1, N_MOD, d), lambda g, j: (norm_tile(g)[0], 0, 0)),
                  pl.BlockSpec((1, d), lambda g, j: (0, 0)),
                  pl.BlockSpec((d, tn), lambda g, j: (0, j))],
        out_specs=pl.BlockSpec((1, tm, tn), lambda g, j: mm_tile(g) + (jnp.where(g == 0, 0, j),)),
        out_shape=jax.ShapeDtypeStruct((bsz, s, n), BF16),
        scratch_shapes=[pltpu.VMEM((2, tm, d), BF16)],
        compiler_params=pltpu.CompilerParams(dimension_semantics=("arbitrary", "arbitrary"),
                                             vmem_limit_bytes=VMEM_LIMIT),
        name="inproj",
    )(x, mod, g.reshape(1, d), w_bf16)


def _angles_kernel(pos_ref, invf_ref, cos_ref, sin_ref):
    rows = pos_ref.shape[0]
    per_row = LANES // ROPE_HALF
    ang = pos_ref[...].astype(F32) * invf_ref[...]
    src = lax.broadcasted_iota(I32, (LANES, LANES), 0)
    dst = lax.broadcasted_iota(I32, (LANES, LANES), 1)
    rotary_lane = lax.broadcasted_iota(I32, (1, LANES), 1) < ROPE_DIM
    for out_ref, val, fill in ((cos_ref, jnp.cos(ang), 1.0), (sin_ref, jnp.sin(ang), 0.0)):
        p0 = val.astype(BF16)
        r0 = val - p0.astype(F32)
        p1 = r0.astype(BF16)
        p2 = (r0 - p1.astype(F32)).astype(BF16)
        rest = jnp.where(rotary_lane, 0.0, fill)
        for a in range(per_row):
            sel = jnp.where((dst < ROPE_DIM) & ((dst & (ROPE_HALF - 1)) + a * ROPE_HALF == src),
                            1.0, 0.0).astype(BF16)
            spread = (jnp.dot(p0, sel, preferred_element_type=F32)
                      + jnp.dot(p1, sel, preferred_element_type=F32)
                      + jnp.dot(p2, sel, preferred_element_type=F32))
            out_ref[pl.ds(a, rows, stride=per_row), :] = spread + rest


def _rope_tables(positions):
    bsz, s = positions.shape
    per_row = LANES // ROPE_HALF
    rows = bsz * s // per_row
    inv = ROPE_THETA ** (-jnp.arange(ROPE_HALF, dtype=F32) * 2.0 / ROPE_DIM)
    invf = jnp.tile(inv, per_row).reshape(1, LANES)
    pos_rep = jnp.repeat(positions.reshape(rows, per_row), ROPE_HALF, axis=1)
    cos, sin = pl.pallas_call(
        _angles_kernel,
        out_shape=[jax.ShapeDtypeStruct((bsz * s, LANES), F32)] * 2,
        name="angles",
    )(pos_rep, invf)
    return cos.reshape(bsz, s, LANES), sin.reshape(bsz, s, LANES)


def _lane_mean(x):
    k = x.shape[1]
    ones = jnp.full((k, LANES), 1.0 / k, BF16)
    return jnp.dot(x.astype(BF16), ones, preferred_element_type=F32)


def _rope(x, cos, sin):
    lane = lax.broadcasted_iota(I32, x.shape, 1)
    partner = jnp.where(lane < ROPE_HALF,
                        pltpu.roll(x, HEAD_DIM - ROPE_HALF, 1),
                        pltpu.roll(x, ROPE_HALF, 1))
    return x * cos + partner * sin


def _mixer_kernel(sink_ref, q_ref, kvp_ref, kvc_ref, kvn_ref, u0_ref, u1_ref, v0_ref, v1_ref,
                  cosp_ref, cosc_ref, cosn_ref, sinp_ref, sinc_ref, sinn_ref,
                  gn_ref, ws_ref, bst_ref, gg_ref, o_ref, ya_scr, yg_scr):
    tq = q_ref.shape[1]
    nsub = tq // BLOCK
    aw = ATTN_HEADS * HEAD_DIM
    i = pl.program_id(1)
    sign = jnp.where(lax.broadcasted_iota(I32, (1, HEAD_DIM), 1) < ROPE_HALF, -1.0, 1.0)

    cos_c, sin_c = cosc_ref[0], sinc_ref[0] * sign
    tabs = ((cosp_ref[0], sinp_ref[0] * sign), (cos_c, sin_c), (cosn_ref[0], sinn_ref[0] * sign))
    scale = HEAD_DIM ** -0.5 * LOG2_E
    cos_q, sin_q = cos_c * scale, sin_c * scale

    def prepare_kv_head(h):
        kparts, vparts = [], []
        for ref, (cs, sn) in zip((kvp_ref, kvc_ref, kvn_ref), tabs):
            k = ref[0, :, h * HEAD_DIM:(h + 1) * HEAD_DIM].astype(F32)
            kparts.append(_rope(k, cs, sn).astype(BF16))
            vparts.append(ref[0, :, (KV_HEADS + h) * HEAD_DIM:(KV_HEADS + h + 1) * HEAD_DIM])
        kband = jnp.concatenate(kparts, axis=0)
        vband = jnp.concatenate(
            [jnp.concatenate(vparts, axis=0), jnp.ones((tq + 2 * BLOCK, HEAD_DIM), BF16)], axis=1)
        return kband, vband

    rows = Q_PER_KV * BLOCK
    band = 3 * BLOCK
    q_off = lax.broadcasted_iota(I32, (rows, BLOCK), 0) & (BLOCK - 1)
    k_off = lax.broadcasted_iota(I32, (rows, BLOCK), 1)
    bias_prev = jnp.where(k_off >= q_off, 0.0, NEG_INF)
    bias_next = jnp.where(k_off <= q_off, 0.0, NEG_INF)
    first_tile = i == 0
    last_tile = i == pl.num_programs(1) - 1
    def attention_block(s, h, kband, vband):
        bp = jnp.where(first_tile, NEG_INF, bias_prev) if s == 0 else bias_prev
        bn = jnp.where(last_tile, NEG_INF, bias_next) if s == nsub - 1 else bias_next
        blk = slice(s * BLOCK, (s + 1) * BLOCK)
        q4 = jnp.concatenate(
            [_rope(q_ref[0, blk, hq * HEAD_DIM:(hq + 1) * HEAD_DIM].astype(F32),
                   cos_q[blk], sin_q[blk]).astype(BF16)
             for hq in range(h * Q_PER_KV, (h + 1) * Q_PER_KV)], axis=0)
        kb = kband[s * BLOCK:s * BLOCK + band]
        vb = vband[s * BLOCK:s * BLOCK + band]
        sc = lax.dot_general(q4, kb, (((1,), (1,)), ((), ())), preferred_element_type=F32)
        sc = jnp.concatenate([sc[:, :BLOCK] + bp, sc[:, BLOCK:2 * BLOCK], sc[:, 2 * BLOCK:] + bn],
                             axis=1)
        sink = jnp.concatenate([jnp.full((BLOCK, 1), sink_ref[h * Q_PER_KV + g] * LOG2_E, F32)
                                for g in range(Q_PER_KV)], axis=0)
        m = jnp.maximum(jnp.max(sc, axis=-1, keepdims=True), sink)
        p = jnp.exp2(sc - m)
        pv = jnp.dot(p.astype(BF16), vb, preferred_element_type=F32)
        den = pv[:, HEAD_DIM:] + jnp.exp2(sink - m)
        o = pv[:, :HEAD_DIM] / den
        for g in range(Q_PER_KV):
            hq = h * Q_PER_KV + g
            ya_scr[s * BLOCK:(s + 1) * BLOCK, hq * HEAD_DIM:(hq + 1) * HEAD_DIM] = (
                o[g * BLOCK:(g + 1) * BLOCK])

    half_heads = GMLP_HEADS // 2

    def gmlp_head(h):
        u_ref = u0_ref if h < half_heads else u1_ref
        v_ref = v0_ref if h < half_heads else v1_ref
        hh = h % half_heads
        u = _gelu(u_ref[0, :, hh * HEAD_DIM:(hh + 1) * HEAD_DIM].astype(F32))
        v = _gelu(v_ref[0, :, hh * HEAD_DIM:(hh + 1) * HEAD_DIM].astype(F32))
        mu = _lane_mean(v)
        dv = v - mu
        var = _lane_mean(dv * dv)
        vn = (dv * lax.rsqrt(var + EPS) * gn_ref[h:h + 1, :]).astype(BF16)
        w = ws_ref[h].astype(BF16)
        bias = bst_ref[:, h:h + 1]
        for cidx in range(nsub):
            sl = slice(cidx * BLOCK, (cidx + 1) * BLOCK)
            mixed = jnp.dot(w, vn[sl], preferred_element_type=F32) + bias
            yg_scr[sl, h * HEAD_DIM:(h + 1) * HEAD_DIM] = u[sl] * mixed

    heads_per_block = -(-GMLP_HEADS // (KV_HEADS * nsub))
    next_head = 0
    for h in range(KV_HEADS):
        prepared = prepare_kv_head(h)
        for s in range(nsub):
            attention_block(s, h, *prepared)
            for _ in range(heads_per_block):
                if next_head < GMLP_HEADS:
                    gmlp_head(next_head)
                    next_head += 1
    for h in range(next_head, GMLP_HEADS):
        gmlp_head(h)

    for scr, lo in ((ya_scr, 0), (yg_scr, aw)):
        y = scr[...]
        width = y.shape[1]
        inv_rms = lax.rsqrt(_lane_mean(y * y) + EPS)
        inv_rms = jnp.concatenate([inv_rms] * (width // LANES), axis=1)
        o_ref[0, :, lo:lo + width] = (y * inv_rms * gg_ref[:, lo:lo + width]).astype(BF16)


def _mixer(z, cos, sin, sink, gn, ws, bs, gg):
    assert WINDOW == BLOCK
    bsz, s, _ = z.shape
    tq = min(MIXER_TILE, s)
    nsub = tq // BLOCK
    nblk = s // BLOCK
    aw = ATTN_HEADS * HEAD_DIM
    gw = GMLP_HEADS * HEAD_DIM
    cw = 512

    def prev_blk(b, i):
        return (b, jnp.maximum(i * nsub - 1, 0), 2)

    def next_blk(b, i):
        return (b, jnp.minimum((i + 1) * nsub, nblk - 1), 2)

    tab_specs = [pl.BlockSpec((1, BLOCK, HEAD_DIM), lambda b, i: prev_blk(b, i)[:2] + (0,)),
                 pl.BlockSpec((1, tq, HEAD_DIM), lambda b, i: (b, i, 0)),
                 pl.BlockSpec((1, BLOCK, HEAD_DIM), lambda b, i: next_blk(b, i)[:2] + (0,))]
    return pl.pallas_call(
        _mixer_kernel,
        grid=(bsz, s // tq),
        in_specs=[pl.BlockSpec(memory_space=pltpu.SMEM),
                  pl.BlockSpec((1, tq, aw), lambda b, i: (b, i, 0)),
                  pl.BlockSpec((1, BLOCK, cw), prev_blk),
                  pl.BlockSpec((1, tq, cw), lambda b, i: (b, i, 2)),
                  pl.BlockSpec((1, BLOCK, cw), next_blk),
                  pl.BlockSpec((1, tq, cw), lambda b, i: (b, i, 3)),
                  pl.BlockSpec((1, tq, cw), lambda b, i: (b, i, 4)),
                  pl.BlockSpec((1, tq, cw), lambda b, i: (b, i, 5)),
                  pl.BlockSpec((1, tq, cw), lambda b, i: (b, i, 6)),
                  *tab_specs, *tab_specs,
                  pl.BlockSpec((GMLP_HEADS, HEAD_DIM), lambda b, i: (0, 0)),
                  pl.BlockSpec((GMLP_HEADS, BLOCK, BLOCK), lambda b, i: (0, 0, 0)),
                  pl.BlockSpec((BLOCK, GMLP_HEADS), lambda b, i: (0, 0)),
                  pl.BlockSpec((1, aw + gw), lambda b, i: (0, 0))],
        out_specs=pl.BlockSpec((1, tq, aw + gw), lambda b, i: (b, i, 0)),
        out_shape=jax.ShapeDtypeStruct((bsz, s, aw + gw), BF16),
        scratch_shapes=[pltpu.VMEM((tq, aw), F32), pltpu.VMEM((tq, gw), F32)],
        compiler_params=pltpu.CompilerParams(dimension_semantics=("arbitrary", "arbitrary"),
                                             vmem_limit_bytes=VMEM_LIMIT),
        name="mixer",
    )(sink, z, z, z, z, z, z, z, z, cos, cos, cos, sin, sin, sin, gn, ws, bs.T,
      gg.reshape(1, aw + gw))


def _split_bf16(x):
    hi = x.astype(BF16)
    lo = (x - hi.astype(F32)).astype(BF16)
    return hi, lo


def _outproj_kernel(y_ref, w_ref, x_ref, mod_ref, gpost_ref, gpre_ref, wr_ref, br_ref,
                    h_ref, f_ref, eid_ref, ew_ref, mix, wr_split):
    @pl.when((pl.program_id(0) == 0) & (pl.program_id(1) == 0))
    def _():
        w_hi, w_lo = _split_bf16(wr_ref[...])
        wr_split[:, :LANES] = w_hi
        wr_split[:, LANES:] = w_lo

    mix[...] = jnp.dot(y_ref[0], w_ref[...], preferred_element_type=F32)
    for stage in _outproj_finish(mix, x_ref, mod_ref, gpost_ref, gpre_ref, wr_split, br_ref,
                                 h_ref, f_ref, eid_ref, ew_ref):
        stage()


def _outproj_finish(mix_ref, x_ref, mod_ref, gpost_ref, gpre_ref, wr_ref, br_ref,
                    h_ref, f_ref, eid_ref, ew_ref):
    state = {}

    def residual():
        h1 = x_ref[0] + _rms(mix_ref[...], mod_ref[0][2:3] * gpost_ref[...])
        h_ref[0] = h1
        state["h1"] = h1

    def prenorm():
        mod = mod_ref[0]
        f = _rms(state["h1"], gpre_ref[...] * (1.0 + mod[4:5])) + mod[3:4]
        f_ref[...] = f
        state["f"] = f

    def logits():
        f_hi, f_lo = _split_bf16(state["f"])
        tm = f_hi.shape[0]
        r = jnp.dot(jnp.concatenate([f_hi, f_lo], axis=0), wr_ref[...],
                    preferred_element_type=F32)
        lg = (r[:tm, :LANES] + r[:tm, LANES:]) + (r[tm:, :LANES] + r[tm:, LANES:])
        state["logits"] = lg.T[:ROUTER_ROWS] + br_ref[...]

    def route():
        _route(state["logits"], eid_ref, ew_ref)

    return [residual, prenorm, logits, route]


def _route(logits, eid_ref, ew_ref):
    gl = [logits[g:g + 1] for g in range(N_GROUPS)]
    gmax = functools.reduce(jnp.maximum, gl)
    gidx = jnp.full(gmax.shape, N_GROUPS - 1, I32)
    for g in range(N_GROUPS - 2, -1, -1):
        gidx = jnp.where(gl[g] == gmax, g, gidx)
    gval = 1.0 / functools.reduce(lambda a, b: a + b, [jnp.exp(v - gmax) for v in gl])

    es = []
    for e in range(EXPERTS_PER_GROUP):
        r = N_GROUPS + (N_GROUPS - 1) * EXPERTS_PER_GROUP + e
        v = logits[r:r + 1]
        for g in range(N_GROUPS - 2, -1, -1):
            r = N_GROUPS + g * EXPERTS_PER_GROUP + e
            v = jnp.where(gidx == g, logits[r:r + 1], v)
        es.append(v)
    m1 = functools.reduce(jnp.maximum, es)
    i1 = jnp.full(m1.shape, EXPERTS_PER_GROUP - 1, I32)
    for e in range(EXPERTS_PER_GROUP - 2, -1, -1):
        i1 = jnp.where(es[e] == m1, e, i1)
    rest = [jnp.where(i1 == e, -jnp.inf, es[e]) for e in range(EXPERTS_PER_GROUP)]
    m2 = functools.reduce(jnp.maximum, rest)
    i2 = jnp.full(m2.shape, EXPERTS_PER_GROUP - 1, I32)
    for e in range(EXPERTS_PER_GROUP - 2, -1, -1):
        i2 = jnp.where(rest[e] == m2, e, i2)
    p2 = jnp.exp(m2 - m1)
    w1 = gval / (1.0 + p2)
    w2 = gval * p2 / (1.0 + p2)
    eid_ref[0:1, :] = gidx * EXPERTS_PER_GROUP + i1
    eid_ref[1:2, :] = gidx * EXPERTS_PER_GROUP + i2
    ew_ref[0:1, :] = w1
    ew_ref[1:2, :] = w2


def _outproj(ycat, w_out_bf16, x, mod, gpost, gpre, wr, br):
    bsz, s, d = x.shape
    t = bsz * s
    tm = min(OUTPROJ_TILE, s)
    nti = s // tm
    return pl.pallas_call(
        _outproj_kernel,
        grid=(bsz, nti),
        in_specs=[pl.BlockSpec((1, tm, d), lambda b, i: (b, i, 0)),
                  pl.BlockSpec((d, d), lambda b, i: (0, 0)),
                  pl.BlockSpec((1, tm, d), lambda b, i: (b, i, 0)),
                  pl.BlockSpec((1, N_MOD, d), lambda b, i: (b, 0, 0)),
                  pl.BlockSpec((1, d), lambda b, i: (0, 0)),
                  pl.BlockSpec((1, d), lambda b, i: (0, 0)),
                  pl.BlockSpec((d, LANES), lambda b, i: (0, 0)),
                  pl.BlockSpec((ROUTER_ROWS, 1), lambda b, i: (0, 0))],
        out_specs=[pl.BlockSpec((1, tm, d), lambda b, i: (b, i, 0)),
                   pl.BlockSpec((tm, d), lambda b, i: (b * nti + i, 0)),
                   pl.BlockSpec((TOP_K, tm), lambda b, i: (0, b * nti + i)),
                   pl.BlockSpec((TOP_K, tm), lambda b, i: (0, b * nti + i))],
        out_shape=[jax.ShapeDtypeStruct((bsz, s, d), F32),
                   jax.ShapeDtypeStruct((t, d), F32),
                   jax.ShapeDtypeStruct((TOP_K, t), I32),
                   jax.ShapeDtypeStruct((TOP_K, t), F32)],
        scratch_shapes=[pltpu.VMEM((tm, d), F32), pltpu.VMEM((d, 2 * LANES), BF16)],
        compiler_params=pltpu.CompilerParams(dimension_semantics=("arbitrary", "arbitrary"),
                                             vmem_limit_bytes=LARGE_VMEM_LIMIT),
        name="outproj",
    )(ycat, w_out_bf16, x, mod, gpost.reshape(1, d), gpre.reshape(1, d), wr, br)


def _slots_kernel(eid_ref, pos_ref, tab_ref, rank_scr, *, n_slots):
    t = eid_ref.shape[1]
    chunk = min(512, t)
    nchunk = t // chunk
    tri = jnp.where(lax.broadcasted_iota(I32, (chunk, chunk), 0)
                    <= lax.broadcasted_iota(I32, (chunk, chunk), 1), 1.0, 0.0).astype(BF16)
    e_io = lax.broadcasted_iota(I32, (N_EXPERTS, chunk), 0)

    cnt = jnp.zeros((N_EXPERTS, 1), F32)
    for k in range(TOP_K):
        def rank_body(c, carry, k=k):
            off = pl.multiple_of(c * chunk, chunk)
            onehot = e_io == eid_ref[pl.ds(k, 1), pl.ds(off, chunk)]
            ones = jnp.where(onehot, 1.0, 0.0)
            prefix = jnp.dot(ones.astype(BF16), tri, preferred_element_type=F32) + carry
            rank = jnp.sum(jnp.where(onehot, prefix, 0.0), axis=0, keepdims=True) - 1.0
            rank_scr[pl.ds(k, 1), pl.ds(off, chunk)] = rank
            return carry + jnp.sum(ones, axis=1, keepdims=True)
        cnt = lax.fori_loop(0, nchunk, rank_body, cnt)

    padded = jnp.floor((cnt + (SLOT_TILE - 1)) * (1.0 / SLOT_TILE)) * SLOT_TILE
    sub = lax.broadcasted_iota(I32, (N_EXPERTS, LANES), 0)
    lan = lax.broadcasted_iota(I32, (N_EXPERTS, LANES), 1)
    padded_row = jnp.sum(jnp.where(sub == lan, padded, 0.0), axis=0, keepdims=True)
    start = jnp.sum(jnp.where(lan < sub, padded_row, 0.0), axis=1, keepdims=True)
    end = start + padded
    pad_lo = start + cnt

    for k in range(TOP_K):
        def pos_body(c, carry, k=k):
            off = pl.multiple_of(c * chunk, chunk)
            onehot = e_io == eid_ref[pl.ds(k, 1), pl.ds(off, chunk)]
            base = jnp.sum(jnp.where(onehot, start + SLOT_TILE, 0.0), axis=0, keepdims=True)
            pos_ref[pl.ds(k, 1), pl.ds(off, chunk)] = (
                base + rank_scr[pl.ds(k, 1), pl.ds(off, chunk)]).astype(I32)
            return carry
        lax.fori_loop(0, nchunk, pos_body, 0)

    lo_row = jnp.sum(jnp.where(sub == lan, pad_lo, 0.0), axis=0, keepdims=True)
    hi_row = jnp.sum(jnp.where(sub + PAD_HI_LANE == lan, end, 0.0), axis=0, keepdims=True)
    first_row = jnp.sum(jnp.where(sub + TILE_START_LANE == lan, start * (1.0 / SLOT_TILE), 0.0),
                        axis=0, keepdims=True)
    lane_row = lax.broadcasted_iota(I32, (1, LANES), 1)
    total = jnp.sum(padded_row, axis=1, keepdims=True)
    tail = (jnp.where(lane_row == N_EXPERTS, total, 0.0)
            + jnp.where(lane_row == PAD_HI_LANE + N_EXPERTS, float(n_slots), 0.0)
            + jnp.where(lane_row == TILE_START_LANE + N_EXPERTS, total * (1.0 / SLOT_TILE), 0.0))
    tab_ref[...] = (lo_row + hi_row + first_row + tail).astype(I32)


def _slots(eid, n_tiles):
    t = eid.shape[1]
    return pl.pallas_call(
        functools.partial(_slots_kernel, n_slots=n_tiles * SLOT_TILE),
        out_shape=[jax.ShapeDtypeStruct((TOP_K, t), I32),
                   jax.ShapeDtypeStruct((1, LANES), I32)],
        scratch_shapes=[pltpu.VMEM((TOP_K, t), F32)],
        compiler_params=pltpu.CompilerParams(vmem_limit_bytes=VMEM_LIMIT),
        name="slots",
    )(eid)


def _invert_kernel(pos_ref, tab_ref, inv_ref, *, n_pairs, n_slots):
    group = 8
    spare = n_pairs
    for e in range(N_EXPERTS + 1):
        lo, hi = tab_ref[e], tab_ref[PAD_HI_LANE + e]

        def fill(i, c, lo=lo, spare=spare):
            for k in range(group):
                inv_ref[SLOT_TILE + lo + i * group + k] = spare + i * group + k
            return c
        lax.fori_loop(0, (hi - lo + group - 1) // group, fill, 0)
        spare = spare + hi - lo

    def guard(r, c):
        inv_ref[r] = n_slots + r
        inv_ref[SLOT_TILE + n_slots + r] = n_slots + r
        return c
    lax.fori_loop(0, SLOT_TILE, guard, 0, unroll=8)

    def place(n, c):
        inv_ref[pos_ref[n]] = n
        return c
    lax.fori_loop(0, n_pairs, place, 0, unroll=16)


def _invert(pos_flat, tab, n_tiles):
    n_pairs = pos_flat.shape[0]
    n_slots = n_tiles * SLOT_TILE
    return pl.pallas_call(
        functools.partial(_invert_kernel, n_pairs=n_pairs, n_slots=n_slots),
        in_specs=[pl.BlockSpec(memory_space=pltpu.SMEM), pl.BlockSpec(memory_space=pltpu.SMEM)],
        out_specs=pl.BlockSpec(memory_space=pltpu.SMEM),
        out_shape=jax.ShapeDtypeStruct((n_slots + 2 * SLOT_TILE,), I32),
        name="invert",
    )(pos_flat, tab)


def _experts_kernel(inv_ref, tab_ref, f_ref, wg_ref, wu_ref, wd_ref, y_ref,
                    wbg, wbu, wbd, xa, xb, ya, yb, gsem, ssem, zsem, *, n_tokens, n_slots):
    e = pl.program_id(0)
    j = pl.program_id(1)
    fc = wg_ref.shape[2]
    total = tab_ref[TILE_START_LANE + N_EXPERTS]

    def gather_copy(tile, r, xbuf):
        v = inv_ref[(tile + 1) * SLOT_TILE + r]
        return pltpu.make_async_copy(f_ref.at[pl.ds(v & (n_tokens - 1), 1)], xbuf.at[pl.ds(r, 1)], gsem)

    def scatter_copy(tile, r, ybuf):
        v = inv_ref[(tile + 1) * SLOT_TILE + r]
        return pltpu.make_async_copy(ybuf.at[pl.ds(r, 1)], y_ref.at[pl.ds(v, 1)], ssem)

    def gather_wait(xbuf):
        pltpu.make_async_copy(f_ref.at[pl.ds(0, SLOT_TILE)], xbuf, gsem).wait()

    def scatter_wait(ybuf):
        pltpu.make_async_copy(ybuf, y_ref.at[pl.ds(0, SLOT_TILE)], ssem).wait()

    def rows(fn):
        def body(r, c):
            fn(r)
            return c
        lax.fori_loop(0, SLOT_TILE, body, 0, unroll=8)

    @pl.when((e == 0) & (j == 0))
    def _():
        ya[...] = jnp.zeros(ya.shape, ya.dtype)
        yb[...] = jnp.zeros(yb.shape, yb.dtype)
        rows(lambda r: gather_copy(0, r, xa).start())

        def zero_copy(tile):
            dst = pl.multiple_of(tile * SLOT_TILE, SLOT_TILE)
            return pltpu.make_async_copy(ya, y_ref.at[pl.ds(dst, SLOT_TILE)], zsem)

        def zstart(tile, c):
            zero_copy(tile).start()
            return c

        def zwait(tile, c):
            zero_copy(tile).wait()
            return c
        lax.fori_loop(total, n_slots // SLOT_TILE, zstart, 0)
        lax.fori_loop(total, n_slots // SLOT_TILE, zwait, 0)

    @pl.when(e < N_EXPERTS)
    def _():
        slot = e % 2
        col = pl.multiple_of(j * fc, fc)
        wbg[slot, :, pl.ds(col, fc)] = wg_ref[0].astype(BF16)
        wbu[slot, :, pl.ds(col, fc)] = wu_ref[0].astype(BF16)
        wbd[slot, pl.ds(col, fc), :] = wd_ref[0].astype(BF16)

    prev = jnp.maximum(e - 1, 0)
    first = tab_ref[TILE_START_LANE + prev]
    count = jnp.where(e == 0, 0, tab_ref[TILE_START_LANE + prev + 1] - first)
    lo = first + (count * j) // N_WCHUNK
    hi = first + (count * (j + 1)) // N_WCHUNK
    wslot = prev % 2

    def tile(t, c):
        def run(xcur, xnext, ycur, yprev, ws):
            gather_wait(xcur)
            for r in range(SLOT_TILE):
                gather_copy(t + 1, r, xnext).start()
            for r in range(SLOT_TILE):
                scatter_copy(t - 1, r, yprev).start()
            x = xcur[...].astype(BF16)
            hg = jnp.dot(x, wbg[ws], preferred_element_type=F32)
            hu = jnp.dot(x, wbu[ws], preferred_element_type=F32)
            hid = (_silu(hg) * hu).astype(BF16)
            ycur[...] = jnp.dot(hid, wbd[ws], preferred_element_type=F32)
            scatter_wait(yprev)

        for ws in range(2):
            for parity, bufs in enumerate(((xa, xb, ya, yb), (xb, xa, yb, ya))):
                @pl.when((wslot == ws) & (t % 2 == parity))
                def _(bufs=bufs, ws=ws):
                    run(*bufs, ws)
        return c
    lax.fori_loop(lo, hi, tile, 0)

    @pl.when((e == N_EXPERTS) & (j == N_WCHUNK - 1))
    def _():
        last = total - 1

        def finish(xnext, ylast):
            gather_wait(xnext)
            rows(lambda r: scatter_copy(last, r, ylast).start())
            scatter_wait(ylast)

        @pl.when(last % 2 == 0)
        def _():
            finish(xb, ya)

        @pl.when(last % 2 == 1)
        def _():
            finish(xa, yb)


def _experts(inv, tab, f, wg, wu, wd, n_tiles):
    t, d = f.shape
    assert t & (t - 1) == 0
    fdim = wg.shape[2]
    fc = fdim // N_WCHUNK
    n_slots = n_tiles * SLOT_TILE

    def expert(e):
        return jnp.minimum(e, N_EXPERTS - 1)

    def chunk(e, j):
        return jnp.where(e == N_EXPERTS, N_WCHUNK - 1, j)

    return pl.pallas_call(
        functools.partial(_experts_kernel, n_tokens=t, n_slots=n_slots),
        grid_spec=pltpu.PrefetchScalarGridSpec(
            num_scalar_prefetch=2,
            grid=(N_EXPERTS + 1, N_WCHUNK),
            in_specs=[pl.BlockSpec(memory_space=pl.ANY),
                      pl.BlockSpec((1, d, fc), lambda e, j, inv, tab: (expert(e), 0, chunk(e, j))),
                      pl.BlockSpec((1, d, fc), lambda e, j, inv, tab: (expert(e), 0, chunk(e, j))),
                      pl.BlockSpec((1, fc, d), lambda e, j, inv, tab: (expert(e), chunk(e, j), 0))],
            out_specs=pl.BlockSpec(memory_space=pl.ANY),
            scratch_shapes=[pltpu.VMEM((2, d, fdim), BF16), pltpu.VMEM((2, d, fdim), BF16),
                            pltpu.VMEM((2, fdim, d), BF16),
                            pltpu.VMEM((SLOT_TILE, d), F32), pltpu.VMEM((SLOT_TILE, d), F32),
                            pltpu.VMEM((SLOT_TILE, d), F32), pltpu.VMEM((SLOT_TILE, d), F32),
                            pltpu.SemaphoreType.DMA, pltpu.SemaphoreType.DMA, pltpu.SemaphoreType.DMA]),
        out_shape=jax.ShapeDtypeStruct((n_slots + SLOT_TILE, d), F32),
        compiler_params=pltpu.CompilerParams(dimension_semantics=("arbitrary", "arbitrary"),
                                             vmem_limit_bytes=LARGE_VMEM_LIMIT),
        name="experts",
    )(inv, tab, f, wg, wu, wd)


def _final_kernel(y0_ref, y1_ref, h_ref, ew_ref, mod_ref, g_ref, o_ref):
    ew = ew_ref[...]
    y = ew[:, 0:1] * y0_ref[...] + ew[:, 1:2] * y1_ref[...]
    o_ref[0] = h_ref[0] + _rms(y, mod_ref[0][5:6] * g_ref[...])


def _final(y_pairs, h1, ew_t, mod, g):
    bsz, s, d = h1.shape
    t = bsz * s
    tg = min(FINAL_TILE, s)
    nti = s // tg
    return pl.pallas_call(
        _final_kernel,
        grid=(bsz, nti),
        in_specs=[pl.BlockSpec((tg, d), lambda b, i: (b * nti + i, 0)),
                  pl.BlockSpec((tg, d), lambda b, i: (t // tg + b * nti + i, 0)),
                  pl.BlockSpec((1, tg, d), lambda b, i: (b, i, 0)),
                  pl.BlockSpec((tg, TOP_K), lambda b, i: (b * nti + i, 0)),
                  pl.BlockSpec((1, N_MOD, d), lambda b, i: (b, 0, 0)),
                  pl.BlockSpec((1, d), lambda b, i: (0, 0))],
        out_specs=pl.BlockSpec((1, tg, d), lambda b, i: (b, i, 0)),
        out_shape=jax.ShapeDtypeStruct((bsz, s, d), F32),
        compiler_params=pltpu.CompilerParams(dimension_semantics=("arbitrary", "arbitrary"),
                                             vmem_limit_bytes=VMEM_LIMIT),
        name="final",
    )(y_pairs, y_pairs, h1, ew_t, mod, g.reshape(1, d))


def kernel(x, c, positions, w_ada, b_ada, g_mix_pre, g_mix_post, w_in, sink_logits, gmlp_norm_g,
           w_spatial, b_spatial, g_group_out, w_out, g_ffn_pre, g_ffn_post, w_router_group,
           b_router_group, w_router_expert, b_router_expert, w_gate, w_up, w_down):
    bsz, s, d = x.shape
    t = bsz * s
    depth = w_ada.shape[0]
    fdim = w_gate.shape[-1]
    n_tiles = (TOP_K * t) // SLOT_TILE + N_EXPERTS
    n_slots = n_tiles * SLOT_TILE
    cos, sin = _rope_tables(positions)
    h = x
    for l in range(depth):
        mod = _ada(c, w_ada[l], b_ada[l]).reshape(bsz, N_MOD, d)
        z = _inproj(h, mod, g_mix_pre[l], w_in[l].astype(BF16))
        ycat = _mixer(z, cos, sin, sink_logits[l], gmlp_norm_g[l], w_spatial[l],
                      b_spatial[l], g_group_out[l])
        wr = jnp.concatenate([w_router_group[l], w_router_expert[l].reshape(d, N_EXPERTS)], axis=1)
        wr = jnp.pad(wr, ((0, 0), (0, LANES - N_GROUPS - N_EXPERTS)))
        br = jnp.concatenate([b_router_group[l], b_router_expert[l].reshape(N_EXPERTS)])
        br = jnp.pad(br, (0, ROUTER_ROWS - N_GROUPS - N_EXPERTS)).reshape(ROUTER_ROWS, 1)
        h1, f, eid, ew = _outproj(ycat, w_out[l].astype(BF16), h, mod, g_mix_post[l], g_ffn_pre[l],
                                  wr, br)
        pos, tab = _slots(eid, n_tiles)
        tab = tab.reshape(LANES)
        inv = _invert(pos.reshape(TOP_K * t), tab, n_tiles)
        y_pairs = _experts(inv, tab, f,
                           w_gate[l].reshape(N_EXPERTS, d, fdim),
                           w_up[l].reshape(N_EXPERTS, d, fdim),
                           w_down[l].reshape(N_EXPERTS, fdim, d), n_tiles)
        h = _final(y_pairs, h1, ew.T, mod, g_ffn_post[l])
    return h
```

```python
import functools

import jax
import jax.numpy as jnp
from jax import lax
from jax.experimental import pallas as pl
from jax.experimental.pallas import tpu as pltpu

F32 = jnp.float32
BF16 = jnp.bfloat16
I32 = jnp.int32

HEAD_DIM = 128
ATTN_HEADS = 8
KV_HEADS = 2
Q_PER_KV = ATTN_HEADS // KV_HEADS
GMLP_HEADS = 8
WINDOW = 128
BLOCK = 128
ROPE_THETA = 500000.0
ROPE_DIM = HEAD_DIM // 4
ROPE_HALF = ROPE_DIM // 2
N_GROUPS = 4
EXPERTS_PER_GROUP = 4
N_EXPERTS = N_GROUPS * EXPERTS_PER_GROUP
TOP_K = 2
N_MOD = 6
EPS = 1e-6
NEG_INF = -1e30
LOG2_E = 1.4426950408889634
LANES = 128
BF16_SUBLANES = 16
PAD_HI_LANE = 32
TILE_START_LANE = 64
N_WCHUNK = 4
ROUTER_ROWS = 32
SLOT_TILE = 256

ADA_COL_TILE = 1024
INPROJ_ROW_TILE = 2048
INPROJ_COL_TILE = 512
INPROJ_SLICES = 4
MIXER_TILE = 1024
OUTPROJ_TILE = 512
FINAL_TILE = 512
VMEM_LIMIT = 52 * 1024 * 1024
LARGE_VMEM_LIMIT = 58 * 1024 * 1024


def _rms(x, g):
    ms = jnp.mean(x * x, axis=-1, keepdims=True)
    return x * lax.rsqrt(ms + EPS) * g


def _silu(x):
    return x / (1.0 + jnp.exp(-x))


def _gelu(x):
    return 0.5 * x * (1.0 + lax.erf(x * 0.7071067811865476))


def _ada_kernel(c_ref, w_ref, b_ref, o_ref):
    ca = _silu(c_ref[...]).astype(BF16)
    o_ref[...] = jnp.dot(ca, w_ref[...].astype(BF16), preferred_element_type=F32) + b_ref[...]


def _ada(c, w, b):
    bsz, d = c.shape
    n = w.shape[1]
    tn = ADA_COL_TILE
    return pl.pallas_call(
        _ada_kernel,
        grid=(n // tn,),
        in_specs=[pl.BlockSpec((bsz, d), lambda j: (0, 0)),
                  pl.BlockSpec((d, tn), lambda j: (0, j)),
                  pl.BlockSpec((1, tn), lambda j: (0, j))],
        out_specs=pl.BlockSpec((bsz, tn), lambda j: (0, j)),
        out_shape=jax.ShapeDtypeStruct((bsz, n), F32),
        compiler_params=pltpu.CompilerParams(dimension_semantics=("arbitrary",),
                                             vmem_limit_bytes=VMEM_LIMIT),
        name="ada",
    )(c, w, b.reshape(1, n))


def _inproj_kernel(x_ref, mod_ref, g_ref, w_ref, o_ref, a_scr, *, n_tiles):
    g = pl.program_id(0)
    j = pl.program_id(1)
    rows = x_ref.shape[1]

    @pl.when((g < n_tiles) & (j % 2 == 0) & (j < 2 * INPROJ_SLICES))
    def _():
        mod = mod_ref[0]
        a = _rms(x_ref[0], g_ref[...] * (1.0 + mod[1:2])) + mod[0:1]
        start = pl.multiple_of((j // 2) * rows, rows)
        a_scr[g % 2, pl.ds(start, rows), :] = a.astype(BF16)

    @pl.when(g > 0)
    def _():
        o_ref[0] = jnp.dot(a_scr[(g - 1) % 2], w_ref[...], preferred_element_type=F32).astype(BF16)


def _inproj(x, mod, g, w_bf16):
    bsz, s, d = x.shape
    n = w_bf16.shape[1]
    tm = min(INPROJ_ROW_TILE, s)
    tn = INPROJ_COL_TILE
    nj = n // tn
    nti = s // tm
    ntile = bsz * nti
    assert 2 * (INPROJ_SLICES - 1) < nj
    rows = tm // INPROJ_SLICES

    def norm_tile(g):
        gg = jnp.minimum(g, ntile - 1)
        return gg // nti, gg % nti

    def mm_tile(g):
        gg = jnp.maximum(g - 1, 0)
        return gg // nti, gg % nti

    def x_block(g, j):
        b, i = norm_tile(g)
        return b, i * INPROJ_SLICES + jnp.minimum(j // 2, INPROJ_SLICES - 1), 0

    return pl.pallas_call(
        functools.partial(_inproj_kernel, n_tiles=ntile),
        grid=(ntile + 1, nj),
        in_specs=[pl.BlockSpec((1, rows, d), x_block),
                  pl.BlockSpec((1, N_MOD, d), lambda g, j: (norm_tile(g)[0], 0, 0)),
                  pl.BlockSpec((1, d), lambda g, j: (0, 0)),
                  pl.BlockSpec((d, tn), lambda g, j: (0, j))],
        out_specs=pl.BlockSpec((1, tm, tn), lambda g, j: mm_tile(g) + (jnp.where(g == 0, 0, j),)),
        out_shape=jax.ShapeDtypeStruct((bsz, s, n), BF16),
        scratch_shapes=[pltpu.VMEM((2, tm, d), BF16)],
        compiler_params=pltpu.CompilerParams(dimension_semantics=("arbitrary", "arbitrary"),
                                             vmem_limit_bytes=VMEM_LIMIT),
        name="inproj",
    )(x, mod, g.reshape(1, d), w_bf16)


def _angles_kernel(pos_ref, invf_ref, cos_ref, sin_ref):
    rows = pos_ref.shape[0]
    per_row = LANES // ROPE_HALF
    ang = pos_ref[...].astype(F32) * invf_ref[...]
    src = lax.broadcasted_iota(I32, (LANES, LANES), 0)
    dst = lax.broadcasted_iota(I32, (LANES, LANES), 1)
    rotary_lane = lax.broadcasted_iota(I32, (1, LANES), 1) < ROPE_DIM
    for out_ref, val, fill in ((cos_ref, jnp.cos(ang), 1.0), (sin_ref, jnp.sin(ang), 0.0)):
        p0 = val.astype(BF16)
        r0 = val - p0.astype(F32)
        p1 = r0.astype(BF16)
        p2 = (r0 - p1.astype(F32)).astype(BF16)
        rest = jnp.where(rotary_lane, 0.0, fill)
        for a in range(per_row):
            sel = jnp.where((dst < ROPE_DIM) & ((dst & (ROPE_HALF - 1)) + a * ROPE_HALF == src),
                            1.0, 0.0).astype(BF16)
            spread = (jnp.dot(p0, sel, preferred_element_type=F32)
                      + jnp.dot(p1, sel, preferred_element_type=F32)
                      + jnp.dot(p2, sel, preferred_element_type=F32))
            out_ref[pl.ds(a, rows, stride=per_row), :] = spread + rest


def _rope_tables(positions):
    bsz, s = positions.shape
    per_row = LANES // ROPE_HALF
    rows = bsz * s // per_row
    inv = ROPE_THETA ** (-jnp.arange(ROPE_HALF, dtype=F32) * 2.0 / ROPE_DIM)
    invf = jnp.tile(inv, per_row).reshape(1, LANES)
    pos_rep = jnp.repeat(positions.reshape(rows, per_row), ROPE_HALF, axis=1)
    cos, sin = pl.pallas_call(
        _angles_kernel,
        out_shape=[jax.ShapeDtypeStruct((bsz * s, LANES), F32)] * 2,
        name="angles",
    )(pos_rep, invf)
    return cos.reshape(bsz, s, LANES), sin.reshape(bsz, s, LANES)


def _lane_mean(x):
    k = x.shape[1]
    ones = jnp.full((k, LANES), 1.0 / k, BF16)
    return jnp.dot(x.astype(BF16), ones, preferred_element_type=F32)


def _rope(x, cos, sin):
    lane = lax.broadcasted_iota(I32, x.shape, 1)
    partner = jnp.where(lane < ROPE_HALF,
                        pltpu.roll(x, HEAD_DIM - ROPE_HALF, 1),
                        pltpu.roll(x, ROPE_HALF, 1))
    return x * cos + partner * sin


def _mixer_kernel(sink_ref, q_ref, kvp_ref, kvc_ref, kvn_ref, u0_ref, u1_ref, v0_ref, v1_ref,
                  cosp_ref, cosc_ref, cosn_ref, sinp_ref, sinc_ref, sinn_ref,
                  gn_ref, ws_ref, bst_ref, gg_ref, o_ref, ya_scr, yg_scr):
    tq = q_ref.shape[1]
    nsub = tq // BLOCK
    aw = ATTN_HEADS * HEAD_DIM
    i = pl.program_id(1)
    sign = jnp.where(lax.broadcasted_iota(I32, (1, HEAD_DIM), 1) < ROPE_HALF, -1.0, 1.0)

    cos_c, sin_c = cosc_ref[0], sinc_ref[0] * sign
    tabs = ((cosp_ref[0], sinp_ref[0] * sign), (cos_c, sin_c), (cosn_ref[0], sinn_ref[0] * sign))
    scale = HEAD_DIM ** -0.5 * LOG2_E
    cos_q, sin_q = cos_c * scale, sin_c * scale

    def prepare_kv_head(h):
        kparts, vparts = [], []
        for ref, (cs, sn) in zip((kvp_ref, kvc_ref, kvn_ref), tabs):
            k = ref[0, :, h * HEAD_DIM:(h + 1) * HEAD_DIM].astype(F32)
            kparts.append(_rope(k, cs, sn).astype(BF16))
            vparts.append(ref[0, :, (KV_HEADS + h) * HEAD_DIM:(KV_HEADS + h + 1) * HEAD_DIM])
        kband = jnp.concatenate(kparts, axis=0)
        vband = jnp.concatenate(
            [jnp.concatenate(vparts, axis=0), jnp.ones((tq + 2 * BLOCK, HEAD_DIM), BF16)], axis=1)
        return kband, vband

    rows = Q_PER_KV * BLOCK
    band = 3 * BLOCK
    q_off = lax.broadcasted_iota(I32, (rows, BLOCK), 0) & (BLOCK - 1)
    k_off = lax.broadcasted_iota(I32, (rows, BLOCK), 1)
    bias_prev = jnp.where(k_off >= q_off, 0.0, NEG_INF)
    bias_next = jnp.where(k_off <= q_off, 0.0, NEG_INF)
    first_tile = i == 0
    last_tile = i == pl.num_programs(1) - 1
    def attention_block(s, h, kband, vband):
        bp = jnp.where(first_tile, NEG_INF, bias_prev) if s == 0 else bias_prev
        bn = jnp.where(last_tile, NEG_INF, bias_next) if s == nsub - 1 else bias_next
        blk = slice(s * BLOCK, (s + 1) * BLOCK)
        q4 = jnp.concatenate(
            [_rope(q_ref[0, blk, hq * HEAD_DIM:(hq + 1) * HEAD_DIM].astype(F32),
                   cos_q[blk], sin_q[blk]).astype(BF16)
             for hq in range(h * Q_PER_KV, (h + 1) * Q_PER_KV)], axis=0)
        kb = kband[s * BLOCK:s * BLOCK + band]
        vb = vband[s * BLOCK:s * BLOCK + band]
        sc = lax.dot_general(q4, kb, (((1,), (1,)), ((), ())), preferred_element_type=F32)
        sc = jnp.concatenate([sc[:, :BLOCK] + bp, sc[:, BLOCK:2 * BLOCK], sc[:, 2 * BLOCK:] + bn],
                             axis=1)
        sink = jnp.concatenate([jnp.full((BLOCK, 1), sink_ref[h * Q_PER_KV + g] * LOG2_E, F32)
                                for g in range(Q_PER_KV)], axis=0)
        m = jnp.maximum(jnp.max(sc, axis=-1, keepdims=True), sink)
        p = jnp.exp2(sc - m)
        pv = jnp.dot(p.astype(BF16), vb, preferred_element_type=F32)
        den = pv[:, HEAD_DIM:] + jnp.exp2(sink - m)
        o = pv[:, :HEAD_DIM] / den
        for g in range(Q_PER_KV):
            hq = h * Q_PER_KV + g
            ya_scr[s * BLOCK:(s + 1) * BLOCK, hq * HEAD_DIM:(hq + 1) * HEAD_DIM] = (
                o[g * BLOCK:(g + 1) * BLOCK])

    half_heads = GMLP_HEADS // 2

    def gmlp_head(h):
        u_ref = u0_ref if h < half_heads else u1_ref
        v_ref = v0_ref if h < half_heads else v1_ref
        hh = h % half_heads
        u = _gelu(u_ref[0, :, hh * HEAD_DIM:(hh + 1) * HEAD_DIM].astype(F32))
        v = _gelu(v_ref[0, :, hh * HEAD_DIM:(hh + 1) * HEAD_DIM].astype(F32))
        mu = _lane_mean(v)
        dv = v - mu
        var = _lane_mean(dv * dv)
        vn = (dv * lax.rsqrt(var + EPS) * gn_ref[h:h + 1, :]).astype(BF16)
        w = ws_ref[h].astype(BF16)
        bias = bst_ref[:, h:h + 1]
        for cidx in range(nsub):
            sl = slice(cidx * BLOCK, (cidx + 1) * BLOCK)
            mixed = jnp.dot(w, vn[sl], preferred_element_type=F32) + bias
            yg_scr[sl, h * HEAD_DIM:(h + 1) * HEAD_DIM] = u[sl] * mixed

    heads_per_block = -(-GMLP_HEADS // (KV_HEADS * nsub))
    next_head = 0
    for h in range(KV_HEADS):
        prepared = prepare_kv_head(h)
        for s in range(nsub):
            attention_block(s, h, *prepared)
            for _ in range(heads_per_block):
                if next_head < GMLP_HEADS:
                    gmlp_head(next_head)
                    next_head += 1
    for h in range(next_head, GMLP_HEADS):
        gmlp_head(h)

    for scr, lo in ((ya_scr, 0), (yg_scr, aw)):
        y = scr[...]
        width = y.shape[1]
        inv_rms = lax.rsqrt(_lane_mean(y * y) + EPS)
        inv_rms = jnp.concatenate([inv_rms] * (width // LANES), axis=1)
        o_ref[0, :, lo:lo + width] = (y * inv_rms * gg_ref[:, lo:lo + width]).astype(BF16)


def _mixer(z, cos, sin, sink, gn, ws, bs, gg):
    assert WINDOW == BLOCK
    bsz, s, _ = z.shape
    tq = min(MIXER_TILE, s)
    nsub = tq // BLOCK
    nblk = s // BLOCK
    aw = ATTN_HEADS * HEAD_DIM
    gw = GMLP_HEADS * HEAD_DIM
    cw = 2 * KV_HEADS * HEAD_DIM
    assert aw % cw == 0 and gw == 2 * cw
    kv = aw // cw
    u0, u1, v0, v1 = kv + 1, kv + 2, kv + 3, kv + 4

    def prev_blk(b, i):
        return (b, jnp.maximum(i * nsub - 1, 0), kv)

    def next_blk(b, i):
        return (b, jnp.minimum((i + 1) * nsub, nblk - 1), kv)

    tab_specs = [pl.BlockSpec((1, BLOCK, HEAD_DIM), lambda b, i: prev_blk(b, i)[:2] + (0,)),
                 pl.BlockSpec((1, tq, HEAD_DIM), lambda b, i: (b, i, 0)),
                 pl.BlockSpec((1, BLOCK, HEAD_DIM), lambda b, i: next_blk(b, i)[:2] + (0,))]
    return pl.pallas_call(
        _mixer_kernel,
        grid=(bsz, s // tq),
        in_specs=[pl.BlockSpec(memory_space=pltpu.SMEM),
                  pl.BlockSpec((1, tq, aw), lambda b, i: (b, i, 0)),
                  pl.BlockSpec((1, BLOCK, cw), prev_blk),
                  pl.BlockSpec((1, tq, cw), lambda b, i: (b, i, kv)),
                  pl.BlockSpec((1, BLOCK, cw), next_blk),
                  pl.BlockSpec((1, tq, cw), lambda b, i: (b, i, u0)),
                  pl.BlockSpec((1, tq, cw), lambda b, i: (b, i, u1)),
                  pl.BlockSpec((1, tq, cw), lambda b, i: (b, i, v0)),
                  pl.BlockSpec((1, tq, cw), lambda b, i: (b, i, v1)),
                  *tab_specs, *tab_specs,
                  pl.BlockSpec((GMLP_HEADS, HEAD_DIM), lambda b, i: (0, 0)),
                  pl.BlockSpec((GMLP_HEADS, BLOCK, BLOCK), lambda b, i: (0, 0, 0)),
                  pl.BlockSpec((BLOCK, GMLP_HEADS), lambda b, i: (0, 0)),
                  pl.BlockSpec((1, aw + gw), lambda b, i: (0, 0))],
        out_specs=pl.BlockSpec((1, tq, aw + gw), lambda b, i: (b, i, 0)),
        out_shape=jax.ShapeDtypeStruct((bsz, s, aw + gw), BF16),
        scratch_shapes=[pltpu.VMEM((tq, aw), F32), pltpu.VMEM((tq, gw), F32)],
        compiler_params=pltpu.CompilerParams(dimension_semantics=("arbitrary", "arbitrary"),
                                             vmem_limit_bytes=VMEM_LIMIT),
        name="mixer",
    )(sink, z, z, z, z, z, z, z, z, cos, cos, cos, sin, sin, sin, gn, ws, bs.T,
      gg.reshape(1, aw + gw))


def _split_bf16(x):
    hi = x.astype(BF16)
    lo = (x - hi.astype(F32)).astype(BF16)
    return hi, lo


def _outproj_kernel(y_ref, w_ref, x_ref, mod_ref, gpost_ref, gpre_ref, wr_ref, br_ref,
                    h_ref, f_ref, eid_ref, ew_ref, mix, wr_split):
    @pl.when((pl.program_id(0) == 0) & (pl.program_id(1) == 0))
    def _():
        w_hi, w_lo = _split_bf16(wr_ref[...])
        wr_split[:, :LANES] = w_hi
        wr_split[:, LANES:] = w_lo

    mix[...] = jnp.dot(y_ref[0], w_ref[...], preferred_element_type=F32)
    for stage in _outproj_finish(mix, x_ref, mod_ref, gpost_ref, gpre_ref, wr_split, br_ref,
                                 h_ref, f_ref, eid_ref, ew_ref):
        stage()


def _outproj_finish(mix_ref, x_ref, mod_ref, gpost_ref, gpre_ref, wr_ref, br_ref,
                    h_ref, f_ref, eid_ref, ew_ref):
    state = {}

    def residual():
        h1 = x_ref[0] + _rms(mix_ref[...], mod_ref[0][2:3] * gpost_ref[...])
        h_ref[0] = h1
        state["h1"] = h1

    def prenorm():
        mod = mod_ref[0]
        f = _rms(state["h1"], gpre_ref[...] * (1.0 + mod[4:5])) + mod[3:4]
        f_ref[...] = f
        state["f"] = f

    def logits():
        f_hi, f_lo = _split_bf16(state["f"])
        tm = f_hi.shape[0]
        r = jnp.dot(jnp.concatenate([f_hi, f_lo], axis=0), wr_ref[...],
                    preferred_element_type=F32)
        lg = (r[:tm, :LANES] + r[:tm, LANES:]) + (r[tm:, :LANES] + r[tm:, LANES:])
        state["logits"] = lg.T[:ROUTER_ROWS] + br_ref[...]

    def route():
        _route(state["logits"], eid_ref, ew_ref)

    return [residual, prenorm, logits, route]


def _route(logits, eid_ref, ew_ref):
    gl = [logits[g:g + 1] for g in range(N_GROUPS)]
    gmax = functools.reduce(jnp.maximum, gl)
    gidx = jnp.full(gmax.shape, N_GROUPS - 1, I32)
    for g in range(N_GROUPS - 2, -1, -1):
        gidx = jnp.where(gl[g] == gmax, g, gidx)
    gval = 1.0 / functools.reduce(lambda a, b: a + b, [jnp.exp(v - gmax) for v in gl])

    es = []
    for e in range(EXPERTS_PER_GROUP):
        r = N_GROUPS + (N_GROUPS - 1) * EXPERTS_PER_GROUP + e
        v = logits[r:r + 1]
        for g in range(N_GROUPS - 2, -1, -1):
            r = N_GROUPS + g * EXPERTS_PER_GROUP + e
            v = jnp.where(gidx == g, logits[r:r + 1], v)
        es.append(v)
    m1 = functools.reduce(jnp.maximum, es)
    i1 = jnp.full(m1.shape, EXPERTS_PER_GROUP - 1, I32)
    for e in range(EXPERTS_PER_GROUP - 2, -1, -1):
        i1 = jnp.where(es[e] == m1, e, i1)
    rest = [jnp.where(i1 == e, -jnp.inf, es[e]) for e in range(EXPERTS_PER_GROUP)]
    m2 = functools.reduce(jnp.maximum, rest)
    i2 = jnp.full(m2.shape, EXPERTS_PER_GROUP - 1, I32)
    for e in range(EXPERTS_PER_GROUP - 2, -1, -1):
        i2 = jnp.where(rest[e] == m2, e, i2)
    p2 = jnp.exp(m2 - m1)
    w1 = gval / (1.0 + p2)
    w2 = gval * p2 / (1.0 + p2)
    eid_ref[0:1, :] = gidx * EXPERTS_PER_GROUP + i1
    eid_ref[1:2, :] = gidx * EXPERTS_PER_GROUP + i2
    ew_ref[0:1, :] = w1
    ew_ref[1:2, :] = w2


def _outproj(ycat, w_out_bf16, x, mod, gpost, gpre, wr, br):
    bsz, s, d = x.shape
    t = bsz * s
    tm = min(OUTPROJ_TILE, s)
    nti = s // tm
    return pl.pallas_call(
        _outproj_kernel,
        grid=(bsz, nti),
        in_specs=[pl.BlockSpec((1, tm, d), lambda b, i: (b, i, 0)),
                  pl.BlockSpec((d, d), lambda b, i: (0, 0)),
                  pl.BlockSpec((1, tm, d), lambda b, i: (b, i, 0)),
                  pl.BlockSpec((1, N_MOD, d), lambda b, i: (b, 0, 0)),
                  pl.BlockSpec((1, d), lambda b, i: (0, 0)),
                  pl.BlockSpec((1, d), lambda b, i: (0, 0)),
                  pl.BlockSpec((d, LANES), lambda b, i: (0, 0)),
                  pl.BlockSpec((ROUTER_ROWS, 1), lambda b, i: (0, 0))],
        out_specs=[pl.BlockSpec((1, tm, d), lambda b, i: (b, i, 0)),
                   pl.BlockSpec((tm, d), lambda b, i: (b * nti + i, 0)),
                   pl.BlockSpec((TOP_K, tm), lambda b, i: (0, b * nti + i)),
                   pl.BlockSpec((TOP_K, tm), lambda b, i: (0, b * nti + i))],
        out_shape=[jax.ShapeDtypeStruct((bsz, s, d), F32),
                   jax.ShapeDtypeStruct((t, d), F32),
                   jax.ShapeDtypeStruct((TOP_K, t), I32),
                   jax.ShapeDtypeStruct((TOP_K, t), F32)],
        scratch_shapes=[pltpu.VMEM((tm, d), F32), pltpu.VMEM((d, 2 * LANES), BF16)],
        compiler_params=pltpu.CompilerParams(dimension_semantics=("arbitrary", "arbitrary"),
                                             vmem_limit_bytes=LARGE_VMEM_LIMIT),
        name="outproj",
    )(ycat, w_out_bf16, x, mod, gpost.reshape(1, d), gpre.reshape(1, d), wr, br)


def _slots_kernel(eid_ref, pos_ref, tab_ref, rank_scr, *, n_slots):
    t = eid_ref.shape[1]
    chunk = min(512, t)
    nchunk = t // chunk
    tri = jnp.where(lax.broadcasted_iota(I32, (chunk, chunk), 0)
                    <= lax.broadcasted_iota(I32, (chunk, chunk), 1), 1.0, 0.0).astype(BF16)
    e_io = lax.broadcasted_iota(I32, (N_EXPERTS, chunk), 0)

    cnt = jnp.zeros((N_EXPERTS, 1), F32)
    for k in range(TOP_K):
        def rank_body(c, carry, k=k):
            off = pl.multiple_of(c * chunk, chunk)
            onehot = e_io == eid_ref[pl.ds(k, 1), pl.ds(off, chunk)]
            ones = jnp.where(onehot, 1.0, 0.0)
            prefix = jnp.dot(ones.astype(BF16), tri, preferred_element_type=F32) + carry
            rank = jnp.sum(jnp.where(onehot, prefix, 0.0), axis=0, keepdims=True) - 1.0
            rank_scr[pl.ds(k, 1), pl.ds(off, chunk)] = rank
            return carry + jnp.sum(ones, axis=1, keepdims=True)
        cnt = lax.fori_loop(0, nchunk, rank_body, cnt)

    padded = jnp.floor((cnt + (SLOT_TILE - 1)) * (1.0 / SLOT_TILE)) * SLOT_TILE
    sub = lax.broadcasted_iota(I32, (N_EXPERTS, LANES), 0)
    lan = lax.broadcasted_iota(I32, (N_EXPERTS, LANES), 1)
    padded_row = jnp.sum(jnp.where(sub == lan, padded, 0.0), axis=0, keepdims=True)
    start = jnp.sum(jnp.where(lan < sub, padded_row, 0.0), axis=1, keepdims=True)
    end = start + padded
    pad_lo = start + cnt

    for k in range(TOP_K):
        def pos_body(c, carry, k=k):
            off = pl.multiple_of(c * chunk, chunk)
            onehot = e_io == eid_ref[pl.ds(k, 1), pl.ds(off, chunk)]
            base = jnp.sum(jnp.where(onehot, start + SLOT_TILE, 0.0), axis=0, keepdims=True)
            pos_ref[pl.ds(k, 1), pl.ds(off, chunk)] = (
                base + rank_scr[pl.ds(k, 1), pl.ds(off, chunk)]).astype(I32)
            return carry
        lax.fori_loop(0, nchunk, pos_body, 0)

    lo_row = jnp.sum(jnp.where(sub == lan, pad_lo, 0.0), axis=0, keepdims=True)
    hi_row = jnp.sum(jnp.where(sub + PAD_HI_LANE == lan, end, 0.0), axis=0, keepdims=True)
    first_row = jnp.sum(jnp.where(sub + TILE_START_LANE == lan, start * (1.0 / SLOT_TILE), 0.0),
                        axis=0, keepdims=True)
    lane_row = lax.broadcasted_iota(I32, (1, LANES), 1)
    total = jnp.sum(padded_row, axis=1, keepdims=True)
    tail = (jnp.where(lane_row == N_EXPERTS, total, 0.0)
            + jnp.where(lane_row == PAD_HI_LANE + N_EXPERTS, float(n_slots), 0.0)
            + jnp.where(lane_row == TILE_START_LANE + N_EXPERTS, total * (1.0 / SLOT_TILE), 0.0))
    tab_ref[...] = (lo_row + hi_row + first_row + tail).astype(I32)


def _slots(eid, n_tiles):
    t = eid.shape[1]
    return pl.pallas_call(
        functools.partial(_slots_kernel, n_slots=n_tiles * SLOT_TILE),
        out_shape=[jax.ShapeDtypeStruct((TOP_K, t), I32),
                   jax.ShapeDtypeStruct((1, LANES), I32)],
        scratch_shapes=[pltpu.VMEM((TOP_K, t), F32)],
        compiler_params=pltpu.CompilerParams(vmem_limit_bytes=VMEM_LIMIT),
        name="slots",
    )(eid)


def _invert_kernel(pos_ref, tab_ref, inv_ref, *, n_pairs, n_slots):
    group = 8
    spare = n_pairs
    for e in range(N_EXPERTS + 1):
        lo, hi = tab_ref[e], tab_ref[PAD_HI_LANE + e]

        def fill(i, c, lo=lo, spare=spare):
            for k in range(group):
                inv_ref[SLOT_TILE + lo + i * group + k] = spare + i * group + k
            return c
        lax.fori_loop(0, (hi - lo + group - 1) // group, fill, 0)
        spare = spare + hi - lo

    def guard(r, c):
        inv_ref[r] = n_slots + r
        inv_ref[SLOT_TILE + n_slots + r] = n_slots + r
        return c
    lax.fori_loop(0, SLOT_TILE, guard, 0, unroll=8)

    def place(n, c):
        inv_ref[pos_ref[n]] = n
        return c
    lax.fori_loop(0, n_pairs, place, 0, unroll=16)


def _invert(pos_flat, tab, n_tiles):
    n_pairs = pos_flat.shape[0]
    n_slots = n_tiles * SLOT_TILE
    return pl.pallas_call(
        functools.partial(_invert_kernel, n_pairs=n_pairs, n_slots=n_slots),
        in_specs=[pl.BlockSpec(memory_space=pltpu.SMEM), pl.BlockSpec(memory_space=pltpu.SMEM)],
        out_specs=pl.BlockSpec(memory_space=pltpu.SMEM),
        out_shape=jax.ShapeDtypeStruct((n_slots + 2 * SLOT_TILE,), I32),
        name="invert",
    )(pos_flat, tab)


def _experts_kernel(inv_ref, tab_ref, f_ref, wg_ref, wu_ref, wd_ref, y_ref,
                    wbg, wbu, wbd, xa, xb, ya, yb, gsem, ssem, zsem, *, n_tokens, n_slots):
    e = pl.program_id(0)
    j = pl.program_id(1)
    fc = wg_ref.shape[2]
    total = tab_ref[TILE_START_LANE + N_EXPERTS]

    def gather_copy(tile, r, xbuf):
        v = inv_ref[(tile + 1) * SLOT_TILE + r]
        return pltpu.make_async_copy(f_ref.at[pl.ds(v & (n_tokens - 1), 1)], xbuf.at[pl.ds(r, 1)], gsem)

    def scatter_copy(tile, r, ybuf):
        v = inv_ref[(tile + 1) * SLOT_TILE + r]
        return pltpu.make_async_copy(ybuf.at[pl.ds(r, 1)], y_ref.at[pl.ds(v, 1)], ssem)

    def gather_wait(xbuf):
        pltpu.make_async_copy(f_ref.at[pl.ds(0, SLOT_TILE)], xbuf, gsem).wait()

    def scatter_wait(ybuf):
        pltpu.make_async_copy(ybuf, y_ref.at[pl.ds(0, SLOT_TILE)], ssem).wait()

    def rows(fn):
        def body(r, c):
            fn(r)
            return c
        lax.fori_loop(0, SLOT_TILE, body, 0, unroll=8)

    @pl.when((e == 0) & (j == 0))
    def _():
        ya[...] = jnp.zeros(ya.shape, ya.dtype)
        yb[...] = jnp.zeros(yb.shape, yb.dtype)
        rows(lambda r: gather_copy(0, r, xa).start())

        def zero_copy(tile):
            dst = pl.multiple_of(tile * SLOT_TILE, SLOT_TILE)
            return pltpu.make_async_copy(ya, y_ref.at[pl.ds(dst, SLOT_TILE)], zsem)

        def zstart(tile, c):
            zero_copy(tile).start()
            return c

        def zwait(tile, c):
            zero_copy(tile).wait()
            return c
        lax.fori_loop(total, n_slots // SLOT_TILE, zstart, 0)
        lax.fori_loop(total, n_slots // SLOT_TILE, zwait, 0)

    @pl.when(e < N_EXPERTS)
    def _():
        slot = e % 2
        col = pl.multiple_of(j * fc, fc)
        wbg[slot, :, pl.ds(col, fc)] = wg_ref[0].astype(BF16)
        wbu[slot, :, pl.ds(col, fc)] = wu_ref[0].astype(BF16)
        wbd[slot, pl.ds(col, fc), :] = wd_ref[0].astype(BF16)

    prev = jnp.maximum(e - 1, 0)
    first = tab_ref[TILE_START_LANE + prev]
    count = jnp.where(e == 0, 0, tab_ref[TILE_START_LANE + prev + 1] - first)
    lo = first + (count * j) // N_WCHUNK
    hi = first + (count * (j + 1)) // N_WCHUNK
    wslot = prev % 2

    def tile(t, c):
        def run(xcur, xnext, ycur, yprev, ws):
            gather_wait(xcur)
            for r in range(SLOT_TILE):
                gather_copy(t + 1, r, xnext).start()
            for r in range(SLOT_TILE):
                scatter_copy(t - 1, r, yprev).start()
            x = xcur[...].astype(BF16)
            hg = jnp.dot(x, wbg[ws], preferred_element_type=F32)
            hu = jnp.dot(x, wbu[ws], preferred_element_type=F32)
            hid = (_silu(hg) * hu).astype(BF16)
            ycur[...] = jnp.dot(hid, wbd[ws], preferred_element_type=F32)
            scatter_wait(yprev)

        for ws in range(2):
            for parity, bufs in enumerate(((xa, xb, ya, yb), (xb, xa, yb, ya))):
                @pl.when((wslot == ws) & (t % 2 == parity))
                def _(bufs=bufs, ws=ws):
                    run(*bufs, ws)
        return c
    lax.fori_loop(lo, hi, tile, 0)

    @pl.when((e == N_EXPERTS) & (j == N_WCHUNK - 1))
    def _():
        last = total - 1

        def finish(xnext, ylast):
            gather_wait(xnext)
            rows(lambda r: scatter_copy(last, r, ylast).start())
            scatter_wait(ylast)

        @pl.when(last % 2 == 0)
        def _():
            finish(xb, ya)

        @pl.when(last % 2 == 1)
        def _():
            finish(xa, yb)


def _experts(inv, tab, f, wg, wu, wd, n_tiles):
    t, d = f.shape
    assert t & (t - 1) == 0
    fdim = wg.shape[2]
    fc = fdim // N_WCHUNK
    n_slots = n_tiles * SLOT_TILE

    def expert(e):
        return jnp.minimum(e, N_EXPERTS - 1)

    def chunk(e, j):
        return jnp.where(e == N_EXPERTS, N_WCHUNK - 1, j)

    return pl.pallas_call(
        functools.partial(_experts_kernel, n_tokens=t, n_slots=n_slots),
        grid_spec=pltpu.PrefetchScalarGridSpec(
            num_scalar_prefetch=2,
            grid=(N_EXPERTS + 1, N_WCHUNK),
            in_specs=[pl.BlockSpec(memory_space=pl.ANY),
                      pl.BlockSpec((1, d, fc), lambda e, j, inv, tab: (expert(e), 0, chunk(e, j))),
                      pl.BlockSpec((1, d, fc), lambda e, j, inv, tab: (expert(e), 0, chunk(e, j))),
                      pl.BlockSpec((1, fc, d), lambda e, j, inv, tab: (expert(e), chunk(e, j), 0))],
            out_specs=pl.BlockSpec(memory_space=pl.ANY),
            scratch_shapes=[pltpu.VMEM((2, d, fdim), BF16), pltpu.VMEM((2, d, fdim), BF16),
                            pltpu.VMEM((2, fdim, d), BF16),
                            pltpu.VMEM((SLOT_TILE, d), F32), pltpu.VMEM((SLOT_TILE, d), F32),
                            pltpu.VMEM((SLOT_TILE, d), F32), pltpu.VMEM((SLOT_TILE, d), F32),
                            pltpu.SemaphoreType.DMA, pltpu.SemaphoreType.DMA, pltpu.SemaphoreType.DMA]),
        out_shape=jax.ShapeDtypeStruct((n_slots + SLOT_TILE, d), F32),
        compiler_params=pltpu.CompilerParams(dimension_semantics=("arbitrary", "arbitrary"),
                                             vmem_limit_bytes=LARGE_VMEM_LIMIT),
        name="experts",
    )(inv, tab, f, wg, wu, wd)


def _final_kernel(y0_ref, y1_ref, h_ref, ew_ref, mod_ref, g_ref, o_ref):
    ew = ew_ref[...]
    y = ew[:, 0:1] * y0_ref[...] + ew[:, 1:2] * y1_ref[...]
    o_ref[0] = h_ref[0] + _rms(y, mod_ref[0][5:6] * g_ref[...])


def _final(y_pairs, h1, ew_t, mod, g):
    bsz, s, d = h1.shape
    t = bsz * s
    tg = min(FINAL_TILE, s)
    nti = s // tg
    return pl.pallas_call(
        _final_kernel,
        grid=(bsz, nti),
        in_specs=[pl.BlockSpec((tg, d), lambda b, i: (b * nti + i, 0)),
                  pl.BlockSpec((tg, d), lambda b, i: (t // tg + b * nti + i, 0)),
                  pl.BlockSpec((1, tg, d), lambda b, i: (b, i, 0)),
                  pl.BlockSpec((tg, TOP_K), lambda b, i: (b * nti + i, 0)),
                  pl.BlockSpec((1, N_MOD, d), lambda b, i: (b, 0, 0)),
                  pl.BlockSpec((1, d), lambda b, i: (0, 0))],
        out_specs=pl.BlockSpec((1, tg, d), lambda b, i: (b, i, 0)),
        out_shape=jax.ShapeDtypeStruct((bsz, s, d), F32),
        compiler_params=pltpu.CompilerParams(dimension_semantics=("arbitrary", "arbitrary"),
                                             vmem_limit_bytes=VMEM_LIMIT),
        name="final",
    )(y_pairs, y_pairs, h1, ew_t, mod, g.reshape(1, d))


def kernel(x, c, positions, w_ada, b_ada, g_mix_pre, g_mix_post, w_in, sink_logits, gmlp_norm_g,
           w_spatial, b_spatial, g_group_out, w_out, g_ffn_pre, g_ffn_post, w_router_group,
           b_router_group, w_router_expert, b_router_expert, w_gate, w_up, w_down):
    bsz, s, d = x.shape
    t = bsz * s
    depth = w_ada.shape[0]
    fdim = w_gate.shape[-1]
    n_tiles = (TOP_K * t) // SLOT_TILE + N_EXPERTS
    n_slots = n_tiles * SLOT_TILE
    cos, sin = _rope_tables(positions)
    h = x
    for l in range(depth):
        mod = _ada(c, w_ada[l], b_ada[l]).reshape(bsz, N_MOD, d)
        z = _inproj(h, mod, g_mix_pre[l], w_in[l].astype(BF16))
        ycat = _mixer(z, cos, sin, sink_logits[l], gmlp_norm_g[l], w_spatial[l],
                      b_spatial[l], g_group_out[l])
        wr = jnp.concatenate([w_router_group[l], w_router_expert[l].reshape(d, N_EXPERTS)], axis=1)
        wr = jnp.pad(wr, ((0, 0), (0, LANES - N_GROUPS - N_EXPERTS)))
        br = jnp.concatenate([b_router_group[l], b_router_expert[l].reshape(N_EXPERTS)])
        br = jnp.pad(br, (0, ROUTER_ROWS - N_GROUPS - N_EXPERTS)).reshape(ROUTER_ROWS, 1)
        h1, f, eid, ew = _outproj(ycat, w_out[l].astype(BF16), h, mod, g_mix_post[l], g_ffn_pre[l],
                                  wr, br)
        pos, tab = _slots(eid, n_tiles)
        tab = tab.reshape(LANES)
        inv = _invert(pos.reshape(TOP_K * t), tab, n_tiles)
        y_pairs = _experts(inv, tab, f,
                           w_gate[l].reshape(N_EXPERTS, d, fdim),
                           w_up[l].reshape(N_EXPERTS, d, fdim),
                           w_down[l].reshape(N_EXPERTS, fdim, d), n_tiles)
        h = _final(y_pairs, h1, ew.T, mod, g_ffn_post[l])
    return h
```

```python
import functools

import jax
import jax.numpy as jnp
from jax import lax
from jax.experimental import pallas as pl
from jax.experimental.pallas import tpu as pltpu

F32 = jnp.float32
BF16 = jnp.bfloat16
I32 = jnp.int32

HEAD_DIM = 128
ATTN_HEADS = 8
KV_HEADS = 2
Q_PER_KV = ATTN_HEADS // KV_HEADS
GMLP_HEADS = 8
WINDOW = 128
BLOCK = 128
ROPE_THETA = 500000.0
ROPE_DIM = HEAD_DIM // 4
ROPE_HALF = ROPE_DIM // 2
N_GROUPS = 4
EXPERTS_PER_GROUP = 4
N_EXPERTS = N_GROUPS * EXPERTS_PER_GROUP
TOP_K = 2
N_MOD = 6
EPS = 1e-6
NEG_INF = -1e30
LOG2_E = 1.4426950408889634
LANES = 128
BF16_SUBLANES = 16
PAD_HI_LANE = 32
TILE_START_LANE = 64
N_WCHUNK = 4
ROUTER_ROWS = 32
SLOT_TILE = 256

ADA_COL_TILE = 1024
INPROJ_ROW_TILE = 2048
INPROJ_COL_TILE = 512
INPROJ_SLICES = 4
MIXER_TILE = 1024
OUTPROJ_TILE = 512
FINAL_TILE = 512
STREAM_BUFFERS = 3
VMEM_LIMIT = 52 * 1024 * 1024
LARGE_VMEM_LIMIT = 58 * 1024 * 1024


def _rms(x, g):
    ms = jnp.mean(x * x, axis=-1, keepdims=True)
    return x * lax.rsqrt(ms + EPS) * g


def _silu(x):
    return x / (1.0 + jnp.exp(-x))


def _gelu(x):
    return 0.5 * x * (1.0 + lax.erf(x * 0.7071067811865476))


def _ada_kernel(c_ref, w_ref, b_ref, o_ref):
    ca = _silu(c_ref[...]).astype(BF16)
    o_ref[...] = jnp.dot(ca, w_ref[...].astype(BF16), preferred_element_type=F32) + b_ref[...]


def _ada(c, w, b):
    bsz, d = c.shape
    n = w.shape[1]
    tn = ADA_COL_TILE
    return pl.pallas_call(
        _ada_kernel,
        grid=(n // tn,),
        in_specs=[pl.BlockSpec((bsz, d), lambda j: (0, 0)),
                  pl.BlockSpec((d, tn), lambda j: (0, j)),
                  pl.BlockSpec((1, tn), lambda j: (0, j))],
        out_specs=pl.BlockSpec((bsz, tn), lambda j: (0, j)),
        out_shape=jax.ShapeDtypeStruct((bsz, n), F32),
        compiler_params=pltpu.CompilerParams(dimension_semantics=("arbitrary",),
                                             vmem_limit_bytes=VMEM_LIMIT),
        name="ada",
    )(c, w, b.reshape(1, n))


def _inproj_kernel(x_ref, mod_ref, g_ref, w_ref, o_ref, a_scr, *, n_tiles):
    g = pl.program_id(0)
    j = pl.program_id(1)
    rows = x_ref.shape[1]

    @pl.when((g < n_tiles) & (j % 2 == 0) & (j < 2 * INPROJ_SLICES))
    def _():
        mod = mod_ref[0]
        a = _rms(x_ref[0], g_ref[...] * (1.0 + mod[1:2])) + mod[0:1]
        start = pl.multiple_of((j // 2) * rows, rows)
        a_scr[g % 2, pl.ds(start, rows), :] = a.astype(BF16)

    @pl.when(g > 0)
    def _():
        o_ref[0] = jnp.dot(a_scr[(g - 1) % 2], w_ref[...], preferred_element_type=F32).astype(BF16)


def _inproj(x, mod, g, w_bf16):
    bsz, s, d = x.shape
    n = w_bf16.shape[1]
    tm = min(INPROJ_ROW_TILE, s)
    tn = INPROJ_COL_TILE
    nj = n // tn
    nti = s // tm
    ntile = bsz * nti
    assert 2 * (INPROJ_SLICES - 1) < nj
    rows = tm // INPROJ_SLICES

    def norm_tile(g):
        gg = jnp.minimum(g, ntile - 1)
        return gg // nti, gg % nti

    def mm_tile(g):
        gg = jnp.maximum(g - 1, 0)
        return gg // nti, gg % nti

    def x_block(g, j):
        b, i = norm_tile(g)
        return b, i * INPROJ_SLICES + jnp.minimum(j // 2, INPROJ_SLICES - 1), 0

    return pl.pallas_call(
        functools.partial(_inproj_kernel, n_tiles=ntile),
        grid=(ntile + 1, nj),
        in_specs=[pl.BlockSpec((1, rows, d), x_block),
                  pl.BlockSpec((1, N_MOD, d), lambda g, j: (norm_tile(g)[0], 0, 0)),
                  pl.BlockSpec((1, d), lambda g, j: (0, 0)),
                  pl.BlockSpec((d, tn), lambda g, j: (0, j))],
        out_specs=pl.BlockSpec((1, tm, tn), lambda g, j: mm_tile(g) + (jnp.where(g == 0, 0, j),)),
        out_shape=jax.ShapeDtypeStruct((bsz, s, n), BF16),
        scratch_shapes=[pltpu.VMEM((2, tm, d), BF16)],
        compiler_params=pltpu.CompilerParams(dimension_semantics=("arbitrary", "arbitrary"),
                                             vmem_limit_bytes=VMEM_LIMIT),
        name="inproj",
    )(x, mod, g.reshape(1, d), w_bf16)


def _angles_kernel(pos_ref, invf_ref, cos_ref, sin_ref):
    rows = pos_ref.shape[0]
    per_row = LANES // ROPE_HALF
    ang = pos_ref[...].astype(F32) * invf_ref[...]
    src = lax.broadcasted_iota(I32, (LANES, LANES), 0)
    dst = lax.broadcasted_iota(I32, (LANES, LANES), 1)
    rotary_lane = lax.broadcasted_iota(I32, (1, LANES), 1) < ROPE_DIM
    for out_ref, val, fill in ((cos_ref, jnp.cos(ang), 1.0), (sin_ref, jnp.sin(ang), 0.0)):
        p0 = val.astype(BF16)
        r0 = val - p0.astype(F32)
        p1 = r0.astype(BF16)
        p2 = (r0 - p1.astype(F32)).astype(BF16)
        rest = jnp.where(rotary_lane, 0.0, fill)
        for a in range(per_row):
            sel = jnp.where((dst < ROPE_DIM) & ((dst & (ROPE_HALF - 1)) + a * ROPE_HALF == src),
                            1.0, 0.0).astype(BF16)
            spread = (jnp.dot(p0, sel, preferred_element_type=F32)
                      + jnp.dot(p1, sel, preferred_element_type=F32)
                      + jnp.dot(p2, sel, preferred_element_type=F32))
            out_ref[pl.ds(a, rows, stride=per_row), :] = spread + rest


def _rope_tables(positions):
    bsz, s = positions.shape
    per_row = LANES // ROPE_HALF
    rows = bsz * s // per_row
    inv = ROPE_THETA ** (-jnp.arange(ROPE_HALF, dtype=F32) * 2.0 / ROPE_DIM)
    invf = jnp.tile(inv, per_row).reshape(1, LANES)
    pos_rep = jnp.repeat(positions.reshape(rows, per_row), ROPE_HALF, axis=1)
    cos, sin = pl.pallas_call(
        _angles_kernel,
        out_shape=[jax.ShapeDtypeStruct((bsz * s, LANES), F32)] * 2,
        name="angles",
    )(pos_rep, invf)
    return cos.reshape(bsz, s, LANES), sin.reshape(bsz, s, LANES)


def _lane_mean(x):
    k = x.shape[1]
    ones = jnp.full((k, LANES), 1.0 / k, BF16)
    return jnp.dot(x.astype(BF16), ones, preferred_element_type=F32)


def _rope(x, cos, sin):
    lane = lax.broadcasted_iota(I32, x.shape, 1)
    partner = jnp.where(lane < ROPE_HALF,
                        pltpu.roll(x, HEAD_DIM - ROPE_HALF, 1),
                        pltpu.roll(x, ROPE_HALF, 1))
    return x * cos + partner * sin


def _mixer_kernel(sink_ref, q_ref, kvp_ref, kvc_ref, kvn_ref, u0_ref, u1_ref, v0_ref, v1_ref,
                  cosp_ref, cosc_ref, cosn_ref, sinp_ref, sinc_ref, sinn_ref,
                  gn_ref, ws_ref, bst_ref, gg_ref, o_ref, ya_scr, yg_scr):
    tq = q_ref.shape[1]
    nsub = tq // BLOCK
    aw = ATTN_HEADS * HEAD_DIM
    i = pl.program_id(1)
    sign = jnp.where(lax.broadcasted_iota(I32, (1, HEAD_DIM), 1) < ROPE_HALF, -1.0, 1.0)

    cos_c, sin_c = cosc_ref[0], sinc_ref[0] * sign
    tabs = ((cosp_ref[0], sinp_ref[0] * sign), (cos_c, sin_c), (cosn_ref[0], sinn_ref[0] * sign))
    scale = HEAD_DIM ** -0.5 * LOG2_E
    cos_q, sin_q = cos_c * scale, sin_c * scale

    def prepare_kv_head(h):
        kparts, vparts = [], []
        for ref, (cs, sn) in zip((kvp_ref, kvc_ref, kvn_ref), tabs):
            k = ref[0, :, h * HEAD_DIM:(h + 1) * HEAD_DIM].astype(F32)
            kparts.append(_rope(k, cs, sn).astype(BF16))
            vparts.append(ref[0, :, (KV_HEADS + h) * HEAD_DIM:(KV_HEADS + h + 1) * HEAD_DIM])
        kband = jnp.concatenate(kparts, axis=0)
        vband = jnp.concatenate(
            [jnp.concatenate(vparts, axis=0), jnp.ones((tq + 2 * BLOCK, HEAD_DIM), BF16)], axis=1)
        return kband, vband

    rows = Q_PER_KV * BLOCK
    band = 3 * BLOCK
    q_off = lax.broadcasted_iota(I32, (rows, BLOCK), 0) & (BLOCK - 1)
    k_off = lax.broadcasted_iota(I32, (rows, BLOCK), 1)
    bias_prev = jnp.where(k_off >= q_off, 0.0, NEG_INF)
    bias_next = jnp.where(k_off <= q_off, 0.0, NEG_INF)
    first_tile = i == 0
    last_tile = i == pl.num_programs(1) - 1
    def attention_block(s, h, kband, vband):
        bp = jnp.where(first_tile, NEG_INF, bias_prev) if s == 0 else bias_prev
        bn = jnp.where(last_tile, NEG_INF, bias_next) if s == nsub - 1 else bias_next
        blk = slice(s * BLOCK, (s + 1) * BLOCK)
        q4 = jnp.concatenate(
            [_rope(q_ref[0, blk, hq * HEAD_DIM:(hq + 1) * HEAD_DIM].astype(F32),
                   cos_q[blk], sin_q[blk]).astype(BF16)
             for hq in range(h * Q_PER_KV, (h + 1) * Q_PER_KV)], axis=0)
        kb = kband[s * BLOCK:s * BLOCK + band]
        vb = vband[s * BLOCK:s * BLOCK + band]
        sc = lax.dot_general(q4, kb, (((1,), (1,)), ((), ())), preferred_element_type=F32)
        sc = jnp.concatenate([sc[:, :BLOCK] + bp, sc[:, BLOCK:2 * BLOCK], sc[:, 2 * BLOCK:] + bn],
                             axis=1)
        sink = jnp.concatenate([jnp.full((BLOCK, 1), sink_ref[h * Q_PER_KV + g] * LOG2_E, F32)
                                for g in range(Q_PER_KV)], axis=0)
        m = jnp.maximum(jnp.max(sc, axis=-1, keepdims=True), sink)
        p = jnp.exp2(sc - m)
        pv = jnp.dot(p.astype(BF16), vb, preferred_element_type=F32)
        den = pv[:, HEAD_DIM:] + jnp.exp2(sink - m)
        o = pv[:, :HEAD_DIM] / den
        for g in range(Q_PER_KV):
            hq = h * Q_PER_KV + g
            ya_scr[s * BLOCK:(s + 1) * BLOCK, hq * HEAD_DIM:(hq + 1) * HEAD_DIM] = (
                o[g * BLOCK:(g + 1) * BLOCK])

    half_heads = GMLP_HEADS // 2

    def gmlp_head(h):
        u_ref = u0_ref if h < half_heads else u1_ref
        v_ref = v0_ref if h < half_heads else v1_ref
        hh = h % half_heads
        u = _gelu(u_ref[0, :, hh * HEAD_DIM:(hh + 1) * HEAD_DIM].astype(F32))
        v = _gelu(v_ref[0, :, hh * HEAD_DIM:(hh + 1) * HEAD_DIM].astype(F32))
        mu = _lane_mean(v)
        dv = v - mu
        var = _lane_mean(dv * dv)
        vn = (dv * lax.rsqrt(var + EPS) * gn_ref[h:h + 1, :]).astype(BF16)
        w = ws_ref[h].astype(BF16)
        bias = bst_ref[:, h:h + 1]
        vcat = jnp.concatenate([vn[c * BLOCK:(c + 1) * BLOCK] for c in range(nsub)], axis=1)
        mixed = jnp.dot(w, vcat, preferred_element_type=F32) + bias
        for cidx in range(nsub):
            sl = slice(cidx * BLOCK, (cidx + 1) * BLOCK)
            yg_scr[sl, h * HEAD_DIM:(h + 1) * HEAD_DIM] = (
                u[sl] * mixed[:, cidx * HEAD_DIM:(cidx + 1) * HEAD_DIM])

    heads_per_block = -(-GMLP_HEADS // (KV_HEADS * nsub))
    next_head = 0
    for h in range(KV_HEADS):
        prepared = prepare_kv_head(h)
        for s in range(nsub):
            attention_block(s, h, *prepared)
            for _ in range(heads_per_block):
                if next_head < GMLP_HEADS:
                    gmlp_head(next_head)
                    next_head += 1
    for h in range(next_head, GMLP_HEADS):
        gmlp_head(h)

    for scr, lo in ((ya_scr, 0), (yg_scr, aw)):
        y = scr[...]
        width = y.shape[1]
        o_ref[0, :, lo:lo + width] = _rms(y, gg_ref[:, lo:lo + width]).astype(BF16)


def _mixer(z, cos, sin, sink, gn, ws, bs, gg):
    assert WINDOW == BLOCK
    bsz, s, _ = z.shape
    tq = min(MIXER_TILE, s)
    nsub = tq // BLOCK
    nblk = s // BLOCK
    aw = ATTN_HEADS * HEAD_DIM
    gw = GMLP_HEADS * HEAD_DIM
    cw = 2 * KV_HEADS * HEAD_DIM
    assert aw % cw == 0 and gw == 2 * cw
    kv = aw // cw
    u0, u1, v0, v1 = kv + 1, kv + 2, kv + 3, kv + 4

    def prev_blk(b, i):
        return (b, jnp.maximum(i * nsub - 1, 0), kv)

    def next_blk(b, i):
        return (b, jnp.minimum((i + 1) * nsub, nblk - 1), kv)

    tab_specs = [pl.BlockSpec((1, BLOCK, HEAD_DIM), lambda b, i: prev_blk(b, i)[:2] + (0,)),
                 pl.BlockSpec((1, tq, HEAD_DIM), lambda b, i: (b, i, 0)),
                 pl.BlockSpec((1, BLOCK, HEAD_DIM), lambda b, i: next_blk(b, i)[:2] + (0,))]
    return pl.pallas_call(
        _mixer_kernel,
        grid=(bsz, s // tq),
        in_specs=[pl.BlockSpec(memory_space=pltpu.SMEM),
                  pl.BlockSpec((1, tq, aw), lambda b, i: (b, i, 0)),
                  pl.BlockSpec((1, BLOCK, cw), prev_blk),
                  pl.BlockSpec((1, tq, cw), lambda b, i: (b, i, kv)),
                  pl.BlockSpec((1, BLOCK, cw), next_blk),
                  pl.BlockSpec((1, tq, cw), lambda b, i: (b, i, u0)),
                  pl.BlockSpec((1, tq, cw), lambda b, i: (b, i, u1)),
                  pl.BlockSpec((1, tq, cw), lambda b, i: (b, i, v0)),
                  pl.BlockSpec((1, tq, cw), lambda b, i: (b, i, v1)),
                  *tab_specs, *tab_specs,
                  pl.BlockSpec((GMLP_HEADS, HEAD_DIM), lambda b, i: (0, 0)),
                  pl.BlockSpec((GMLP_HEADS, BLOCK, BLOCK), lambda b, i: (0, 0, 0)),
                  pl.BlockSpec((BLOCK, GMLP_HEADS), lambda b, i: (0, 0)),
                  pl.BlockSpec((1, aw + gw), lambda b, i: (0, 0))],
        out_specs=pl.BlockSpec((1, tq, aw + gw), lambda b, i: (b, i, 0)),
        out_shape=jax.ShapeDtypeStruct((bsz, s, aw + gw), BF16),
        scratch_shapes=[pltpu.VMEM((tq, aw), F32), pltpu.VMEM((tq, gw), F32)],
        compiler_params=pltpu.CompilerParams(dimension_semantics=("arbitrary", "arbitrary"),
                                             vmem_limit_bytes=VMEM_LIMIT),
        name="mixer",
    )(sink, z, z, z, z, z, z, z, z, cos, cos, cos, sin, sin, sin, gn, ws, bs.T,
      gg.reshape(1, aw + gw))


def _split_bf16(x):
    hi = x.astype(BF16)
    lo = (x - hi.astype(F32)).astype(BF16)
    return hi, lo


def _outproj_kernel(y_ref, w_ref, x_ref, mod_ref, gpost_ref, gpre_ref, wr_ref, br_ref,
                    h_ref, f_ref, eid_ref, ew_ref, mix, wr_split):
    @pl.when((pl.program_id(0) == 0) & (pl.program_id(1) == 0))
    def _():
        w_hi, w_lo = _split_bf16(wr_ref[...])
        wr_split[:, :LANES] = w_hi
        wr_split[:, LANES:] = w_lo

    mix[...] = jnp.dot(y_ref[0], w_ref[...], preferred_element_type=F32)
    for stage in _outproj_finish(mix, x_ref, mod_ref, gpost_ref, gpre_ref, wr_split, br_ref,
                                 h_ref, f_ref, eid_ref, ew_ref):
        stage()


def _outproj_finish(mix_ref, x_ref, mod_ref, gpost_ref, gpre_ref, wr_ref, br_ref,
                    h_ref, f_ref, eid_ref, ew_ref):
    state = {}

    def residual():
        h1 = x_ref[0] + _rms(mix_ref[...], mod_ref[0][2:3] * gpost_ref[...])
        h_ref[0] = h1
        state["h1"] = h1

    def prenorm():
        mod = mod_ref[0]
        f = _rms(state["h1"], gpre_ref[...] * (1.0 + mod[4:5])) + mod[3:4]
        f_ref[...] = f
        state["f"] = f

    def logits():
        f_hi, f_lo = _split_bf16(state["f"])
        tm = f_hi.shape[0]
        r = jnp.dot(jnp.concatenate([f_hi, f_lo], axis=0), wr_ref[...],
                    preferred_element_type=F32)
        lg = (r[:tm, :LANES] + r[:tm, LANES:]) + (r[tm:, :LANES] + r[tm:, LANES:])
        state["logits"] = lg.T[:ROUTER_ROWS] + br_ref[...]

    def route():
        _route(state["logits"], eid_ref, ew_ref)

    return [residual, prenorm, logits, route]


def _route(logits, eid_ref, ew_ref):
    gl = [logits[g:g + 1] for g in range(N_GROUPS)]
    gmax = functools.reduce(jnp.maximum, gl)
    gidx = jnp.full(gmax.shape, N_GROUPS - 1, I32)
    for g in range(N_GROUPS - 2, -1, -1):
        gidx = jnp.where(gl[g] == gmax, g, gidx)
    gval = 1.0 / functools.reduce(lambda a, b: a + b, [jnp.exp(v - gmax) for v in gl])

    es = []
    for e in range(EXPERTS_PER_GROUP):
        r = N_GROUPS + (N_GROUPS - 1) * EXPERTS_PER_GROUP + e
        v = logits[r:r + 1]
        for g in range(N_GROUPS - 2, -1, -1):
            r = N_GROUPS + g * EXPERTS_PER_GROUP + e
            v = jnp.where(gidx == g, logits[r:r + 1], v)
        es.append(v)
    m1 = functools.reduce(jnp.maximum, es)
    i1 = jnp.full(m1.shape, EXPERTS_PER_GROUP - 1, I32)
    for e in range(EXPERTS_PER_GROUP - 2, -1, -1):
        i1 = jnp.where(es[e] == m1, e, i1)
    rest = [jnp.where(i1 == e, -jnp.inf, es[e]) for e in range(EXPERTS_PER_GROUP)]
    m2 = functools.reduce(jnp.maximum, rest)
    i2 = jnp.full(m2.shape, EXPERTS_PER_GROUP - 1, I32)
    for e in range(EXPERTS_PER_GROUP - 2, -1, -1):
        i2 = jnp.where(rest[e] == m2, e, i2)
    p2 = jnp.exp(m2 - m1)
    w1 = gval / (1.0 + p2)
    w2 = gval * p2 / (1.0 + p2)
    eid_ref[0:1, :] = gidx * EXPERTS_PER_GROUP + i1
    eid_ref[1:2, :] = gidx * EXPERTS_PER_GROUP + i2
    ew_ref[0:1, :] = w1
    ew_ref[1:2, :] = w2


def _outproj(ycat, w_out_bf16, x, mod, gpost, gpre, wr, br):
    bsz, s, d = x.shape
    t = bsz * s
    tm = min(OUTPROJ_TILE, s)
    nti = s // tm
    return pl.pallas_call(
        _outproj_kernel,
        grid=(bsz, nti),
        in_specs=[pl.BlockSpec((1, tm, d), lambda b, i: (b, i, 0)),
                  pl.BlockSpec((d, d), lambda b, i: (0, 0)),
                  pl.BlockSpec((1, tm, d), lambda b, i: (b, i, 0)),
                  pl.BlockSpec((1, N_MOD, d), lambda b, i: (b, 0, 0)),
                  pl.BlockSpec((1, d), lambda b, i: (0, 0)),
                  pl.BlockSpec((1, d), lambda b, i: (0, 0)),
                  pl.BlockSpec((d, LANES), lambda b, i: (0, 0)),
                  pl.BlockSpec((ROUTER_ROWS, 1), lambda b, i: (0, 0))],
        out_specs=[pl.BlockSpec((1, tm, d), lambda b, i: (b, i, 0)),
                   pl.BlockSpec((tm, d), lambda b, i: (b * nti + i, 0)),
                   pl.BlockSpec((TOP_K, tm), lambda b, i: (0, b * nti + i)),
                   pl.BlockSpec((TOP_K, tm), lambda b, i: (0, b * nti + i))],
        out_shape=[jax.ShapeDtypeStruct((bsz, s, d), F32),
                   jax.ShapeDtypeStruct((t, d), F32),
                   jax.ShapeDtypeStruct((TOP_K, t), I32),
                   jax.ShapeDtypeStruct((TOP_K, t), F32)],
        scratch_shapes=[pltpu.VMEM((tm, d), F32), pltpu.VMEM((d, 2 * LANES), BF16)],
        compiler_params=pltpu.CompilerParams(dimension_semantics=("arbitrary", "arbitrary"),
                                             vmem_limit_bytes=LARGE_VMEM_LIMIT),
        name="outproj",
    )(ycat, w_out_bf16, x, mod, gpost.reshape(1, d), gpre.reshape(1, d), wr, br)


def _slots_kernel(eid_ref, pos_ref, tab_ref, rank_scr, *, n_slots):
    t = eid_ref.shape[1]
    chunk = min(512, t)
    nchunk = t // chunk
    tri = jnp.where(lax.broadcasted_iota(I32, (chunk, chunk), 0)
                    <= lax.broadcasted_iota(I32, (chunk, chunk), 1), 1.0, 0.0).astype(BF16)
    e_io = lax.broadcasted_iota(I32, (N_EXPERTS, chunk), 0)

    cnt = jnp.zeros((N_EXPERTS, 1), F32)
    for k in range(TOP_K):
        def rank_body(c, carry, k=k):
            off = pl.multiple_of(c * chunk, chunk)
            onehot = e_io == eid_ref[pl.ds(k, 1), pl.ds(off, chunk)]
            ones = jnp.where(onehot, 1.0, 0.0)
            prefix = jnp.dot(ones.astype(BF16), tri, preferred_element_type=F32) + carry
            rank = jnp.sum(jnp.where(onehot, prefix, 0.0), axis=0, keepdims=True) - 1.0
            rank_scr[pl.ds(k, 1), pl.ds(off, chunk)] = rank
            return carry + jnp.sum(ones, axis=1, keepdims=True)
        cnt = lax.fori_loop(0, nchunk, rank_body, cnt)

    padded = jnp.floor((cnt + (SLOT_TILE - 1)) * (1.0 / SLOT_TILE)) * SLOT_TILE
    sub = lax.broadcasted_iota(I32, (N_EXPERTS, LANES), 0)
    lan = lax.broadcasted_iota(I32, (N_EXPERTS, LANES), 1)
    padded_row = jnp.sum(jnp.where(sub == lan, padded, 0.0), axis=0, keepdims=True)
    start = jnp.sum(jnp.where(lan < sub, padded_row, 0.0), axis=1, keepdims=True)
    end = start + padded
    pad_lo = start + cnt

    for k in range(TOP_K):
        def pos_body(c, carry, k=k):
            off = pl.multiple_of(c * chunk, chunk)
            onehot = e_io == eid_ref[pl.ds(k, 1), pl.ds(off, chunk)]
            base = jnp.sum(jnp.where(onehot, start + SLOT_TILE, 0.0), axis=0, keepdims=True)
            pos_ref[pl.ds(k, 1), pl.ds(off, chunk)] = (
                base + rank_scr[pl.ds(k, 1), pl.ds(off, chunk)]).astype(I32)
            return carry
        lax.fori_loop(0, nchunk, pos_body, 0)

    lo_row = jnp.sum(jnp.where(sub == lan, pad_lo, 0.0), axis=0, keepdims=True)
    hi_row = jnp.sum(jnp.where(sub + PAD_HI_LANE == lan, end, 0.0), axis=0, keepdims=True)
    first_row = jnp.sum(jnp.where(sub + TILE_START_LANE == lan, start * (1.0 / SLOT_TILE), 0.0),
                        axis=0, keepdims=True)
    lane_row = lax.broadcasted_iota(I32, (1, LANES), 1)
    total = jnp.sum(padded_row, axis=1, keepdims=True)
    tail = (jnp.where(lane_row == N_EXPERTS, total, 0.0)
            + jnp.where(lane_row == PAD_HI_LANE + N_EXPERTS, float(n_slots), 0.0)
            + jnp.where(lane_row == TILE_START_LANE + N_EXPERTS, total * (1.0 / SLOT_TILE), 0.0))
    tab_ref[...] = (lo_row + hi_row + first_row + tail).astype(I32)


def _slots(eid, n_tiles):
    t = eid.shape[1]
    return pl.pallas_call(
        functools.partial(_slots_kernel, n_slots=n_tiles * SLOT_TILE),
        out_shape=[jax.ShapeDtypeStruct((TOP_K, t), I32),
                   jax.ShapeDtypeStruct((1, LANES), I32)],
        scratch_shapes=[pltpu.VMEM((TOP_K, t), F32)],
        compiler_params=pltpu.CompilerParams(vmem_limit_bytes=VMEM_LIMIT),
        name="slots",
    )(eid)


def _invert_kernel(pos_ref, tab_ref, inv_ref, *, n_pairs, n_slots):
    group = 8
    spare = n_pairs
    for e in range(N_EXPERTS + 1):
        lo, hi = tab_ref[e], tab_ref[PAD_HI_LANE + e]

        def fill(i, c, lo=lo, spare=spare):
            for k in range(group):
                inv_ref[SLOT_TILE + lo + i * group + k] = spare + i * group + k
            return c
        lax.fori_loop(0, (hi - lo + group - 1) // group, fill, 0)
        spare = spare + hi - lo

    def guard(r, c):
        inv_ref[r] = n_slots + r
        inv_ref[SLOT_TILE + n_slots + r] = n_slots + r
        return c
    lax.fori_loop(0, SLOT_TILE, guard, 0, unroll=8)

    def place(n, c):
        inv_ref[pos_ref[n]] = n
        return c
    lax.fori_loop(0, n_pairs, place, 0, unroll=16)


def _invert(pos_flat, tab, n_tiles):
    n_pairs = pos_flat.shape[0]
    n_slots = n_tiles * SLOT_TILE
    return pl.pallas_call(
        functools.partial(_invert_kernel, n_pairs=n_pairs, n_slots=n_slots),
        in_specs=[pl.BlockSpec(memory_space=pltpu.SMEM), pl.BlockSpec(memory_space=pltpu.SMEM)],
        out_specs=pl.BlockSpec(memory_space=pltpu.SMEM),
        out_shape=jax.ShapeDtypeStruct((n_slots + 2 * SLOT_TILE,), I32),
        name="invert",
    )(pos_flat, tab)


def _experts_kernel(inv_ref, tab_ref, f_ref, wg_ref, wu_ref, wd_ref, y_ref,
                    wbg, wbu, wbd, xa, xb, ya, yb, gsem, ssem, zsem, *, n_tokens, n_slots):
    e = pl.program_id(0)
    j = pl.program_id(1)
    fc = wg_ref.shape[2]
    total = tab_ref[TILE_START_LANE + N_EXPERTS]

    def gather_copy(tile, r, xbuf):
        v = inv_ref[(tile + 1) * SLOT_TILE + r]
        return pltpu.make_async_copy(f_ref.at[pl.ds(v & (n_tokens - 1), 1)], xbuf.at[pl.ds(r, 1)], gsem)

    def scatter_copy(tile, r, ybuf):
        v = inv_ref[(tile + 1) * SLOT_TILE + r]
        return pltpu.make_async_copy(ybuf.at[pl.ds(r, 1)], y_ref.at[pl.ds(v, 1)], ssem)

    def gather_wait(xbuf):
        pltpu.make_async_copy(f_ref.at[pl.ds(0, SLOT_TILE)], xbuf, gsem).wait()

    def scatter_wait(ybuf):
        pltpu.make_async_copy(ybuf, y_ref.at[pl.ds(0, SLOT_TILE)], ssem).wait()

    def rows(fn):
        def body(r, c):
            fn(r)
            return c
        lax.fori_loop(0, SLOT_TILE, body, 0, unroll=8)

    @pl.when((e == 0) & (j == 0))
    def _():
        ya[...] = jnp.zeros(ya.shape, ya.dtype)
        yb[...] = jnp.zeros(yb.shape, yb.dtype)
        rows(lambda r: gather_copy(0, r, xa).start())

        def zero_copy(tile):
            dst = pl.multiple_of(tile * SLOT_TILE, SLOT_TILE)
            return pltpu.make_async_copy(ya, y_ref.at[pl.ds(dst, SLOT_TILE)], zsem)

        def zstart(tile, c):
            zero_copy(tile).start()
            return c

        def zwait(tile, c):
            zero_copy(tile).wait()
            return c
        lax.fori_loop(total, n_slots // SLOT_TILE, zstart, 0)
        lax.fori_loop(total, n_slots // SLOT_TILE, zwait, 0)

    @pl.when(e < N_EXPERTS)
    def _():
        slot = e % 2
        col = pl.multiple_of(j * fc, fc)
        wbg[slot, :, pl.ds(col, fc)] = wg_ref[0].astype(BF16)
        wbu[slot, :, pl.ds(col, fc)] = wu_ref[0].astype(BF16)
        wbd[slot, pl.ds(col, fc), :] = wd_ref[0].astype(BF16)

    prev = jnp.maximum(e - 1, 0)
    first = tab_ref[TILE_START_LANE + prev]
    count = jnp.where(e == 0, 0, tab_ref[TILE_START_LANE + prev + 1] - first)
    lo = first + (count * j) // N_WCHUNK
    hi = first + (count * (j + 1)) // N_WCHUNK
    wslot = prev % 2

    def tile(t, c):
        def run(xcur, xnext, ycur, yprev, ws):
            gather_wait(xcur)
            for r in range(SLOT_TILE):
                gather_copy(t + 1, r, xnext).start()
            for r in range(SLOT_TILE):
                scatter_copy(t - 1, r, yprev).start()
            x = xcur[...].astype(BF16)
            hg = jnp.dot(x, wbg[ws], preferred_element_type=F32)
            hu = jnp.dot(x, wbu[ws], preferred_element_type=F32)
            hid = (_silu(hg) * hu).astype(BF16)
            ycur[...] = jnp.dot(hid, wbd[ws], preferred_element_type=F32)
            scatter_wait(yprev)

        for ws in range(2):
            for parity, bufs in enumerate(((xa, xb, ya, yb), (xb, xa, yb, ya))):
                @pl.when((wslot == ws) & (t % 2 == parity))
                def _(bufs=bufs, ws=ws):
                    run(*bufs, ws)
        return c
    lax.fori_loop(lo, hi, tile, 0)

    @pl.when((e == N_EXPERTS) & (j == N_WCHUNK - 1))
    def _():
        last = total - 1

        def finish(xnext, ylast):
            gather_wait(xnext)
            rows(lambda r: scatter_copy(last, r, ylast).start())
            scatter_wait(ylast)

        @pl.when(last % 2 == 0)
        def _():
            finish(xb, ya)

        @pl.when(last % 2 == 1)
        def _():
            finish(xa, yb)


def _experts(inv, tab, f, wg, wu, wd, n_tiles):
    t, d = f.shape
    assert t & (t - 1) == 0
    fdim = wg.shape[2]
    fc = fdim // N_WCHUNK
    n_slots = n_tiles * SLOT_TILE

    def expert(e):
        return jnp.minimum(e, N_EXPERTS - 1)

    def chunk(e, j):
        return jnp.where(e == N_EXPERTS, N_WCHUNK - 1, j)

    return pl.pallas_call(
        functools.partial(_experts_kernel, n_tokens=t, n_slots=n_slots),
        grid_spec=pltpu.PrefetchScalarGridSpec(
            num_scalar_prefetch=2,
            grid=(N_EXPERTS + 1, N_WCHUNK),
            in_specs=[pl.BlockSpec(memory_space=pl.ANY),
                      pl.BlockSpec((1, d, fc), lambda e, j, inv, tab: (expert(e), 0, chunk(e, j))),
                      pl.BlockSpec((1, d, fc), lambda e, j, inv, tab: (expert(e), 0, chunk(e, j))),
                      pl.BlockSpec((1, fc, d), lambda e, j, inv, tab: (expert(e), chunk(e, j), 0))],
            out_specs=pl.BlockSpec(memory_space=pl.ANY),
            scratch_shapes=[pltpu.VMEM((2, d, fdim), BF16), pltpu.VMEM((2, d, fdim), BF16),
                            pltpu.VMEM((2, fdim, d), BF16),
                            pltpu.VMEM((SLOT_TILE, d), F32), pltpu.VMEM((SLOT_TILE, d), F32),
                            pltpu.VMEM((SLOT_TILE, d), F32), pltpu.VMEM((SLOT_TILE, d), F32),
                            pltpu.SemaphoreType.DMA, pltpu.SemaphoreType.DMA, pltpu.SemaphoreType.DMA]),
        out_shape=jax.ShapeDtypeStruct((n_slots + SLOT_TILE, d), F32),
        compiler_params=pltpu.CompilerParams(dimension_semantics=("arbitrary", "arbitrary"),
                                             vmem_limit_bytes=LARGE_VMEM_LIMIT),
        name="experts",
    )(inv, tab, f, wg, wu, wd)


def _final_kernel(y0_ref, y1_ref, h_ref, ew_ref, mod_ref, g_ref, o_ref):
    ew = ew_ref[...]
    y = ew[:, 0:1] * y0_ref[...] + ew[:, 1:2] * y1_ref[...]
    o_ref[0] = h_ref[0] + _rms(y, mod_ref[0][5:6] * g_ref[...])


def _final(y_pairs, h1, ew_t, mod, g):
    bsz, s, d = h1.shape
    t = bsz * s
    tg = min(FINAL_TILE, s)
    nti = s // tg
    stream = pl.Buffered(STREAM_BUFFERS)
    pipeline = pltpu.emit_pipeline(
        _final_kernel,
        grid=(bsz, nti),
        in_specs=[pl.BlockSpec((tg, d), lambda b, i: (b * nti + i, 0), pipeline_mode=stream),
                  pl.BlockSpec((tg, d), lambda b, i: (t // tg + b * nti + i, 0), pipeline_mode=stream),
                  pl.BlockSpec((1, tg, d), lambda b, i: (b, i, 0), pipeline_mode=stream),
                  pl.BlockSpec((tg, TOP_K), lambda b, i: (b * nti + i, 0)),
                  pl.BlockSpec((1, N_MOD, d), lambda b, i: (b, 0, 0)),
                  pl.BlockSpec((1, d), lambda b, i: (0, 0))],
        out_specs=[pl.BlockSpec((1, tg, d), lambda b, i: (b, i, 0))],
    )

    def outer(y0_hbm, y1_hbm, h_hbm, ew_hbm, mod_hbm, g_hbm, o_hbm):
        pipeline(y0_hbm, y1_hbm, h_hbm, ew_hbm, mod_hbm, g_hbm, o_hbm)

    return pl.pallas_call(
        outer,
        in_specs=[pl.BlockSpec(memory_space=pl.ANY)] * 6,
        out_specs=pl.BlockSpec(memory_space=pl.ANY),
        out_shape=jax.ShapeDtypeStruct((bsz, s, d), F32),
        compiler_params=pltpu.CompilerParams(vmem_limit_bytes=VMEM_LIMIT),
        name="final",
    )(y_pairs, y_pairs, h1, ew_t, mod, g.reshape(1, d))


def kernel(x, c, positions, w_ada, b_ada, g_mix_pre, g_mix_post, w_in, sink_logits, gmlp_norm_g,
           w_spatial, b_spatial, g_group_out, w_out, g_ffn_pre, g_ffn_post, w_router_group,
           b_router_group, w_router_expert, b_router_expert, w_gate, w_up, w_down):
    bsz, s, d = x.shape
    t = bsz * s
    depth = w_ada.shape[0]
    fdim = w_gate.shape[-1]
    n_tiles = (TOP_K * t) // SLOT_TILE + N_EXPERTS
    n_slots = n_tiles * SLOT_TILE
    cos, sin = _rope_tables(positions)
    h = x
    for l in range(depth):
        mod = _ada(c, w_ada[l], b_ada[l]).reshape(bsz, N_MOD, d)
        z = _inproj(h, mod, g_mix_pre[l], w_in[l].astype(BF16))
        ycat = _mixer(z, cos, sin, sink_logits[l], gmlp_norm_g[l], w_spatial[l],
                      b_spatial[l], g_group_out[l])
        wr = jnp.concatenate([w_router_group[l], w_router_expert[l].reshape(d, N_EXPERTS)], axis=1)
        wr = jnp.pad(wr, ((0, 0), (0, LANES - N_GROUPS - N_EXPERTS)))
        br = jnp.concatenate([b_router_group[l], b_router_expert[l].reshape(N_EXPERTS)])
        br = jnp.pad(br, (0, ROUTER_ROWS - N_GROUPS - N_EXPERTS)).reshape(ROUTER_ROWS, 1)
        h1, f, eid, ew = _outproj(ycat, w_out[l].astype(BF16), h, mod, g_mix_post[l], g_ffn_pre[l],
                                  wr, br)
        pos, tab = _slots(eid, n_tiles)
        tab = tab.reshape(LANES)
        inv = _invert(pos.reshape(TOP_K * t), tab, n_tiles)
        y_pairs = _experts(inv, tab, f,
                           w_gate[l].reshape(N_EXPERTS, d, fdim),
                           w_up[l].reshape(N_EXPERTS, d, fdim),
                           w_down[l].reshape(N_EXPERTS, fdim, d), n_tiles)
        h = _final(y_pairs, h1, ew.T, mod, g_ffn_post[l])
    return h
```

```python
import functools

import jax
import jax.numpy as jnp
from jax import lax
from jax.experimental import pallas as pl
from jax.experimental.pallas import tpu as pltpu

F32 = jnp.float32
BF16 = jnp.bfloat16
I32 = jnp.int32

HEAD_DIM = 128
ATTN_HEADS = 8
KV_HEADS = 2
Q_PER_KV = ATTN_HEADS // KV_HEADS
GMLP_HEADS = 8
WINDOW = 128
BLOCK = 128
ROPE_THETA = 500000.0
ROPE_DIM = HEAD_DIM // 4
ROPE_HALF = ROPE_DIM // 2
N_GROUPS = 4
EXPERTS_PER_GROUP = 4
N_EXPERTS = N_GROUPS * EXPERTS_PER_GROUP
TOP_K = 2
N_MOD = 6
EPS = 1e-6
NEG_INF = -1e30
LOG2_E = 1.4426950408889634
LANES = 128
BF16_SUBLANES = 16
PAD_HI_LANE = 32
TILE_START_LANE = 64
N_WCHUNK = 4
ROUTER_ROWS = 32
SLOT_TILE = 256

ADA_COL_TILE = 1024
INPROJ_ROW_TILE = 2048
INPROJ_COL_TILE = 512
INPROJ_SLICES = 4
MIXER_TILE = 1024
OUTPROJ_TILE = 512
FINAL_TILE = 512
STREAM_BUFFERS = 3
VMEM_LIMIT = 52 * 1024 * 1024
LARGE_VMEM_LIMIT = 58 * 1024 * 1024


def _rms(x, g):
    ms = jnp.mean(x * x, axis=-1, keepdims=True)
    return x * lax.rsqrt(ms + EPS) * g


def _silu(x):
    return x / (1.0 + jnp.exp(-x))


def _gelu(x):
    return 0.5 * x * (1.0 + lax.erf(x * 0.7071067811865476))


def _ada_kernel(c_ref, w_ref, b_ref, o_ref):
    ca = _silu(c_ref[...]).astype(BF16)
    o_ref[...] = jnp.dot(ca, w_ref[...].astype(BF16), preferred_element_type=F32) + b_ref[...]


def _ada(c, w, b):
    bsz, d = c.shape
    n = w.shape[1]
    tn = ADA_COL_TILE
    return pl.pallas_call(
        _ada_kernel,
        grid=(n // tn,),
        in_specs=[pl.BlockSpec((bsz, d), lambda j: (0, 0)),
                  pl.BlockSpec((d, tn), lambda j: (0, j)),
                  pl.BlockSpec((1, tn), lambda j: (0, j))],
        out_specs=pl.BlockSpec((bsz, tn), lambda j: (0, j)),
        out_shape=jax.ShapeDtypeStruct((bsz, n), F32),
        compiler_params=pltpu.CompilerParams(dimension_semantics=("arbitrary",),
                                             vmem_limit_bytes=VMEM_LIMIT),
        name="ada",
    )(c, w, b.reshape(1, n))


def _inproj_kernel(x_ref, mod_ref, g_ref, w_ref, o_ref, a_scr, *, n_tiles):
    g = pl.program_id(0)
    j = pl.program_id(1)
    rows = x_ref.shape[1]

    @pl.when((g < n_tiles) & (j % 2 == 0) & (j < 2 * INPROJ_SLICES))
    def _():
        mod = mod_ref[0]
        a = _rms(x_ref[0], g_ref[...] * (1.0 + mod[1:2])) + mod[0:1]
        start = pl.multiple_of((j // 2) * rows, rows)
        a_scr[g % 2, pl.ds(start, rows), :] = a.astype(BF16)

    @pl.when(g > 0)
    def _():
        o_ref[0] = jnp.dot(a_scr[(g - 1) % 2], w_ref[...], preferred_element_type=F32).astype(BF16)


def _inproj(x, mod, g, w_bf16):
    bsz, s, d = x.shape
    n = w_bf16.shape[1]
    tm = min(INPROJ_ROW_TILE, s)
    tn = INPROJ_COL_TILE
    nj = n // tn
    nti = s // tm
    ntile = bsz * nti
    assert 2 * (INPROJ_SLICES - 1) < nj
    rows = tm // INPROJ_SLICES

    def norm_tile(g):
        gg = jnp.minimum(g, ntile - 1)
        return gg // nti, gg % nti

    def mm_tile(g):
        gg = jnp.maximum(g - 1, 0)
        return gg // nti, gg % nti

    def x_block(g, j):
        b, i = norm_tile(g)
        return b, i * INPROJ_SLICES + jnp.minimum(j // 2, INPROJ_SLICES - 1), 0

    return pl.pallas_call(
        functools.partial(_inproj_kernel, n_tiles=ntile),
        grid=(ntile + 1, nj),
        in_specs=[pl.BlockSpec((1, rows, d), x_block),
                  pl.BlockSpec((1, N_MOD, d), lambda g, j: (norm_tile(g)[0], 0, 0)),
                  pl.BlockSpec((1, d), lambda g, j: (0, 0)),
                  pl.BlockSpec((d, tn), lambda g, j: (0, j))],
        out_specs=pl.BlockSpec((1, tm, tn), lambda g, j: mm_tile(g) + (jnp.where(g == 0, 0, j),)),
        out_shape=jax.ShapeDtypeStruct((bsz, s, n), BF16),
        scratch_shapes=[pltpu.VMEM((2, tm, d), BF16)],
        compiler_params=pltpu.CompilerParams(dimension_semantics=("arbitrary", "arbitrary"),
                                             vmem_limit_bytes=VMEM_LIMIT),
        name="inproj",
    )(x, mod, g.reshape(1, d), w_bf16)


def _angles_kernel(pos_ref, invf_ref, cos_ref, sin_ref):
    rows = pos_ref.shape[0]
    per_row = LANES // ROPE_HALF
    ang = pos_ref[...].astype(F32) * invf_ref[...]
    src = lax.broadcasted_iota(I32, (LANES, LANES), 0)
    dst = lax.broadcasted_iota(I32, (LANES, LANES), 1)
    rotary_lane = lax.broadcasted_iota(I32, (1, LANES), 1) < ROPE_DIM
    for out_ref, val, fill in ((cos_ref, jnp.cos(ang), 1.0), (sin_ref, jnp.sin(ang), 0.0)):
        p0 = val.astype(BF16)
        r0 = val - p0.astype(F32)
        p1 = r0.astype(BF16)
        p2 = (r0 - p1.astype(F32)).astype(BF16)
        rest = jnp.where(rotary_lane, 0.0, fill)
        for a in range(per_row):
            sel = jnp.where((dst < ROPE_DIM) & ((dst & (ROPE_HALF - 1)) + a * ROPE_HALF == src),
                            1.0, 0.0).astype(BF16)
            spread = (jnp.dot(p0, sel, preferred_element_type=F32)
                      + jnp.dot(p1, sel, preferred_element_type=F32)
                      + jnp.dot(p2, sel, preferred_element_type=F32))
            out_ref[pl.ds(a, rows, stride=per_row), :] = spread + rest


def _rope_tables(positions):
    bsz, s = positions.shape
    per_row = LANES // ROPE_HALF
    rows = bsz * s // per_row
    inv = ROPE_THETA ** (-jnp.arange(ROPE_HALF, dtype=F32) * 2.0 / ROPE_DIM)
    invf = jnp.tile(inv, per_row).reshape(1, LANES)
    pos_rep = jnp.repeat(positions.reshape(rows, per_row), ROPE_HALF, axis=1)
    cos, sin = pl.pallas_call(
        _angles_kernel,
        out_shape=[jax.ShapeDtypeStruct((bsz * s, LANES), F32)] * 2,
        name="angles",
    )(pos_rep, invf)
    return cos.reshape(bsz, s, LANES), sin.reshape(bsz, s, LANES)


def _lane_mean(x):
    k = x.shape[1]
    ones = jnp.full((k, LANES), 1.0 / k, BF16)
    return jnp.dot(x.astype(BF16), ones, preferred_element_type=F32)


def _rope(x, cos, sin):
    lane = lax.broadcasted_iota(I32, x.shape, 1)
    partner = jnp.where(lane < ROPE_HALF,
                        pltpu.roll(x, HEAD_DIM - ROPE_HALF, 1),
                        pltpu.roll(x, ROPE_HALF, 1))
    return x * cos + partner * sin


def _mixer_kernel(sink_ref, q_ref, kvp_ref, kvc_ref, kvn_ref, u0_ref, u1_ref, v0_ref, v1_ref,
                  cosp_ref, cosc_ref, cosn_ref, sinp_ref, sinc_ref, sinn_ref,
                  gn_ref, ws_ref, bst_ref, gg_ref, o_ref, ya_scr, yg_scr):
    tq = q_ref.shape[1]
    nsub = tq // BLOCK
    aw = ATTN_HEADS * HEAD_DIM
    i = pl.program_id(1)
    sign = jnp.where(lax.broadcasted_iota(I32, (1, HEAD_DIM), 1) < ROPE_HALF, -1.0, 1.0)

    cos_c, sin_c = cosc_ref[0], sinc_ref[0] * sign
    tabs = ((cosp_ref[0], sinp_ref[0] * sign), (cos_c, sin_c), (cosn_ref[0], sinn_ref[0] * sign))
    scale = HEAD_DIM ** -0.5 * LOG2_E
    cos_q, sin_q = cos_c * scale, sin_c * scale

    def prepare_kv_head(h):
        kparts, vparts = [], []
        for ref, (cs, sn) in zip((kvp_ref, kvc_ref, kvn_ref), tabs):
            k = ref[0, :, h * HEAD_DIM:(h + 1) * HEAD_DIM].astype(F32)
            kparts.append(_rope(k, cs, sn).astype(BF16))
            vparts.append(ref[0, :, (KV_HEADS + h) * HEAD_DIM:(KV_HEADS + h + 1) * HEAD_DIM])
        kband = jnp.concatenate(kparts, axis=0)
        vband = jnp.concatenate(
            [jnp.concatenate(vparts, axis=0), jnp.ones((tq + 2 * BLOCK, HEAD_DIM), BF16)], axis=1)
        return kband, vband

    rows = Q_PER_KV * BLOCK
    band = 3 * BLOCK
    q_off = lax.broadcasted_iota(I32, (rows, BLOCK), 0) & (BLOCK - 1)
    k_off = lax.broadcasted_iota(I32, (rows, BLOCK), 1)
    bias_prev = jnp.where(k_off >= q_off, 0.0, NEG_INF)
    bias_next = jnp.where(k_off <= q_off, 0.0, NEG_INF)
    first_tile = i == 0
    last_tile = i == pl.num_programs(1) - 1
    def attention_block(s, h, kband, vband):
        bp = jnp.where(first_tile, NEG_INF, bias_prev) if s == 0 else bias_prev
        bn = jnp.where(last_tile, NEG_INF, bias_next) if s == nsub - 1 else bias_next
        blk = slice(s * BLOCK, (s + 1) * BLOCK)
        q4 = jnp.concatenate(
            [_rope(q_ref[0, blk, hq * HEAD_DIM:(hq + 1) * HEAD_DIM].astype(F32),
                   cos_q[blk], sin_q[blk]).astype(BF16)
             for hq in range(h * Q_PER_KV, (h + 1) * Q_PER_KV)], axis=0)
        kb = kband[s * BLOCK:s * BLOCK + band]
        vb = vband[s * BLOCK:s * BLOCK + band]
        sc = lax.dot_general(q4, kb, (((1,), (1,)), ((), ())), preferred_element_type=F32)
        sc = jnp.concatenate([sc[:, :BLOCK] + bp, sc[:, BLOCK:2 * BLOCK], sc[:, 2 * BLOCK:] + bn],
                             axis=1)
        sink = jnp.concatenate([jnp.full((BLOCK, 1), sink_ref[h * Q_PER_KV + g] * LOG2_E, F32)
                                for g in range(Q_PER_KV)], axis=0)
        m = jnp.maximum(jnp.max(sc, axis=-1, keepdims=True), sink)
        p = jnp.exp2(sc - m)
        pv = jnp.dot(p.astype(BF16), vb, preferred_element_type=F32)
        den = pv[:, HEAD_DIM:] + jnp.exp2(sink - m)
        o = pv[:, :HEAD_DIM] / den
        for g in range(Q_PER_KV):
            hq = h * Q_PER_KV + g
            ya_scr[s * BLOCK:(s + 1) * BLOCK, hq * HEAD_DIM:(hq + 1) * HEAD_DIM] = (
                o[g * BLOCK:(g + 1) * BLOCK])

    half_heads = GMLP_HEADS // 2

    def gmlp_head(h):
        u_ref = u0_ref if h < half_heads else u1_ref
        v_ref = v0_ref if h < half_heads else v1_ref
        hh = h % half_heads
        u = _gelu(u_ref[0, :, hh * HEAD_DIM:(hh + 1) * HEAD_DIM].astype(F32))
        v = _gelu(v_ref[0, :, hh * HEAD_DIM:(hh + 1) * HEAD_DIM].astype(F32))
        mu = _lane_mean(v)
        dv = v - mu
        var = _lane_mean(dv * dv)
        vn = (dv * lax.rsqrt(var + EPS) * gn_ref[h:h + 1, :]).astype(BF16)
        w = ws_ref[h].astype(BF16)
        bias = bst_ref[:, h:h + 1]
        vcat = jnp.concatenate([vn[c * BLOCK:(c + 1) * BLOCK] for c in range(nsub)], axis=1)
        mixed = jnp.dot(w, vcat, preferred_element_type=F32) + bias
        for cidx in range(nsub):
            sl = slice(cidx * BLOCK, (cidx + 1) * BLOCK)
            yg_scr[sl, h * HEAD_DIM:(h + 1) * HEAD_DIM] = (
                u[sl] * mixed[:, cidx * HEAD_DIM:(cidx + 1) * HEAD_DIM])

    heads_per_block = -(-GMLP_HEADS // (KV_HEADS * nsub))
    next_head = 0
    for h in range(KV_HEADS):
        prepared = prepare_kv_head(h)
        for s in range(nsub):
            attention_block(s, h, *prepared)
            for _ in range(heads_per_block):
                if next_head < GMLP_HEADS:
                    gmlp_head(next_head)
                    next_head += 1
    for h in range(next_head, GMLP_HEADS):
        gmlp_head(h)

    for scr, lo in ((ya_scr, 0), (yg_scr, aw)):
        y = scr[...]
        width = y.shape[1]
        o_ref[0, :, lo:lo + width] = _rms(y, gg_ref[:, lo:lo + width]).astype(BF16)


def _mixer(z, cos, sin, sink, gn, ws, bs, gg):
    assert WINDOW == BLOCK
    bsz, s, _ = z.shape
    tq = min(MIXER_TILE, s)
    nsub = tq // BLOCK
    nblk = s // BLOCK
    aw = ATTN_HEADS * HEAD_DIM
    gw = GMLP_HEADS * HEAD_DIM
    cw = 2 * KV_HEADS * HEAD_DIM
    assert aw % cw == 0 and gw == 2 * cw
    kv = aw // cw
    u0, u1, v0, v1 = kv + 1, kv + 2, kv + 3, kv + 4

    def prev_blk(b, i):
        return (b, jnp.maximum(i * nsub - 1, 0), kv)

    def next_blk(b, i):
        return (b, jnp.minimum((i + 1) * nsub, nblk - 1), kv)

    tab_specs = [pl.BlockSpec((1, BLOCK, HEAD_DIM), lambda b, i: prev_blk(b, i)[:2] + (0,)),
                 pl.BlockSpec((1, tq, HEAD_DIM), lambda b, i: (b, i, 0)),
                 pl.BlockSpec((1, BLOCK, HEAD_DIM), lambda b, i: next_blk(b, i)[:2] + (0,))]
    return pl.pallas_call(
        _mixer_kernel,
        grid=(bsz, s // tq),
        in_specs=[pl.BlockSpec(memory_space=pltpu.SMEM),
                  pl.BlockSpec((1, tq, aw), lambda b, i: (b, i, 0)),
                  pl.BlockSpec((1, BLOCK, cw), prev_blk),
                  pl.BlockSpec((1, tq, cw), lambda b, i: (b, i, kv)),
                  pl.BlockSpec((1, BLOCK, cw), next_blk),
                  pl.BlockSpec((1, tq, cw), lambda b, i: (b, i, u0)),
                  pl.BlockSpec((1, tq, cw), lambda b, i: (b, i, u1)),
                  pl.BlockSpec((1, tq, cw), lambda b, i: (b, i, v0)),
                  pl.BlockSpec((1, tq, cw), lambda b, i: (b, i, v1)),
                  *tab_specs, *tab_specs,
                  pl.BlockSpec((GMLP_HEADS, HEAD_DIM), lambda b, i: (0, 0)),
                  pl.BlockSpec((GMLP_HEADS, BLOCK, BLOCK), lambda b, i: (0, 0, 0)),
                  pl.BlockSpec((BLOCK, GMLP_HEADS), lambda b, i: (0, 0)),
                  pl.BlockSpec((1, aw + gw), lambda b, i: (0, 0))],
        out_specs=pl.BlockSpec((1, tq, aw + gw), lambda b, i: (b, i, 0)),
        out_shape=jax.ShapeDtypeStruct((bsz, s, aw + gw), BF16),
        scratch_shapes=[pltpu.VMEM((tq, aw), F32), pltpu.VMEM((tq, gw), F32)],
        compiler_params=pltpu.CompilerParams(dimension_semantics=("arbitrary", "arbitrary"),
                                             vmem_limit_bytes=VMEM_LIMIT),
        name="mixer",
    )(sink, z, z, z, z, z, z, z, z, cos, cos, cos, sin, sin, sin, gn, ws, bs.T,
      gg.reshape(1, aw + gw))


def _split_bf16(x):
    hi = x.astype(BF16)
    lo = (x - hi.astype(F32)).astype(BF16)
    return hi, lo


def _outproj_finish(mix_ref, x_ref, mod_ref, gpost_ref, gpre_ref, wr_ref, br_ref,
                    h_ref, f_ref, eid_ref, ew_ref):
    state = {}

    def residual():
        h1 = x_ref[0] + _rms(mix_ref[...], mod_ref[0][2:3] * gpost_ref[...])
        h_ref[0] = h1
        state["h1"] = h1

    def prenorm():
        mod = mod_ref[0]
        f = _rms(state["h1"], gpre_ref[...] * (1.0 + mod[4:5])) + mod[3:4]
        f_ref[...] = f
        state["f"] = f

    def logits():
        f_hi, f_lo = _split_bf16(state["f"])
        tm = f_hi.shape[0]
        r = jnp.dot(jnp.concatenate([f_hi, f_lo], axis=0), wr_ref[...],
                    preferred_element_type=F32)
        lg = (r[:tm, :LANES] + r[:tm, LANES:]) + (r[tm:, :LANES] + r[tm:, LANES:])
        state["logits"] = lg.T[:ROUTER_ROWS] + br_ref[...]

    def route():
        _route(state["logits"], eid_ref, ew_ref)

    return [residual, prenorm, logits, route]


def _route(logits, eid_ref, ew_ref):
    gl = [logits[g:g + 1] for g in range(N_GROUPS)]
    gmax = functools.reduce(jnp.maximum, gl)
    gidx = jnp.full(gmax.shape, N_GROUPS - 1, I32)
    for g in range(N_GROUPS - 2, -1, -1):
        gidx = jnp.where(gl[g] == gmax, g, gidx)
    gval = 1.0 / functools.reduce(lambda a, b: a + b, [jnp.exp(v - gmax) for v in gl])

    es = []
    for e in range(EXPERTS_PER_GROUP):
        r = N_GROUPS + (N_GROUPS - 1) * EXPERTS_PER_GROUP + e
        v = logits[r:r + 1]
        for g in range(N_GROUPS - 2, -1, -1):
            r = N_GROUPS + g * EXPERTS_PER_GROUP + e
            v = jnp.where(gidx == g, logits[r:r + 1], v)
        es.append(v)
    m1 = functools.reduce(jnp.maximum, es)
    i1 = jnp.full(m1.shape, EXPERTS_PER_GROUP - 1, I32)
    for e in range(EXPERTS_PER_GROUP - 2, -1, -1):
        i1 = jnp.where(es[e] == m1, e, i1)
    rest = [jnp.where(i1 == e, -jnp.inf, es[e]) for e in range(EXPERTS_PER_GROUP)]
    m2 = functools.reduce(jnp.maximum, rest)
    i2 = jnp.full(m2.shape, EXPERTS_PER_GROUP - 1, I32)
    for e in range(EXPERTS_PER_GROUP - 2, -1, -1):
        i2 = jnp.where(rest[e] == m2, e, i2)
    p2 = jnp.exp(m2 - m1)
    w1 = gval / (1.0 + p2)
    w2 = gval * p2 / (1.0 + p2)
    eid_ref[0:1, :] = gidx * EXPERTS_PER_GROUP + i1
    eid_ref[1:2, :] = gidx * EXPERTS_PER_GROUP + i2
    ew_ref[0:1, :] = w1
    ew_ref[1:2, :] = w2


def _outproj(ycat, w_out_bf16, x, mod, gpost, gpre, wr, br):
    bsz, s, d = x.shape
    t = bsz * s
    tm = min(OUTPROJ_TILE, s)
    nti = s // tm

    def outer(y_hbm, w_hbm, x_hbm, mod_hbm, gpost_hbm, gpre_hbm, wr_hbm, br_hbm,
              h_hbm, f_hbm, eid_hbm, ew_hbm, mix, wr_split, wr_f32):
        pltpu.sync_copy(wr_hbm, wr_f32)
        w_hi, w_lo = _split_bf16(wr_f32[...])
        wr_split[:, :LANES] = w_hi
        wr_split[:, LANES:] = w_lo

        def body(y_ref, w_ref, x_ref, mod_ref, gpost_ref, gpre_ref, br_ref,
                 h_ref, f_ref, eid_ref, ew_ref):
            mix[...] = jnp.dot(y_ref[0], w_ref[...], preferred_element_type=F32)
            for stage in _outproj_finish(mix, x_ref, mod_ref, gpost_ref, gpre_ref, wr_split, br_ref,
                                         h_ref, f_ref, eid_ref, ew_ref):
                stage()

        pltpu.emit_pipeline(
            body,
            grid=(bsz, nti),
            in_specs=[pl.BlockSpec((1, tm, d), lambda b, i: (b, i, 0)),
                      pl.BlockSpec((d, d), lambda b, i: (0, 0)),
                      pl.BlockSpec((1, tm, d), lambda b, i: (b, i, 0)),
                      pl.BlockSpec((1, N_MOD, d), lambda b, i: (b, 0, 0)),
                      pl.BlockSpec((1, d), lambda b, i: (0, 0)),
                      pl.BlockSpec((1, d), lambda b, i: (0, 0)),
                      pl.BlockSpec((ROUTER_ROWS, 1), lambda b, i: (0, 0))],
            out_specs=[pl.BlockSpec((1, tm, d), lambda b, i: (b, i, 0)),
                       pl.BlockSpec((tm, d), lambda b, i: (b * nti + i, 0)),
                       pl.BlockSpec((TOP_K, tm), lambda b, i: (0, b * nti + i)),
                       pl.BlockSpec((TOP_K, tm), lambda b, i: (0, b * nti + i))],
        )(y_hbm, w_hbm, x_hbm, mod_hbm, gpost_hbm, gpre_hbm, br_hbm, h_hbm, f_hbm, eid_hbm, ew_hbm)

    return pl.pallas_call(
        outer,
        in_specs=[pl.BlockSpec(memory_space=pl.ANY)] * 8,
        out_specs=[pl.BlockSpec(memory_space=pl.ANY)] * 4,
        out_shape=[jax.ShapeDtypeStruct((bsz, s, d), F32),
                   jax.ShapeDtypeStruct((t, d), F32),
                   jax.ShapeDtypeStruct((TOP_K, t), I32),
                   jax.ShapeDtypeStruct((TOP_K, t), F32)],
        scratch_shapes=[pltpu.VMEM((tm, d), F32), pltpu.VMEM((d, 2 * LANES), BF16),
                        pltpu.VMEM((d, LANES), F32)],
        compiler_params=pltpu.CompilerParams(vmem_limit_bytes=LARGE_VMEM_LIMIT),
        name="outproj",
    )(ycat, w_out_bf16, x, mod, gpost.reshape(1, d), gpre.reshape(1, d), wr, br)


def _slots_kernel(eid_ref, pos_ref, tab_ref, rank_scr, *, n_slots):
    t = eid_ref.shape[1]
    chunk = min(512, t)
    nchunk = t // chunk
    tri = jnp.where(lax.broadcasted_iota(I32, (chunk, chunk), 0)
                    <= lax.broadcasted_iota(I32, (chunk, chunk), 1), 1.0, 0.0).astype(BF16)
    e_io = lax.broadcasted_iota(I32, (N_EXPERTS, chunk), 0)

    cnt = jnp.zeros((N_EXPERTS, 1), F32)
    for k in range(TOP_K):
        def rank_body(c, carry, k=k):
            off = pl.multiple_of(c * chunk, chunk)
            onehot = e_io == eid_ref[pl.ds(k, 1), pl.ds(off, chunk)]
            ones = jnp.where(onehot, 1.0, 0.0)
            prefix = jnp.dot(ones.astype(BF16), tri, preferred_element_type=F32) + carry
            rank = jnp.sum(jnp.where(onehot, prefix, 0.0), axis=0, keepdims=True) - 1.0
            rank_scr[pl.ds(k, 1), pl.ds(off, chunk)] = rank
            return carry + jnp.sum(ones, axis=1, keepdims=True)
        cnt = lax.fori_loop(0, nchunk, rank_body, cnt)

    padded = jnp.floor((cnt + (SLOT_TILE - 1)) * (1.0 / SLOT_TILE)) * SLOT_TILE
    sub = lax.broadcasted_iota(I32, (N_EXPERTS, LANES), 0)
    lan = lax.broadcasted_iota(I32, (N_EXPERTS, LANES), 1)
    padded_row = jnp.sum(jnp.where(sub == lan, padded, 0.0), axis=0, keepdims=True)
    start = jnp.sum(jnp.where(lan < sub, padded_row, 0.0), axis=1, keepdims=True)
    end = start + padded
    pad_lo = start + cnt

    for k in range(TOP_K):
        def pos_body(c, carry, k=k):
            off = pl.multiple_of(c * chunk, chunk)
            onehot = e_io == eid_ref[pl.ds(k, 1), pl.ds(off, chunk)]
            base = jnp.sum(jnp.where(onehot, start + SLOT_TILE, 0.0), axis=0, keepdims=True)
            pos_ref[pl.ds(k, 1), pl.ds(off, chunk)] = (
                base + rank_scr[pl.ds(k, 1), pl.ds(off, chunk)]).astype(I32)
            return carry
        lax.fori_loop(0, nchunk, pos_body, 0)

    lo_row = jnp.sum(jnp.where(sub == lan, pad_lo, 0.0), axis=0, keepdims=True)
    hi_row = jnp.sum(jnp.where(sub + PAD_HI_LANE == lan, end, 0.0), axis=0, keepdims=True)
    first_row = jnp.sum(jnp.where(sub + TILE_START_LANE == lan, start * (1.0 / SLOT_TILE), 0.0),
                        axis=0, keepdims=True)
    lane_row = lax.broadcasted_iota(I32, (1, LANES), 1)
    total = jnp.sum(padded_row, axis=1, keepdims=True)
    tail = (jnp.where(lane_row == N_EXPERTS, total, 0.0)
            + jnp.where(lane_row == PAD_HI_LANE + N_EXPERTS, float(n_slots), 0.0)
            + jnp.where(lane_row == TILE_START_LANE + N_EXPERTS, total * (1.0 / SLOT_TILE), 0.0))
    tab_ref[...] = (lo_row + hi_row + first_row + tail).astype(I32)


def _slots(eid, n_tiles):
    t = eid.shape[1]
    return pl.pallas_call(
        functools.partial(_slots_kernel, n_slots=n_tiles * SLOT_TILE),
        out_shape=[jax.ShapeDtypeStruct((TOP_K, t), I32),
                   jax.ShapeDtypeStruct((1, LANES), I32)],
        scratch_shapes=[pltpu.VMEM((TOP_K, t), F32)],
        compiler_params=pltpu.CompilerParams(vmem_limit_bytes=VMEM_LIMIT),
        name="slots",
    )(eid)


def _invert_kernel(pos_ref, tab_ref, inv_ref, *, n_pairs, n_slots):
    group = 8
    spare = n_pairs
    for e in range(N_EXPERTS + 1):
        lo, hi = tab_ref[e], tab_ref[PAD_HI_LANE + e]

        def fill(i, c, lo=lo, spare=spare):
            for k in range(group):
                inv_ref[SLOT_TILE + lo + i * group + k] = spare + i * group + k
            return c
        lax.fori_loop(0, (hi - lo + group - 1) // group, fill, 0)
        spare = spare + hi - lo

    def guard(r, c):
        inv_ref[r] = n_slots + r
        inv_ref[SLOT_TILE + n_slots + r] = n_slots + r
        return c
    lax.fori_loop(0, SLOT_TILE, guard, 0, unroll=8)

    def place(n, c):
        inv_ref[pos_ref[n]] = n
        return c
    lax.fori_loop(0, n_pairs, place, 0, unroll=16)


def _invert(pos_flat, tab, n_tiles):
    n_pairs = pos_flat.shape[0]
    n_slots = n_tiles * SLOT_TILE
    return pl.pallas_call(
        functools.partial(_invert_kernel, n_pairs=n_pairs, n_slots=n_slots),
        in_specs=[pl.BlockSpec(memory_space=pltpu.SMEM), pl.BlockSpec(memory_space=pltpu.SMEM)],
        out_specs=pl.BlockSpec(memory_space=pltpu.SMEM),
        out_shape=jax.ShapeDtypeStruct((n_slots + 2 * SLOT_TILE,), I32),
        name="invert",
    )(pos_flat, tab)


def _experts_kernel(inv_ref, tab_ref, f_ref, wg_ref, wu_ref, wd_ref, y_ref,
                    wbg, wbu, wbd, xa, xb, ya, yb, gsem, ssem, zsem, *, n_tokens, n_slots):
    e = pl.program_id(0)
    j = pl.program_id(1)
    fc = wg_ref.shape[2]
    total = tab_ref[TILE_START_LANE + N_EXPERTS]

    def gather_copy(tile, r, xbuf):
        v = inv_ref[(tile + 1) * SLOT_TILE + r]
        return pltpu.make_async_copy(f_ref.at[pl.ds(v & (n_tokens - 1), 1)], xbuf.at[pl.ds(r, 1)], gsem)

    def scatter_copy(tile, r, ybuf):
        v = inv_ref[(tile + 1) * SLOT_TILE + r]
        return pltpu.make_async_copy(ybuf.at[pl.ds(r, 1)], y_ref.at[pl.ds(v, 1)], ssem)

    def gather_wait(xbuf):
        pltpu.make_async_copy(f_ref.at[pl.ds(0, SLOT_TILE)], xbuf, gsem).wait()

    def scatter_wait(ybuf):
        pltpu.make_async_copy(ybuf, y_ref.at[pl.ds(0, SLOT_TILE)], ssem).wait()

    def rows(fn):
        def body(r, c):
            fn(r)
            return c
        lax.fori_loop(0, SLOT_TILE, body, 0, unroll=8)

    @pl.when((e == 0) & (j == 0))
    def _():
        ya[...] = jnp.zeros(ya.shape, ya.dtype)
        yb[...] = jnp.zeros(yb.shape, yb.dtype)
        rows(lambda r: gather_copy(0, r, xa).start())

        def zero_copy(tile):
            dst = pl.multiple_of(tile * SLOT_TILE, SLOT_TILE)
            return pltpu.make_async_copy(ya, y_ref.at[pl.ds(dst, SLOT_TILE)], zsem)

        def zstart(tile, c):
            zero_copy(tile).start()
            return c

        def zwait(tile, c):
            zero_copy(tile).wait()
            return c
        lax.fori_loop(total, n_slots // SLOT_TILE, zstart, 0)
        lax.fori_loop(total, n_slots // SLOT_TILE, zwait, 0)

    @pl.when(e < N_EXPERTS)
    def _():
        slot = e % 2
        col = pl.multiple_of(j * fc, fc)
        wbg[slot, :, pl.ds(col, fc)] = wg_ref[0].astype(BF16)
        wbu[slot, :, pl.ds(col, fc)] = wu_ref[0].astype(BF16)
        wbd[slot, pl.ds(col, fc), :] = wd_ref[0].astype(BF16)

    prev = jnp.maximum(e - 1, 0)
    first = tab_ref[TILE_START_LANE + prev]
    count = jnp.where(e == 0, 0, tab_ref[TILE_START_LANE + prev + 1] - first)
    lo = first + (count * j) // N_WCHUNK
    hi = first + (count * (j + 1)) // N_WCHUNK
    wslot = prev % 2

    def tile(t, c):
        def run(xcur, xnext, ycur, yprev, ws):
            gather_wait(xcur)
            for r in range(SLOT_TILE):
                gather_copy(t + 1, r, xnext).start()
            for r in range(SLOT_TILE):
                scatter_copy(t - 1, r, yprev).start()
            x = xcur[...].astype(BF16)
            hg = jnp.dot(x, wbg[ws], preferred_element_type=F32)
            hu = jnp.dot(x, wbu[ws], preferred_element_type=F32)
            hid = (_silu(hg) * hu).astype(BF16)
            ycur[...] = jnp.dot(hid, wbd[ws], preferred_element_type=F32)
            scatter_wait(yprev)

        for ws in range(2):
            for parity, bufs in enumerate(((xa, xb, ya, yb), (xb, xa, yb, ya))):
                @pl.when((wslot == ws) & (t % 2 == parity))
                def _(bufs=bufs, ws=ws):
                    run(*bufs, ws)
        return c
    lax.fori_loop(lo, hi, tile, 0)

    @pl.when((e == N_EXPERTS) & (j == N_WCHUNK - 1))
    def _():
        last = total - 1

        def finish(xnext, ylast):
            gather_wait(xnext)
            rows(lambda r: scatter_copy(last, r, ylast).start())
            scatter_wait(ylast)

        @pl.when(last % 2 == 0)
        def _():
            finish(xb, ya)

        @pl.when(last % 2 == 1)
        def _():
            finish(xa, yb)


def _experts(inv, tab, f, wg, wu, wd, n_tiles):
    t, d = f.shape
    assert t & (t - 1) == 0
    fdim = wg.shape[2]
    fc = fdim // N_WCHUNK
    n_slots = n_tiles * SLOT_TILE

    def expert(e):
        return jnp.minimum(e, N_EXPERTS - 1)

    def chunk(e, j):
        return jnp.where(e == N_EXPERTS, N_WCHUNK - 1, j)

    return pl.pallas_call(
        functools.partial(_experts_kernel, n_tokens=t, n_slots=n_slots),
        grid_spec=pltpu.PrefetchScalarGridSpec(
            num_scalar_prefetch=2,
            grid=(N_EXPERTS + 1, N_WCHUNK),
            in_specs=[pl.BlockSpec(memory_space=pl.ANY),
                      pl.BlockSpec((1, d, fc), lambda e, j, inv, tab: (expert(e), 0, chunk(e, j))),
                      pl.BlockSpec((1, d, fc), lambda e, j, inv, tab: (expert(e), 0, chunk(e, j))),
                      pl.BlockSpec((1, fc, d), lambda e, j, inv, tab: (expert(e), chunk(e, j), 0))],
            out_specs=pl.BlockSpec(memory_space=pl.ANY),
            scratch_shapes=[pltpu.VMEM((2, d, fdim), BF16), pltpu.VMEM((2, d, fdim), BF16),
                            pltpu.VMEM((2, fdim, d), BF16),
                            pltpu.VMEM((SLOT_TILE, d), F32), pltpu.VMEM((SLOT_TILE, d), F32),
                            pltpu.VMEM((SLOT_TILE, d), F32), pltpu.VMEM((SLOT_TILE, d), F32),
                            pltpu.SemaphoreType.DMA, pltpu.SemaphoreType.DMA, pltpu.SemaphoreType.DMA]),
        out_shape=jax.ShapeDtypeStruct((n_slots + SLOT_TILE, d), F32),
        compiler_params=pltpu.CompilerParams(dimension_semantics=("arbitrary", "arbitrary"),
                                             vmem_limit_bytes=LARGE_VMEM_LIMIT),
        name="experts",
    )(inv, tab, f, wg, wu, wd)


def _final_kernel(y0_ref, y1_ref, h_ref, ew_ref, mod_ref, g_ref, o_ref):
    ew = ew_ref[...]
    y = ew[:, 0:1] * y0_ref[...] + ew[:, 1:2] * y1_ref[...]
    o_ref[0] = h_ref[0] + _rms(y, mod_ref[0][5:6] * g_ref[...])


def _final(y_pairs, h1, ew_t, mod, g):
    bsz, s, d = h1.shape
    t = bsz * s
    tg = min(FINAL_TILE, s)
    nti = s // tg
    stream = pl.Buffered(STREAM_BUFFERS)
    pipeline = pltpu.emit_pipeline(
        _final_kernel,
        grid=(bsz, nti),
        in_specs=[pl.BlockSpec((tg, d), lambda b, i: (b * nti + i, 0), pipeline_mode=stream),
                  pl.BlockSpec((tg, d), lambda b, i: (t // tg + b * nti + i, 0), pipeline_mode=stream),
                  pl.BlockSpec((1, tg, d), lambda b, i: (b, i, 0), pipeline_mode=stream),
                  pl.BlockSpec((tg, TOP_K), lambda b, i: (b * nti + i, 0)),
                  pl.BlockSpec((1, N_MOD, d), lambda b, i: (b, 0, 0)),
                  pl.BlockSpec((1, d), lambda b, i: (0, 0))],
        out_specs=[pl.BlockSpec((1, tg, d), lambda b, i: (b, i, 0))],
    )

    def outer(y0_hbm, y1_hbm, h_hbm, ew_hbm, mod_hbm, g_hbm, o_hbm):
        pipeline(y0_hbm, y1_hbm, h_hbm, ew_hbm, mod_hbm, g_hbm, o_hbm)

    return pl.pallas_call(
        outer,
        in_specs=[pl.BlockSpec(memory_space=pl.ANY)] * 6,
        out_specs=pl.BlockSpec(memory_space=pl.ANY),
        out_shape=jax.ShapeDtypeStruct((bsz, s, d), F32),
        compiler_params=pltpu.CompilerParams(vmem_limit_bytes=VMEM_LIMIT),
        name="final",
    )(y_pairs, y_pairs, h1, ew_t, mod, g.reshape(1, d))


def kernel(x, c, positions, w_ada, b_ada, g_mix_pre, g_mix_post, w_in, sink_logits, gmlp_norm_g,
           w_spatial, b_spatial, g_group_out, w_out, g_ffn_pre, g_ffn_post, w_router_group,
           b_router_group, w_router_expert, b_router_expert, w_gate, w_up, w_down):
    bsz, s, d = x.shape
    t = bsz * s
    depth = w_ada.shape[0]
    fdim = w_gate.shape[-1]
    n_tiles = (TOP_K * t) // SLOT_TILE + N_EXPERTS
    n_slots = n_tiles * SLOT_TILE
    cos, sin = _rope_tables(positions)
    h = x
    for l in range(depth):
        mod = _ada(c, w_ada[l], b_ada[l]).reshape(bsz, N_MOD, d)
        z = _inproj(h, mod, g_mix_pre[l], w_in[l].astype(BF16))
        ycat = _mixer(z, cos, sin, sink_logits[l], gmlp_norm_g[l], w_spatial[l],
                      b_spatial[l], g_group_out[l])
        wr = jnp.concatenate([w_router_group[l], w_router_expert[l].reshape(d, N_EXPERTS)], axis=1)
        wr = jnp.pad(wr, ((0, 0), (0, LANES - N_GROUPS - N_EXPERTS)))
        br = jnp.concatenate([b_router_group[l], b_router_expert[l].reshape(N_EXPERTS)])
        br = jnp.pad(br, (0, ROUTER_ROWS - N_GROUPS - N_EXPERTS)).reshape(ROUTER_ROWS, 1)
        h1, f, eid, ew = _outproj(ycat, w_out[l].astype(BF16), h, mod, g_mix_post[l], g_ffn_pre[l],
                                  wr, br)
        pos, tab = _slots(eid, n_tiles)
        tab = tab.reshape(LANES)
        inv = _invert(pos.reshape(TOP_K * t), tab, n_tiles)
        y_pairs = _experts(inv, tab, f,
                           w_gate[l].reshape(N_EXPERTS, d, fdim),
                           w_up[l].reshape(N_EXPERTS, d, fdim),
                           w_down[l].reshape(N_EXPERTS, fdim, d), n_tiles)
        h = _final(y_pairs, h1, ew.T, mod, g_ffn_post[l])
    return h
```
